```python
import jax, jax.numpy as jnp
from jax import lax
import numpy as np

D_MODEL = 2048
BATCH = 4
SEQ = 2048
DEPTH = 2

GRID_W = 64
CTX_LEN = 256
EPS = 1e-6
NEG_INF = -1e30
ROPE_THETA = 10000.0
BLOCK = 128
N_MOD = 6

MLA_HEADS = 8
MLA_Q_RANK = 512
MLA_KV_RANK = 256
MLA_NOPE = 128
MLA_ROPE = 64
MLA_V = 128

LRU_WIDTH = 1024
LRU_BLOCKS = 8
LRU_CONV = 4
LRU_CONV_LEFT = 2
LRU_C = 8.0

HEAD_DIM = 128
WIN_HEADS = 8
WIN_KV_HEADS = 2
WINDOW = 128
NA_HEADS = 8
NA_ROWS = 8
NA_COLS = 16

N_EXPERTS = 32
N_GROUPS = 8
EXPERTS_PER_GROUP = N_EXPERTS // N_GROUPS
TOP_K = 2
EXPERT_FF = 512

L0_SPLITS = (MLA_Q_RANK, MLA_Q_RANK + MLA_KV_RANK, MLA_Q_RANK + MLA_KV_RANK + MLA_ROPE,
             MLA_Q_RANK + MLA_KV_RANK + MLA_ROPE + LRU_WIDTH)
L0_IN = L0_SPLITS[-1] + LRU_WIDTH
L0_OUT = MLA_HEADS * MLA_V + LRU_WIDTH
WQ_COLS = WIN_HEADS * HEAD_DIM
WKV_COLS = WIN_KV_HEADS * HEAD_DIM
NA_COLS_W = NA_HEADS * HEAD_DIM
L1_SPLITS = (WQ_COLS, WQ_COLS + WKV_COLS, WQ_COLS + 2 * WKV_COLS, WQ_COLS + 2 * WKV_COLS + NA_COLS_W,
             WQ_COLS + 2 * WKV_COLS + 2 * NA_COLS_W)
L1_IN = L1_SPLITS[-1] + NA_COLS_W
L1_OUT = (WIN_HEADS + NA_HEADS) * HEAD_DIM

kernel_name = 'hybrid_dit_mla_rglru_swa_natten_moe'


def rmsnorm(x, g):
    xf = x.astype(jnp.float32)
    y = xf * lax.rsqrt(jnp.mean(xf * xf, axis=-1, keepdims=True) + EPS)
    return (y * g.astype(jnp.float32)).astype(x.dtype)


def modulate(h, shift, scale):
    return h * (1.0 + scale) + shift


def axial_rope(x):
    T, d = x.shape[1], x.shape[-1]
    half, quarter = d // 2, d // 4
    t = jnp.arange(T)
    pos = jnp.stack([t // GRID_W, t % GRID_W], axis=-1).astype(jnp.float32)
    inv_freq = ROPE_THETA ** (-jnp.arange(0, half, 2, dtype=jnp.float32) / half)
    ang = pos[:, :, None] * inv_freq
    cos = jnp.cos(ang)[None, :, None]
    sin = jnp.sin(ang)[None, :, None]
    xs = x.astype(jnp.float32).reshape(x.shape[:-1] + (2, 2, quarter))
    x1, x2 = xs[..., 0, :], xs[..., 1, :]
    out = jnp.stack([x1 * cos - x2 * sin, x2 * cos + x1 * sin], axis=-2)
    return out.reshape(x.shape).astype(x.dtype)


def softmax_sink(s, sink_b):
    if sink_b is None:
        return jax.nn.softmax(s, axis=-1)
    sink_col = jnp.broadcast_to(sink_b.astype(jnp.float32), s.shape[:-1] + (1,))
    return jax.nn.softmax(jnp.concatenate([s, sink_col], axis=-1), axis=-1)[..., :-1]


def full_attention(q, k, v, sink=None):
    scale = q.shape[-1] ** -0.5
    s = jnp.einsum('bqhd,bkhd->bhqk', q, k, preferred_element_type=jnp.float32) * scale
    p = softmax_sink(s, None if sink is None else sink[:, None, None])
    return jnp.einsum('bhqk,bkhd->bqhd', p.astype(v.dtype), v)


def dense_block_attention(q, k, v, k_ctx, v_ctx):
    B, S, H, dq = q.shape
    scale = dq ** -0.5
    keys = jnp.concatenate([k_ctx, k], axis=1)
    vals = jnp.concatenate([v_ctx, v], axis=1)
    qb = jnp.moveaxis(q.reshape(B, S // BLOCK, BLOCK, H, dq), 1, 0)

    def block(qblk):
        s = jnp.einsum('bqhd,bkhd->bhqk', qblk, keys, preferred_element_type=jnp.float32) * scale
        p = jax.nn.softmax(s, axis=-1)
        return jnp.einsum('bhqk,bkhd->bqhd', p.astype(vals.dtype), vals)

    out = lax.map(block, qb)
    return jnp.moveaxis(out, 0, 1).reshape(B, S, H * vals.shape[-1])


def mla_queries(cq, p, rotate):
    B, T, _ = cq.shape
    q = (rmsnorm(cq, p['q_norm']) @ p['w_uq']).reshape(B, T, MLA_HEADS, MLA_NOPE + MLA_ROPE)
    q_nope, q_rope = q[..., :MLA_NOPE], q[..., MLA_NOPE:]
    if rotate:
        q_rope = axial_rope(q_rope)
    return jnp.concatenate([q_nope, q_rope], axis=-1)


def mla_keys_values(ckv, k_rope, p, rotate):
    B, T, _ = ckv.shape
    kv = (rmsnorm(ckv, p['kv_norm']) @ p['w_ukv']).reshape(B, T, MLA_HEADS, MLA_NOPE + MLA_V)
    k_nope, v = kv[..., :MLA_NOPE], kv[..., MLA_NOPE:]
    k_rope = k_rope[:, :, None, :]
    if rotate:
        k_rope = axial_rope(k_rope)
    k = jnp.concatenate([k_nope, jnp.broadcast_to(k_rope, (B, T, MLA_HEADS, MLA_ROPE))], axis=-1)
    return k, v


def centred_conv(x, w, b):
    T = x.shape[1]
    xp = jnp.pad(x, ((0, 0), (LRU_CONV_LEFT, LRU_CONV - 1 - LRU_CONV_LEFT), (0, 0)))
    out = b
    for tap in range(LRU_CONV):
        out = out + xp[:, tap:tap + T] * w[tap]
    return out


def rglru_coeffs(u, p, d):
    B, T, W = u.shape
    ub = u.reshape(B, T, LRU_BLOCKS, W // LRU_BLOCKS)
    gate_a = jnp.einsum('btnk,nkj->btnj', ub, p['gate_a_w'][d].astype(jnp.float32)).reshape(B, T, W) + p['gate_a_b'][d]
    gate_x = jnp.einsum('btnk,nkj->btnj', ub, p['gate_x_w'][d].astype(jnp.float32)).reshape(B, T, W) + p['gate_x_b'][d]
    r = jax.nn.sigmoid(gate_a)
    i = jax.nn.sigmoid(gate_x)
    log_a = -LRU_C * r * jax.nn.softplus(-p['lru_lambda'][d].astype(jnp.float32))
    a = jnp.exp(log_a)
    b = jnp.sqrt(-jnp.expm1(2.0 * log_a)) * (i * u)
    return a, b


def _affine_combine(first, second):
    a1, b1 = first
    a2, b2 = second
    return a1 * a2, a2 * b1 + b2


def linear_scan(a, b, h0, reverse):
    acc_a, acc_b = lax.associative_scan(_affine_combine, (a, b), reverse=reverse, axis=1)
    if h0 is None:
        return acc_b
    return acc_a * h0[:, None, :] + acc_b


def rglru_mixer(xr_l, gr_l, xr_c, gr_c, p, ctx_out):
    u_l = centred_conv(xr_l, p['conv_w'], p['conv_b']).astype(jnp.float32)
    u_c = centred_conv(xr_c, p['conv_w'], p['conv_b']).astype(jnp.float32)
    a, b = rglru_coeffs(u_c, p, 0)
    hc_f = linear_scan(a, b, None, False)
    a, b = rglru_coeffs(u_l, p, 0)
    hl_f = linear_scan(a, b, hc_f[:, -1], False)
    a, b = rglru_coeffs(u_c, p, 1)
    hc_b = linear_scan(a, b, None, True)
    a, b = rglru_coeffs(u_l, p, 1)
    hl_b = linear_scan(a, b, hc_b[:, 0], True)
    y_l = (hl_f + hl_b).astype(xr_l.dtype) * jax.nn.gelu(gr_l)
    y_c = (hc_f + hc_b).astype(xr_c.dtype) * jax.nn.gelu(gr_c) if ctx_out else None
    return y_l, y_c


def mla_rglru_mixer(hl, hc, p, ctx_out):
    cq_l, ckv_l, kr_l, xr_l, gr_l = jnp.split(hl @ p['w_in'], L0_SPLITS, axis=-1)
    cq_c, ckv_c, kr_c, xr_c, gr_c = jnp.split(hc @ p['w_in'], L0_SPLITS, axis=-1)
    q_l = mla_queries(cq_l, p, True)
    k_l, v_l = mla_keys_values(ckv_l, kr_l, p, True)
    k_c, v_c = mla_keys_values(ckv_c, kr_c, p, False)
    att_l = dense_block_attention(q_l, k_l, v_l, k_c, v_c)
    rnn_l, rnn_c = rglru_mixer(xr_l, gr_l, xr_c, gr_c, p, ctx_out)
    y_l = jnp.concatenate([att_l, rnn_l], axis=-1)
    y_c = None
    if ctx_out:
        B, L = hc.shape[:2]
        q_c = mla_queries(cq_c, p, False)
        att_c = full_attention(q_c, k_c, v_c).reshape(B, L, MLA_HEADS * MLA_V)
        y_c = jnp.concatenate([att_c, rnn_c], axis=-1)
    return y_l, y_c


def window_attention(q, k, v, k_ctx, v_ctx, sink):
    B, S, H, d = q.shape
    Hk = k.shape[2]
    G = H // Hk
    nb = S // BLOCK
    span = BLOCK + 2 * WINDOW
    scale = d ** -0.5
    kp = jnp.pad(k, ((0, 0), (WINDOW, WINDOW), (0, 0), (0, 0)))
    vp = jnp.pad(v, ((0, 0), (WINDOW, WINDOW), (0, 0), (0, 0)))
    idx = jnp.arange(nb)[:, None] * BLOCK + jnp.arange(span)[None, :]
    kb = kp[:, idx]
    vb = vp[:, idx]
    qb = q.reshape(B, nb, BLOCK, Hk, G, d)
    s_loc = jnp.einsum('bnqkgd,bnjkd->bnkgqj', qb, kb, preferred_element_type=jnp.float32) * scale
    qpos = jnp.arange(nb)[:, None] * BLOCK + jnp.arange(BLOCK)[None, :]
    kpos = idx - WINDOW
    valid = ((kpos[:, None, :] >= 0) & (kpos[:, None, :] < S)
             & (jnp.abs(qpos[:, :, None] - kpos[:, None, :]) <= WINDOW))
    s_loc = jnp.where(valid[None, :, None, None], s_loc, NEG_INF)
    s_ctx = jnp.einsum('bnqkgd,bckd->bnkgqc', qb, k_ctx, preferred_element_type=jnp.float32) * scale
    p = softmax_sink(jnp.concatenate([s_loc, s_ctx], axis=-1), sink.reshape(Hk, G, 1, 1))
    p_loc, p_ctx = p[..., :span], p[..., span:]
    out = (jnp.einsum('bnkgqj,bnjkd->bnqkgd', p_loc.astype(v.dtype), vb)
           + jnp.einsum('bnkgqc,bckd->bnqkgd', p_ctx.astype(v.dtype), v_ctx))
    return out.reshape(B, S, H * d)


def neighbourhood_attention(q, k, v, k_ctx, v_ctx, rpb):
    B, S, H, d = q.shape
    rows = S // GRID_W
    kr = min(NA_ROWS, rows)
    kc = NA_COLS
    scale = d ** -0.5
    qg = q.reshape(B, rows, GRID_W, H, d)
    kg = k.reshape(B, rows, GRID_W, H, d)
    vg = v.reshape(B, rows, GRID_W, H, d)
    cols = jnp.arange(GRID_W)
    col_idx = jnp.clip(cols - kc // 2, 0, GRID_W - kc)[:, None] + jnp.arange(kc)[None, :]
    coff = col_idx - cols[:, None] + (NA_COLS - 1)
    n_win = kr * kc

    def row_block(r):
        r0 = jnp.clip(r - kr // 2, 0, rows - kr)
        kwin = lax.dynamic_slice_in_dim(kg, r0, kr, axis=1)[:, :, col_idx]
        vwin = lax.dynamic_slice_in_dim(vg, r0, kr, axis=1)[:, :, col_idx]
        qrow = lax.dynamic_index_in_dim(qg, r, axis=1, keepdims=False)
        roff = r0 + jnp.arange(kr) - r + (NA_ROWS - 1)
        bias = rpb[:, roff][:, :, coff]
        s = jnp.einsum('bqhd,brqjhd->bhqrj', qrow, kwin, preferred_element_type=jnp.float32) * scale
        s = s + jnp.transpose(bias, (0, 2, 1, 3))[None].astype(jnp.float32)
        s_ctx = jnp.einsum('bqhd,bchd->bhqc', qrow, k_ctx, preferred_element_type=jnp.float32) * scale
        p = jax.nn.softmax(jnp.concatenate([s.reshape(B, H, GRID_W, n_win), s_ctx], axis=-1), axis=-1)
        p_win = p[..., :n_win].reshape(B, H, GRID_W, kr, kc).astype(v.dtype)
        return (jnp.einsum('bhqrj,brqjhd->bqhd', p_win, vwin)
                + jnp.einsum('bhqc,bchd->bqhd', p[..., n_win:].astype(v.dtype), v_ctx))

    out = lax.map(row_block, jnp.arange(rows))
    return jnp.moveaxis(out, 0, 1).reshape(B, S, H * d)


def window_na_mixer(hl, hc, p, ctx_out):
    B, S, _ = hl.shape
    L = hc.shape[1]

    def heads(z, n):
        return z.reshape(z.shape[:2] + (n, HEAD_DIM))

    qw_l, kw_l, vw_l, qn_l, kn_l, vn_l = jnp.split(hl @ p['w_in'], L1_SPLITS, axis=-1)
    qw_c, kw_c, vw_c, qn_c, kn_c, vn_c = jnp.split(hc @ p['w_in'], L1_SPLITS, axis=-1)
    kw_c, vw_c = heads(kw_c, WIN_KV_HEADS), heads(vw_c, WIN_KV_HEADS)
    kn_c, vn_c = heads(kn_c, NA_HEADS), heads(vn_c, NA_HEADS)
    win_l = window_attention(axial_rope(heads(qw_l, WIN_HEADS)), axial_rope(heads(kw_l, WIN_KV_HEADS)),
                             heads(vw_l, WIN_KV_HEADS), kw_c, vw_c, p['sink'])
    na_l = neighbourhood_attention(heads(qn_l, NA_HEADS), heads(kn_l, NA_HEADS), heads(vn_l, NA_HEADS),
                                   kn_c, vn_c, p['rpb'])
    y_l = jnp.concatenate([win_l, na_l], axis=-1)
    y_c = None
    if ctx_out:
        rep = WIN_HEADS // WIN_KV_HEADS
        win_c = full_attention(heads(qw_c, WIN_HEADS), jnp.repeat(kw_c, rep, axis=2),
                               jnp.repeat(vw_c, rep, axis=2), p['sink']).reshape(B, L, WQ_COLS)
        na_c = full_attention(heads(qn_c, NA_HEADS), kn_c, vn_c).reshape(B, L, NA_COLS_W)
        y_c = jnp.concatenate([win_c, na_c], axis=-1)
    return y_l, y_c


def grouped_moe(h, router_w, router_b, w_gate, w_up, w_down):
    shp = h.shape
    t = h.reshape(-1, shp[-1])
    n = t.shape[0]
    scores = jax.nn.sigmoid(jnp.dot(t, router_w, preferred_element_type=jnp.float32))
    grouped = (scores + router_b.astype(jnp.float32)).reshape(n, N_GROUPS, EXPERTS_PER_GROUP)
    group_score = lax.top_k(grouped, TOP_K)[0].sum(axis=-1)
    g_sel = jnp.argmax(group_score, axis=-1)
    in_group = grouped[jnp.arange(n), g_sel]
    _, e_local = lax.top_k(in_group, TOP_K)
    e_sel = g_sel[:, None] * EXPERTS_PER_GROUP + e_local
    w_sel = jnp.take_along_axis(scores, e_sel, axis=1)
    w_sel = w_sel / jnp.sum(w_sel, axis=-1, keepdims=True)
    gates = jnp.sum(jax.nn.one_hot(e_sel, N_EXPERTS, dtype=jnp.float32) * w_sel[..., None], axis=1).astype(h.dtype)
    out = jnp.zeros_like(t)
    for e in range(N_EXPERTS):
        he = jax.nn.silu(t @ w_gate[e]) * (t @ w_up[e])
        out = out + gates[:, e:e + 1] * (he @ w_down[e])
    return out.reshape(shp)


def setup_inputs(seed: int = 0) -> dict:
    key = jax.random.key(seed)
    keys = iter(list(jax.random.split(key, 64)))

    def nrm(shape, scale):
        return jax.random.normal(next(keys), shape, jnp.float32) * scale

    def gain(n):
        return 1.0 + nrm((n,), 0.01)

    D = D_MODEL
    bw = LRU_WIDTH // LRU_BLOCKS
    inp = {}
    inp['x'] = nrm((BATCH, SEQ, D), 1.0)
    inp['c'] = nrm((BATCH, D), 1.0)
    inp['ctx'] = nrm((BATCH, CTX_LEN, D), 1.0)
    inp['c_ctx'] = nrm((D,), 1.0)
    inp['router_w'] = nrm((D, N_EXPERTS), D ** -0.5)
    inp['router_b'] = nrm((N_EXPERTS,), 0.01)
    inp['final_norm'] = gain(D)
    inp['l0_mod_w'] = nrm((D, N_MOD * D), 0.5 * D ** -0.5)
    inp['l0_mod_b'] = nrm((N_MOD * D,), 0.02)
    inp['l0_norm_mix'] = gain(D)
    inp['l0_norm_ffn'] = gain(D)
    inp['l0_w_in'] = nrm((D, L0_IN), D ** -0.5)
    inp['l0_q_norm'] = gain(MLA_Q_RANK)
    inp['l0_w_uq'] = nrm((MLA_Q_RANK, MLA_HEADS * (MLA_NOPE + MLA_ROPE)), MLA_Q_RANK ** -0.5)
    inp['l0_kv_norm'] = gain(MLA_KV_RANK)
    inp['l0_w_ukv'] = nrm((MLA_KV_RANK, MLA_HEADS * (MLA_NOPE + MLA_V)), MLA_KV_RANK ** -0.5)
    inp['l0_conv_w'] = nrm((LRU_CONV, LRU_WIDTH), 0.5)
    inp['l0_conv_b'] = nrm((LRU_WIDTH,), 0.02)
    inp['l0_gate_a_w'] = nrm((2, LRU_BLOCKS, bw, bw), bw ** -0.5)
    inp['l0_gate_a_b'] = nrm((2, LRU_WIDTH), 0.1)
    inp['l0_gate_x_w'] = nrm((2, LRU_BLOCKS, bw, bw), bw ** -0.5)
    inp['l0_gate_x_b'] = nrm((2, LRU_WIDTH), 0.1)
    u = jax.random.uniform(next(keys), (2, LRU_WIDTH), jnp.float32, 0.9, 0.999)
    a0 = u ** (1.0 / LRU_C)
    inp['l0_lru_lambda'] = jnp.log(a0) - jnp.log1p(-a0)
    inp['l0_w_out'] = nrm((L0_OUT, D), L0_OUT ** -0.5)
    inp['l0_exp_gate'] = nrm((N_EXPERTS, D, EXPERT_FF), D ** -0.5)
    inp['l0_exp_up'] = nrm((N_EXPERTS, D, EXPERT_FF), D ** -0.5)
    inp['l0_exp_down'] = nrm((N_EXPERTS, EXPERT_FF, D), EXPERT_FF ** -0.5)
    inp['l1_mod_w'] = nrm((D, N_MOD * D), 0.5 * D ** -0.5)
    inp['l1_mod_b'] = nrm((N_MOD * D,), 0.02)
    inp['l1_norm_mix'] = gain(D)
    inp['l1_norm_ffn'] = gain(D)
    inp['l1_w_in'] = nrm((D, L1_IN), D ** -0.5)
    inp['l1_sink'] = nrm((WIN_HEADS,), 1.0)
    inp['l1_rpb'] = nrm((NA_HEADS, 2 * NA_ROWS - 1, 2 * NA_COLS - 1), 0.5)
    inp['l1_w_out'] = nrm((L1_OUT, D), L1_OUT ** -0.5)
    inp['l1_exp_gate'] = nrm((N_EXPERTS, D, EXPERT_FF), D ** -0.5)
    inp['l1_exp_up'] = nrm((N_EXPERTS, D, EXPERT_FF), D ** -0.5)
    inp['l1_exp_down'] = nrm((N_EXPERTS, EXPERT_FF, D), EXPERT_FF ** -0.5)
    return inp


def reference(x, c, ctx, c_ctx, router_w, router_b, final_norm,
              l0_mod_w, l0_mod_b, l0_norm_mix, l0_norm_ffn, l0_w_in, l0_q_norm, l0_w_uq, l0_kv_norm, l0_w_ukv,
              l0_conv_w, l0_conv_b, l0_gate_a_w, l0_gate_a_b, l0_gate_x_w, l0_gate_x_b, l0_lru_lambda, l0_w_out,
              l0_exp_gate, l0_exp_up, l0_exp_down,
              l1_mod_w, l1_mod_b, l1_norm_mix, l1_norm_ffn, l1_w_in, l1_sink, l1_rpb, l1_w_out,
              l1_exp_gate, l1_exp_up, l1_exp_down):
    layer_params = (
        dict(mod_w=l0_mod_w, mod_b=l0_mod_b, norm_mix=l0_norm_mix, norm_ffn=l0_norm_ffn, w_in=l0_w_in,
             q_norm=l0_q_norm, w_uq=l0_w_uq, kv_norm=l0_kv_norm, w_ukv=l0_w_ukv, conv_w=l0_conv_w,
             conv_b=l0_conv_b, gate_a_w=l0_gate_a_w, gate_a_b=l0_gate_a_b, gate_x_w=l0_gate_x_w,
             gate_x_b=l0_gate_x_b, lru_lambda=l0_lru_lambda, w_out=l0_w_out,
             exp_gate=l0_exp_gate, exp_up=l0_exp_up, exp_down=l0_exp_down),
        dict(mod_w=l1_mod_w, mod_b=l1_mod_b, norm_mix=l1_norm_mix, norm_ffn=l1_norm_ffn, w_in=l1_w_in,
             sink=l1_sink, rpb=l1_rpb, w_out=l1_w_out,
             exp_gate=l1_exp_gate, exp_up=l1_exp_up, exp_down=l1_exp_down),
    )
    mixers = (mla_rglru_mixer, window_na_mixer)
    xl, xc = x, ctx
    L = ctx.shape[1]
    for layer in range(DEPTH):
        p = layer_params[layer]
        mixer = mixers[layer % 2]
        ctx_out = layer < DEPTH - 1
        ml = (jax.nn.silu(c) @ p['mod_w'] + p['mod_b'])[:, None, :]
        mc = (jax.nn.silu(c_ctx) @ p['mod_w'] + p['mod_b'])[None, None, :]
        sh1, sc1, g1, sh2, sc2, g2 = jnp.split(ml, N_MOD, axis=-1)
        csh1, csc1, cg1, csh2, csc2, cg2 = jnp.split(mc, N_MOD, axis=-1)
        hl = modulate(rmsnorm(xl, p['norm_mix']), sh1, sc1)
        hc = modulate(rmsnorm(xc, p['norm_mix']), csh1, csc1)
        yl, yc = mixer(hl, hc, p, ctx_out)
        xl = xl + g1 * (yl @ p['w_out'])
        hl = modulate(rmsnorm(xl, p['norm_ffn']), sh2, sc2)
        if ctx_out:
            xc = xc + cg1 * (yc @ p['w_out'])
            hc = modulate(rmsnorm(xc, p['norm_ffn']), csh2, csc2)
            f = grouped_moe(jnp.concatenate([hc, hl], axis=1), router_w, router_b,
                            p['exp_gate'], p['exp_up'], p['exp_down'])
            xc = xc + cg2 * f[:, :L]
            xl = xl + g2 * f[:, L:]
        else:
            xl = xl + g2 * grouped_moe(hl, router_w, router_b, p['exp_gate'], p['exp_up'], p['exp_down'])
    return rmsnorm(xl, final_norm)
```

```python
import functools

import jax
import jax.numpy as jnp
from jax import lax
from jax.experimental import pallas as pl
from jax.experimental.pallas import tpu as pltpu

F32 = jnp.float32
BF16 = jnp.bfloat16

D_MODEL = 2048
BATCH = 4
SEQ = 2048
GRID_W = 64
CTX_LEN = 256
EPS = 1e-6
NEG_INF = -1e30
ROPE_THETA = 10000.0
N_MOD = 6

MLA_HEADS = 8
MLA_Q_RANK = 512
MLA_KV_RANK = 256
MLA_NOPE = 128
MLA_ROPE = 64
MLA_V = 128

LRU_WIDTH = 1024
LRU_BLOCKS = 8
LRU_C = 8.0

HEAD_DIM = 128
WIN_HEADS = 8
WIN_KV_HEADS = 2
WINDOW = 128
NA_HEADS = 8
NA_ROWS = 8
NA_COLS = 16

N_EXPERTS = 32
N_GROUPS = 8
EXPERTS_PER_GROUP = 4
EXPERT_FF = 512

LANES = 128
SUBLANES = 8
VMEM_LIMIT = 56 * 1024 * 1024

NB = CTX_LEN + SEQ
TOK = BATCH * NB
TM = 256
TPB = NB // TM
LAT_TPB = SEQ // TM
TMX = 256
MLA_QK = 2 * LANES


def _cparams(sem):
    return pltpu.CompilerParams(dimension_semantics=sem, vmem_limit_bytes=VMEM_LIMIT)


def _resident(shape):
    nd = len(shape)
    return pl.BlockSpec(shape, lambda *_: (0,) * nd, pipeline_mode=pl.Buffered(1))


def _rms(x, g):
    return x * lax.rsqrt(jnp.mean(x * x, axis=-1, keepdims=True) + EPS) * g


def _dot(a, b):
    return jnp.dot(a, b, preferred_element_type=F32)


def _dot_nt(a, b):
    return lax.dot_general(a, b, (((1,), (1,)), ((), ())), preferred_element_type=F32)


def _swap_blocks(x, blk):
    lane = lax.broadcasted_iota(jnp.int32, x.shape, 1)
    nxt = pltpu.roll(x, LANES - blk, axis=1)
    prv = pltpu.roll(x, blk, axis=1)
    return jnp.where((lane % (2 * blk)) < blk, nxt, prv)


def _rope(x, cos, sin, blk):
    return x * cos + _swap_blocks(x, blk) * sin


def _comb_tile(i):
    b = i // TPB
    return i, jnp.where(i % TPB == 0, BATCH, b)


def _lat_tile(i):
    b = i // LAT_TPB
    return b * TPB + 1 + i % LAT_TPB, b


def _mod_spec(tile_fn, k):
    return pl.BlockSpec((1, 1, D_MODEL), lambda i, *_: (tile_fn(i)[1] * N_MOD + k, 0, 0))


def _mod_kernel(c_ref, w_ref, b_ref, o_ref):
    c = c_ref[...]
    a = (c * jax.nn.sigmoid(c)).astype(BF16)
    o_ref[...] = _dot(a, w_ref[...].astype(BF16)) + b_ref[...]


def _modulation(cs, w, b):
    n = N_MOD * D_MODEL
    tn = 1024
    out = pl.pallas_call(
        _mod_kernel,
        grid=(n // tn,),
        in_specs=[pl.BlockSpec((SUBLANES, D_MODEL), lambda j: (0, 0)),
                  pl.BlockSpec((D_MODEL, tn), lambda j: (0, j)),
                  pl.BlockSpec((1, tn), lambda j: (0, j))],
        out_specs=pl.BlockSpec((SUBLANES, tn), lambda j: (0, j)),
        out_shape=jax.ShapeDtypeStruct((SUBLANES, n), F32),
        compiler_params=_cparams(("arbitrary",)),
        name="modulation",
    )(cs, w, b.reshape(1, n))
    return out.reshape(SUBLANES * N_MOD, 1, D_MODEL)


L0_CQKV = MLA_Q_RANK + MLA_KV_RANK + LANES
L0_IN_PAD = L0_CQKV + 2 * LRU_WIDTH


def _in0_kernel(x_ref, g_ref, sh_ref, sc_ref, w_ref, cqkv_ref, xr_ref, gr_ref):
    h = _rms(x_ref[...], g_ref[...]) * (1.0 + sc_ref[0]) + sh_ref[0]
    y = _dot(h.astype(BF16), w_ref[...])
    cqkv_ref[...] = y[:, :L0_CQKV]
    xr_ref[...] = y[:, L0_CQKV:L0_CQKV + LRU_WIDTH]
    gr_ref[...] = y[:, L0_CQKV + LRU_WIDTH:]


def _in0(x, g, mod, w):
    row = lambda i: (i, 0)
    return pl.pallas_call(
        _in0_kernel,
        grid=(TOK // TM,),
        in_specs=[pl.BlockSpec((TM, D_MODEL), row), _resident((1, D_MODEL)),
                  _mod_spec(_comb_tile, 0), _mod_spec(_comb_tile, 1), _resident((D_MODEL, L0_IN_PAD))],
        out_specs=[pl.BlockSpec((TM, L0_CQKV), row), pl.BlockSpec((TM, LRU_WIDTH), row),
                   pl.BlockSpec((TM, LRU_WIDTH), row)],
        out_shape=[jax.ShapeDtypeStruct((TOK, L0_CQKV), F32), jax.ShapeDtypeStruct((TOK, LRU_WIDTH), F32),
                   jax.ShapeDtypeStruct((TOK, LRU_WIDTH), F32)],
        compiler_params=_cparams(("parallel",)),
        name="l0_in_proj",
    )(x, g, mod, mod, w)


def _mla_proj_kernel(c_ref, qn_ref, kvn_ref, wq_ref, wk_ref, wv_ref, cos_ref, sin_ref, q_ref, k_ref, v_ref):
    c = c_ref[...]
    cos = cos_ref[...]
    sin = sin_ref[...]
    nq = _rms(c[:, :MLA_Q_RANK], qn_ref[...]).astype(BF16)
    q = _dot(nq, wq_ref[...]) * ((MLA_NOPE + MLA_ROPE) ** -0.5)
    nkv = _rms(c[:, MLA_Q_RANK:MLA_Q_RANK + MLA_KV_RANK], kvn_ref[...]).astype(BF16)
    kn = _dot(nkv, wk_ref[...])
    v_ref[...] = _dot(nkv, wv_ref[...]).astype(BF16)
    kr = _rope(c[:, MLA_Q_RANK + MLA_KV_RANK:], cos, sin, MLA_ROPE // 4).astype(BF16)
    for h in range(MLA_HEADS):
        lo = h * MLA_QK
        q_ref[:, lo:lo + LANES] = q[:, lo:lo + LANES].astype(BF16)
        q_ref[:, lo + LANES:lo + MLA_QK] = _rope(q[:, lo + LANES:lo + MLA_QK], cos, sin, MLA_ROPE // 4).astype(BF16)
        k_ref[:, lo:lo + LANES] = kn[:, h * LANES:(h + 1) * LANES].astype(BF16)
        k_ref[:, lo + LANES:lo + MLA_QK] = kr


def _mla_proj(cqkv, qn, kvn, wq, wk, wv, cos, sin):
    row = lambda i: (i, 0)
    pos = lambda i: (i % TPB, 0)
    hq = MLA_HEADS * MLA_QK
    hv = MLA_HEADS * MLA_V
    return pl.pallas_call(
        _mla_proj_kernel,
        grid=(TOK // TM,),
        in_specs=[pl.BlockSpec((TM, L0_CQKV), row), _resident((1, MLA_Q_RANK)), _resident((1, MLA_KV_RANK)),
                  _resident((MLA_Q_RANK, hq)), _resident((MLA_KV_RANK, hv)), _resident((MLA_KV_RANK, hv)),
                  pl.BlockSpec((TM, LANES), pos), pl.BlockSpec((TM, LANES), pos)],
        out_specs=[pl.BlockSpec((TM, hq), row), pl.BlockSpec((TM, hq), row), pl.BlockSpec((TM, hv), row)],
        out_shape=[jax.ShapeDtypeStruct((TOK, hq), BF16), jax.ShapeDtypeStruct((TOK, hq), BF16),
                   jax.ShapeDtypeStruct((TOK, hv), BF16)],
        compiler_params=_cparams(("parallel",)),
        name="mla_proj",
    )(cqkv, qn, kvn, wq, wk, wv, cos, sin)


MLA_TQ = 256


def _softmax_pv(s, v):
    m = jnp.max(s, axis=-1, keepdims=True)
    p = jnp.exp(s - m)
    l = jnp.sum(p, axis=-1, keepdims=True)
    return _dot(p.astype(BF16), v) / l


def _mla_attn_kernel(q_ref, k_ref, v_ref, o_ref):
    s = _dot_nt(q_ref[0:CTX_LEN, :], k_ref[0:CTX_LEN, :])
    o_ref[0:CTX_LEN, :] = _softmax_pv(s, v_ref[0:CTX_LEN, :]).astype(o_ref.dtype)

    def body(i, carry):
        r0 = pl.multiple_of(CTX_LEN + i * MLA_TQ, MLA_TQ)
        s = _dot_nt(q_ref[pl.ds(r0, MLA_TQ), :], k_ref[...])
        o_ref[pl.ds(r0, MLA_TQ), :] = _softmax_pv(s, v_ref[...]).astype(o_ref.dtype)
        return carry

    lax.fori_loop(0, SEQ // MLA_TQ, body, 0)


def _mla_attn(q, k, v):
    blk = lambda b, h: (b, h)
    return pl.pallas_call(
        _mla_attn_kernel,
        grid=(BATCH, MLA_HEADS),
        in_specs=[pl.BlockSpec((NB, MLA_QK), blk), pl.BlockSpec((NB, MLA_QK), blk), pl.BlockSpec((NB, MLA_V), blk)],
        out_specs=pl.BlockSpec((NB, MLA_V), blk),
        out_shape=jax.ShapeDtypeStruct((TOK, MLA_HEADS * MLA_V), BF16),
        compiler_params=_cparams(("parallel", "parallel")),
        name="mla_attn",
    )(q, k, v)


LRU_BW = LRU_WIDTH // LRU_BLOCKS
CTX_GROUPS = CTX_LEN // SUBLANES
LAT_GROUPS = SEQ // SUBLANES


def _scan_group(a, b, reverse):
    row = lax.broadcasted_iota(jnp.int32, a.shape, 0)
    for d in (1, 2, 4):
        shift = SUBLANES - d if reverse else d
        a_s = pltpu.roll(a, shift, axis=0)
        b_s = pltpu.roll(b, shift, axis=0)
        m = (row < SUBLANES - d) if reverse else (row >= d)
        b = jnp.where(m, a * b_s + b, b)
        a = jnp.where(m, a * a_s, a)
    return a, b


def _rglru_kernel(xr_ref, gr_ref, cw_ref, cb_ref, wa_ref, ba_ref, wx_ref, bx_ref, lam_ref, y_ref,
                  af_ref, bf_ref, ab_ref, bb_ref, hf_ref, hb_ref):
    x = xr_ref[...]
    row = lax.broadcasted_iota(jnp.int32, x.shape, 0)
    xm2 = jnp.where((row >= 2) & (row != CTX_LEN) & (row != CTX_LEN + 1), pltpu.roll(x, 2, axis=0), 0.0)
    xm1 = jnp.where((row >= 1) & (row != CTX_LEN), pltpu.roll(x, 1, axis=0), 0.0)
    xp1 = jnp.where((row != CTX_LEN - 1) & (row != NB - 1), pltpu.roll(x, NB - 1, axis=0), 0.0)
    u = cb_ref[...] + xm2 * cw_ref[0:1, :] + xm1 * cw_ref[1:2, :] + x * cw_ref[2:3, :] + xp1 * cw_ref[3:4, :]
    ub = u.astype(BF16)
    for d, (a_ref, b_ref) in enumerate(((af_ref, bf_ref), (ab_ref, bb_ref))):
        r = jax.nn.sigmoid(_dot(ub, wa_ref[d, 0].astype(BF16)) + ba_ref[d:d + 1, :])
        ig = jax.nn.sigmoid(_dot(ub, wx_ref[d, 0].astype(BF16)) + bx_ref[d:d + 1, :])
        z = -lam_ref[d:d + 1, :]
        softplus = jnp.maximum(z, 0.0) + jnp.log(1.0 + jnp.exp(-jnp.abs(z)))
        log_a = -LRU_C * r * softplus
        a = jnp.exp(log_a)
        a_ref[...] = a
        b_ref[...] = jnp.sqrt(1.0 - jnp.exp(2.0 * log_a)) * (ig * u)

    def step(gf, gb, hf, hb):
        rf = pl.multiple_of(gf * SUBLANES, SUBLANES)
        a, b = _scan_group(af_ref[pl.ds(rf, SUBLANES), :], bf_ref[pl.ds(rf, SUBLANES), :], False)
        h = a * hf + b
        hf_ref[pl.ds(rf, SUBLANES), :] = h
        hf = jnp.broadcast_to(h[SUBLANES - 1:SUBLANES, :], h.shape)
        rb = pl.multiple_of(gb * SUBLANES, SUBLANES)
        a, b = _scan_group(ab_ref[pl.ds(rb, SUBLANES), :], bb_ref[pl.ds(rb, SUBLANES), :], True)
        h = a * hb + b
        hb_ref[pl.ds(rb, SUBLANES), :] = h
        hb = jnp.broadcast_to(h[0:1, :], h.shape)
        return hf, hb

    zero = jnp.zeros((SUBLANES, LRU_BW), F32)
    carry = lax.fori_loop(0, CTX_GROUPS, lambda i, c: step(i, CTX_GROUPS - 1 - i, *c), (zero, zero))
    lax.fori_loop(0, LAT_GROUPS, lambda i, c: step(CTX_GROUPS + i, CTX_GROUPS + LAT_GROUPS - 1 - i, *c), carry)
    y_ref[...] = ((hf_ref[...] + hb_ref[...]) * jax.nn.gelu(gr_ref[...])).astype(y_ref.dtype)


def _rglru(xr, gr, conv_w, conv_b, wa, ba, wx, bx, lam):
    blk = lambda b, n: (b, n)
    col = lambda b, n: (0, n)
    gate = lambda b, n: (0, n, 0, 0)
    seg = pltpu.VMEM((NB, LRU_BW), F32)
    return pl.pallas_call(
        _rglru_kernel,
        grid=(BATCH, LRU_BLOCKS),
        in_specs=[pl.BlockSpec((NB, LRU_BW), blk), pl.BlockSpec((NB, LRU_BW), blk),
                  pl.BlockSpec((4, LRU_BW), col), pl.BlockSpec((1, LRU_BW), col),
                  pl.BlockSpec((2, 1, LRU_BW, LRU_BW), gate), pl.BlockSpec((2, LRU_BW), col),
                  pl.BlockSpec((2, 1, LRU_BW, LRU_BW), gate), pl.BlockSpec((2, LRU_BW), col),
                  pl.BlockSpec((2, LRU_BW), col)],
        out_specs=pl.BlockSpec((NB, LRU_BW), blk),
        out_shape=jax.ShapeDtypeStruct((TOK, LRU_WIDTH), BF16),
        scratch_shapes=[seg, seg, seg, seg, seg, seg],
        compiler_params=_cparams(("parallel", "parallel")),
        name="rglru",
    )(xr, gr, conv_w, conv_b.reshape(1, LRU_WIDTH), wa, ba, wx, bx, lam)


L1_Q = WIN_HEADS * HEAD_DIM
L1_KV = WIN_KV_HEADS * HEAD_DIM
L1_NA = NA_HEADS * HEAD_DIM
L1_IN = L1_Q + 2 * L1_KV + 3 * L1_NA


def _in1_kernel(x_ref, g_ref, sh_ref, sc_ref, w_ref, cos_ref, sin_ref,
                qw_ref, kw_ref, vw_ref, qn_ref, kn_ref, vn_ref):
    h = _rms(x_ref[...], g_ref[...]) * (1.0 + sc_ref[0]) + sh_ref[0]
    y = _dot(h.astype(BF16), w_ref[...])
    cos = cos_ref[...]
    sin = sin_ref[...]
    scale = HEAD_DIM ** -0.5
    for hd in range(WIN_HEADS):
        lo = hd * HEAD_DIM
        qw_ref[:, lo:lo + HEAD_DIM] = _rope(y[:, lo:lo + HEAD_DIM] * scale, cos, sin, HEAD_DIM // 4).astype(BF16)
    for hd in range(WIN_KV_HEADS):
        lo = hd * HEAD_DIM
        kw_ref[:, lo:lo + HEAD_DIM] = _rope(y[:, L1_Q + lo:L1_Q + lo + HEAD_DIM], cos, sin, HEAD_DIM // 4).astype(BF16)
    o = L1_Q + L1_KV
    vw_ref[...] = y[:, o:o + L1_KV].astype(BF16)
    o += L1_KV
    qn_ref[...] = (y[:, o:o + L1_NA] * scale).astype(BF16)
    kn_ref[...] = y[:, o + L1_NA:o + 2 * L1_NA].astype(BF16)
    vn_ref[...] = y[:, o + 2 * L1_NA:].astype(BF16)


def _in1(x, g, mod, w, cos, sin):
    row = lambda i: (i, 0)
    pos = lambda i: (i % TPB, 0)
    widths = (L1_Q, L1_KV, L1_KV, L1_NA, L1_NA, L1_NA)
    return pl.pallas_call(
        _in1_kernel,
        grid=(TOK // TM,),
        in_specs=[pl.BlockSpec((TM, D_MODEL), row), _resident((1, D_MODEL)),
                  _mod_spec(_comb_tile, 0), _mod_spec(_comb_tile, 1), _resident((D_MODEL, L1_IN)),
                  pl.BlockSpec((TM, LANES), pos), pl.BlockSpec((TM, LANES), pos)],
        out_specs=[pl.BlockSpec((TM, n), row) for n in widths],
        out_shape=[jax.ShapeDtypeStruct((TOK, n), BF16) for n in widths],
        compiler_params=_cparams(("parallel",)),
        name="l1_in_proj",
    )(x, g, mod, mod, w, cos, sin)


WIN_TQ = 128
WIN_SPAN = WIN_TQ + 2 * WINDOW
WIN_G = WIN_HEADS // WIN_KV_HEADS


def _win_kernel(sink_ref, q_ref, k_ref, v_ref, o_ref):
    hk = pl.program_id(1)
    o_ref[0:CTX_LEN, :] = jnp.zeros((CTX_LEN, WIN_G * HEAD_DIM), o_ref.dtype)
    rows = WIN_G * WIN_TQ
    head = lax.broadcasted_iota(jnp.int32, (rows, 1), 0) // WIN_TQ
    sink = jnp.zeros((rows, 1), F32)
    for g in range(WIN_G):
        sink = jnp.where(head == g, sink_ref[hk * WIN_G + g], sink)
    qoff = lax.broadcasted_iota(jnp.int32, (rows, WIN_SPAN), 0) % WIN_TQ
    koff = lax.broadcasted_iota(jnp.int32, (rows, WIN_SPAN), 1)

    def body(n, carry):
        r0 = pl.multiple_of(CTX_LEN + n * WIN_TQ, WIN_TQ)
        start = jnp.clip((n - 1) * WIN_TQ, 0, SEQ - WIN_SPAN)
        ks = pl.multiple_of(CTX_LEN + start, WIN_TQ)
        q4 = q_ref[pl.ds(r0, WIN_TQ), :]
        q = jnp.concatenate([q4[:, g * HEAD_DIM:(g + 1) * HEAD_DIM] for g in range(WIN_G)], axis=0)
        s_c = _dot_nt(q, k_ref[0:CTX_LEN, :])
        s_w = _dot_nt(q, k_ref[pl.ds(ks, WIN_SPAN), :])
        valid = jnp.abs(n * WIN_TQ + qoff - (start + koff)) <= WINDOW
        s_w = jnp.where(valid, s_w, NEG_INF)
        m = jnp.maximum(jnp.maximum(jnp.max(s_c, axis=-1, keepdims=True), jnp.max(s_w, axis=-1, keepdims=True)), sink)
        p_c = jnp.exp(s_c - m)
        p_w = jnp.exp(s_w - m)
        l = jnp.sum(p_c, axis=-1, keepdims=True) + jnp.sum(p_w, axis=-1, keepdims=True) + jnp.exp(sink - m)
        o = (_dot(p_w.astype(BF16), v_ref[pl.ds(ks, WIN_SPAN), :]) + _dot(p_c.astype(BF16), v_ref[0:CTX_LEN, :])) / l
        for g in range(WIN_G):
            o_ref[pl.ds(r0, WIN_TQ), g * HEAD_DIM:(g + 1) * HEAD_DIM] = o[g * WIN_TQ:(g + 1) * WIN_TQ].astype(o_ref.dtype)
        return carry

    lax.fori_loop(0, SEQ // WIN_TQ, body, 0)


def _win_attn(sink, q, k, v):
    blk = lambda b, h, *_: (b, h)
    return pl.pallas_call(
        _win_kernel,
        grid_spec=pltpu.PrefetchScalarGridSpec(
            num_scalar_prefetch=1,
            grid=(BATCH, WIN_KV_HEADS),
            in_specs=[pl.BlockSpec((NB, WIN_G * HEAD_DIM), blk), pl.BlockSpec((NB, HEAD_DIM), blk),
                      pl.BlockSpec((NB, HEAD_DIM), blk)],
            out_specs=pl.BlockSpec((NB, WIN_G * HEAD_DIM), blk)),
        out_shape=jax.ShapeDtypeStruct((TOK, L1_Q), BF16),
        compiler_params=_cparams(("parallel", "parallel")),
        name="window_attn",
    )(sink, q, k, v)


NA_GRID_ROWS = SEQ // GRID_W
NA_BAND = NA_ROWS * GRID_W


def _na_kernel(q_ref, k_ref, v_ref, bias_ref, o_ref):
    o_ref[0:CTX_LEN, :] = jnp.zeros((CTX_LEN, HEAD_DIM), o_ref.dtype)

    def body(r, carry):
        r0 = jnp.clip(r - NA_ROWS // 2, 0, NA_GRID_ROWS - NA_ROWS)
        qs = pl.multiple_of(CTX_LEN + r * GRID_W, GRID_W)
        ks = pl.multiple_of(CTX_LEN + r0 * GRID_W, GRID_W)
        q = q_ref[pl.ds(qs, GRID_W), :]
        s_w = _dot_nt(q, k_ref[pl.ds(ks, NA_BAND), :]) + bias_ref[0, r0 - r + NA_ROWS - 1]
        s_c = _dot_nt(q, k_ref[0:CTX_LEN, :])
        m = jnp.maximum(jnp.max(s_c, axis=-1, keepdims=True), jnp.max(s_w, axis=-1, keepdims=True))
        p_c = jnp.exp(s_c - m)
        p_w = jnp.exp(s_w - m)
        l = jnp.sum(p_c, axis=-1, keepdims=True) + jnp.sum(p_w, axis=-1, keepdims=True)
        o = (_dot(p_w.astype(BF16), v_ref[pl.ds(ks, NA_BAND), :]) + _dot(p_c.astype(BF16), v_ref[0:CTX_LEN, :])) / l
        o_ref[pl.ds(qs, GRID_W), :] = o.astype(o_ref.dtype)
        return carry

    lax.fori_loop(0, NA_GRID_ROWS, body, 0)


def _na_attn(q, k, v, bias):
    blk = lambda b, h: (b, h)
    return pl.pallas_call(
        _na_kernel,
        grid=(BATCH, NA_HEADS),
        in_specs=[pl.BlockSpec((NB, HEAD_DIM), blk), pl.BlockSpec((NB, HEAD_DIM), blk),
                  pl.BlockSpec((NB, HEAD_DIM), blk),
                  pl.BlockSpec((1, NA_ROWS, GRID_W, NA_BAND), lambda b, h: (h, 0, 0, 0))],
        out_specs=pl.BlockSpec((NB, HEAD_DIM), blk),
        out_shape=jax.ShapeDtypeStruct((TOK, L1_NA), BF16),
        compiler_params=_cparams(("parallel", "parallel")),
        name="na_attn",
    )(q, k, v, bias)


def _na_bias_table(rpb):
    cols = jnp.arange(GRID_W)
    c0 = jnp.clip(cols - NA_COLS // 2, 0, GRID_W - NA_COLS)
    kc = cols[None, :]
    valid = (kc >= c0[:, None]) & (kc < c0[:, None] + NA_COLS)
    coff = jnp.clip(kc - cols[:, None] + NA_COLS - 1, 0, 2 * NA_COLS - 2)
    tbl = jnp.where(valid[None, None], rpb[:, :, coff].astype(F32), NEG_INF)
    tbl = jnp.stack([tbl[:, d:d + NA_ROWS] for d in range(NA_ROWS)], axis=1)
    return jnp.transpose(tbl, (0, 1, 3, 2, 4)).reshape(NA_HEADS, NA_ROWS, GRID_W, NA_BAND)


def _router(h, rwh_ref, rwl_ref, rb_ref, tri_ref, carry_ref, info_ref, counts_ref):
    tm = h.shape[0]
    hh = h.astype(BF16)
    hl = (h - hh.astype(F32)).astype(BF16)
    logits = _dot_nt(rwh_ref[...], hh) + (_dot_nt(rwh_ref[...], hl) + _dot_nt(rwl_ref[...], hh))
    scores = jax.nn.sigmoid(logits)
    biased = scores + rb_ref[...]
    nj = EXPERTS_PER_GROUP
    s = [biased[j * N_GROUPS:(j + 1) * N_GROUPS] for j in range(nj)]
    u = [scores[j * N_GROUPS:(j + 1) * N_GROUPS] for j in range(nj)]
    gs = None
    for a in range(nj):
        for b in range(a + 1, nj):
            pair = s[a] + s[b]
            gs = pair if gs is None else jnp.maximum(gs, pair)
    giota = lax.broadcasted_iota(jnp.int32, (N_GROUPS, tm), 0).astype(F32)
    gmax = jnp.max(gs, axis=0, keepdims=True)
    gidx = jnp.min(jnp.where(gs == gmax, giota, float(N_GROUPS)), axis=0, keepdims=True)
    gm = giota == gidx
    v = [jnp.sum(jnp.where(gm, s[j], 0.0), axis=0, keepdims=True) for j in range(nj)]
    w = [jnp.sum(jnp.where(gm, u[j], 0.0), axis=0, keepdims=True) for j in range(nj)]
    sel = []
    for j in range(nj):
        beaten = jnp.zeros((1, tm), F32)
        for i in range(nj):
            if i != j:
                ahead = (v[i] >= v[j]) if i < j else (v[i] > v[j])
                beaten = beaten + jnp.where(ahead, 1.0, 0.0)
        sel.append(beaten < 2.0)
    wsum = sum(jnp.where(sel[j], w[j], 0.0) for j in range(nj))
    first = functools.reduce(jnp.minimum, [jnp.where(sel[j], float(j), float(nj)) for j in range(nj)])
    last = functools.reduce(jnp.maximum, [jnp.where(sel[j], float(j), -1.0) for j in range(nj)])
    gmf = jnp.where(gm, 1.0, 0.0)
    cnt = jnp.concatenate([jnp.where(sel[j], gmf, 0.0) for j in range(nj)], axis=0)
    pos = _dot(cnt.astype(BF16), tri_ref[...]) + carry_ref[:, 0:1]
    rank = [jnp.sum(cnt[j * N_GROUPS:(j + 1) * N_GROUPS] * pos[j * N_GROUPS:(j + 1) * N_GROUPS], axis=0, keepdims=True)
            for j in range(nj)]
    pick = lambda which, vals: sum(jnp.where(which == float(j), vals[j], 0.0) for j in range(nj))
    gate = [w[j] / wsum for j in range(nj)]
    zero = jnp.zeros((1, tm), F32)
    info_ref[...] = jnp.concatenate(
        [gidx * nj + first, gidx * nj + last, pick(first, rank), pick(last, rank), pick(first, gate), pick(last, gate),
         zero, zero], axis=0)
    new = carry_ref[...] + jnp.sum(cnt, axis=1, keepdims=True)
    carry_ref[...] = new
    counts_ref[...] = new


def _out_kernel(ya_ref, yb_ref, x_ref, wa_ref, wb_ref, g1_ref, n_ref, sh_ref, sc_ref,
                rwh_ref, rwl_ref, rb_ref, tri_ref, xo_ref, h_ref, info_ref, counts_ref, carry_ref):
    @pl.when(pl.program_id(0) == 0)
    def _():
        carry_ref[...] = jnp.zeros_like(carry_ref)

    y = _dot(ya_ref[...], wa_ref[...]) + _dot(yb_ref[...], wb_ref[...])
    x = x_ref[...] + g1_ref[0] * y
    xo_ref[...] = x
    h = _rms(x, n_ref[...]) * (1.0 + sc_ref[0]) + sh_ref[0]
    h_ref[...] = h
    _router(h, rwh_ref, rwl_ref, rb_ref, tri_ref, carry_ref, info_ref, counts_ref)


def _out_proj(ya, yb, x, w_out, mod, norm, rwh, rwl, rb, tri, tile_fn, n_tiles):
    half = w_out.shape[0] // 2
    src = lambda i: (tile_fn(i)[0], 0)
    dst = lambda i: (i, 0)
    n_tok = n_tiles * TM
    return pl.pallas_call(
        _out_kernel,
        grid=(n_tiles,),
        in_specs=[pl.BlockSpec((TM, half), src), pl.BlockSpec((TM, half), src), pl.BlockSpec((TM, D_MODEL), src),
                  _resident((half, D_MODEL)), _resident((half, D_MODEL)),
                  _mod_spec(tile_fn, 2), _resident((1, D_MODEL)), _mod_spec(tile_fn, 3), _mod_spec(tile_fn, 4),
                  _resident((N_EXPERTS, D_MODEL)), _resident((N_EXPERTS, D_MODEL)), _resident((N_EXPERTS, 1)),
                  _resident((TM, TM))],
        out_specs=[pl.BlockSpec((TM, D_MODEL), dst), pl.BlockSpec((TM, D_MODEL), dst),
                   pl.BlockSpec((SUBLANES, TM), lambda i: (0, i)),
                   pl.BlockSpec((N_EXPERTS, LANES), lambda i: (0, 0))],
        out_shape=[jax.ShapeDtypeStruct((n_tok, D_MODEL), F32), jax.ShapeDtypeStruct((n_tok, D_MODEL), F32),
                   jax.ShapeDtypeStruct((SUBLANES, n_tok), F32), jax.ShapeDtypeStruct((N_EXPERTS, LANES), F32)],
        scratch_shapes=[pltpu.VMEM((N_EXPERTS, LANES), F32)],
        compiler_params=_cparams(("arbitrary",)),
        name="out_proj_router",
    )(ya, yb, x, w_out[:half], w_out[half:], mod, norm, mod, mod, rwh, rwl, rb, tri)


def _moe_kernel(te_ref, nv_ref, src_ref, h_hbm, wg_ref, wu_ref, wd_ref, y_ref, x_buf, wg_s, wu_s, wd_s, sem):
    g = pl.program_id(0)
    first = jnp.logical_or(g == 0, te_ref[g] != te_ref[jnp.maximum(g - 1, 0)])

    @pl.when(g < nv_ref[0])
    def _():
        def row_copy(r, src):
            return pltpu.make_async_copy(h_hbm.at[pl.ds(src, 1)], x_buf.at[pl.ds(r, 1)], sem)

        def start(r, c):
            row_copy(r, src_ref[g * TMX + r]).start()
            return c

        lax.fori_loop(0, TMX, start, 0)

        @pl.when(first)
        def _():
            wg_s[...] = wg_ref[0].astype(BF16)
            wu_s[...] = wu_ref[0].astype(BF16)
            wd_s[...] = wd_ref[0].astype(BF16)

        def wait(r, c):
            row_copy(r, 0).wait()
            return c

        lax.fori_loop(0, TMX, wait, 0)
        x = x_buf[...].astype(BF16)
        hg = _dot(x, wg_s[...])
        he = (hg * jax.nn.sigmoid(hg)) * _dot(x, wu_s[...])
        y_ref[...] = _dot(he.astype(BF16), wd_s[...])

    @pl.when(g >= nv_ref[0])
    def _():
        y_ref[...] = jnp.zeros_like(y_ref)


def _moe(tile_expert, n_valid, src_row, h, w_gate, w_up, w_down):
    n_tiles = tile_expert.shape[0]
    wsel = lambda g, te, nv, src: (te[g], 0, 0)
    return pl.pallas_call(
        _moe_kernel,
        grid_spec=pltpu.PrefetchScalarGridSpec(
            num_scalar_prefetch=3,
            grid=(n_tiles,),
            in_specs=[pl.BlockSpec(memory_space=pl.ANY),
                      pl.BlockSpec((1, D_MODEL, EXPERT_FF), wsel), pl.BlockSpec((1, D_MODEL, EXPERT_FF), wsel),
                      pl.BlockSpec((1, EXPERT_FF, D_MODEL), wsel)],
            out_specs=pl.BlockSpec((TMX, D_MODEL), lambda g, te, nv, src: (g, 0)),
            scratch_shapes=[pltpu.VMEM((TMX, D_MODEL), F32), pltpu.VMEM((D_MODEL, EXPERT_FF), BF16),
                            pltpu.VMEM((D_MODEL, EXPERT_FF), BF16), pltpu.VMEM((EXPERT_FF, D_MODEL), BF16),
                            pltpu.SemaphoreType.DMA(())]),
        out_shape=jax.ShapeDtypeStruct((n_tiles * TMX, D_MODEL), F32),
        compiler_params=_cparams(("arbitrary",)),
        name="moe_experts",
    )(tile_expert, n_valid, src_row, h, w_gate, w_up, w_down)


def _dispatch_plan(info, counts, n_tok):
    n_tiles = 2 * n_tok // TMX + N_EXPERTS
    cnt = counts[:, 0].astype(jnp.int32).reshape(EXPERTS_PER_GROUP, N_GROUPS).T.reshape(N_EXPERTS)
    ntile = (cnt + TMX - 1) // TMX
    tile_end = jnp.cumsum(ntile)
    n_valid = tile_end[-1]
    row0 = (tile_end - ntile) * TMX
    e_sel = info[0:2].astype(jnp.int32)
    dest = row0[e_sel] + info[2:4].astype(jnp.int32)
    tok = jnp.broadcast_to(jnp.arange(n_tok, dtype=jnp.int32), (2, n_tok))
    src_row = jnp.zeros((n_tiles * TMX,), jnp.int32).at[dest.reshape(-1)].set(tok.reshape(-1))
    tile_id = jnp.minimum(jnp.arange(n_tiles, dtype=jnp.int32), n_valid - 1)
    tile_expert = jnp.searchsorted(tile_end, tile_id, side="right").astype(jnp.int32)
    return tile_expert, n_valid.reshape(1).astype(jnp.int32), src_row, dest.reshape(-1)


def _combine_kernel(dest_ref, y_hbm, x_ref, w_ref, g2_ref, n_ref, o_ref, y_buf, sem, *, n_tok, final):
    i = pl.program_id(0)

    def row_copy(k, r, d):
        return pltpu.make_async_copy(y_hbm.at[pl.ds(d, 1)], y_buf.at[k, pl.ds(r, 1)], sem)

    def start(r, c):
        for k in range(2):
            row_copy(k, r, dest_ref[k * n_tok + i * TM + r]).start()
        return c

    lax.fori_loop(0, TM, start, 0)

    def wait(r, c):
        for k in range(2):
            row_copy(k, r, 0).wait()
        return c

    lax.fori_loop(0, TM, wait, 0)
    w = w_ref[...]
    x = x_ref[...] + g2_ref[0] * (w[:, 4:5] * y_buf[0] + w[:, 5:6] * y_buf[1])
    o_ref[...] = _rms(x, n_ref[...]) if final else x


def _combine(dest, y, x, winfo, mod, norm, tile_fn, n_tiles, final):
    n_tok = n_tiles * TM
    row = lambda i, d: (i, 0)
    return pl.pallas_call(
        functools.partial(_combine_kernel, n_tok=n_tok, final=final),
        grid_spec=pltpu.PrefetchScalarGridSpec(
            num_scalar_prefetch=1,
            grid=(n_tiles,),
            in_specs=[pl.BlockSpec(memory_space=pl.ANY), pl.BlockSpec((TM, D_MODEL), row),
                      pl.BlockSpec((TM, SUBLANES), row), _mod_spec(tile_fn, 5),
                      pl.BlockSpec((1, D_MODEL), lambda i, d: (0, 0))],
            out_specs=pl.BlockSpec((TM, D_MODEL), row),
            scratch_shapes=[pltpu.VMEM((2, TM, D_MODEL), F32), pltpu.SemaphoreType.DMA(())]),
        out_shape=jax.ShapeDtypeStruct((n_tok, D_MODEL), F32),
        compiler_params=_cparams(("arbitrary",)),
        name="moe_combine",
    )(dest, y, x, winfo, mod, norm)


def _moe_block(x, h, info, counts, mod, norm, w_gate, w_up, w_down, tile_fn, n_tiles, final):
    n_tok = n_tiles * TM
    tile_expert, n_valid, src_row, dest = _dispatch_plan(info, counts, n_tok)
    y = _moe(tile_expert, n_valid, src_row, h, w_gate, w_up, w_down)
    return _combine(dest, y, x, info.T, mod, norm, tile_fn, n_tiles, final)


def _rope_tables(dim):
    half, quarter = dim // 2, dim // 4
    t = jnp.arange(SEQ)
    pos = jnp.stack([t // GRID_W, t % GRID_W], axis=-1).astype(F32)
    inv_freq = ROPE_THETA ** (-jnp.arange(0, half, 2, dtype=F32) / half)
    ang = pos[:, :, None] * inv_freq
    cos = jnp.cos(ang)
    sin = jnp.sin(ang)
    cos = jnp.concatenate([cos[:, 0], cos[:, 0], cos[:, 1], cos[:, 1]], axis=-1)
    sin = jnp.concatenate([-sin[:, 0], sin[:, 0], -sin[:, 1], sin[:, 1]], axis=-1)
    pad = LANES - dim
    cos = jnp.pad(cos, ((0, 0), (0, pad)))
    sin = jnp.pad(sin, ((0, 0), (0, pad)))
    ctx_cos = jnp.pad(jnp.ones((CTX_LEN, dim), F32), ((0, 0), (0, pad)))
    return jnp.concatenate([ctx_cos, cos], axis=0), jnp.concatenate([jnp.zeros((CTX_LEN, LANES), F32), sin], axis=0)


def kernel(x, c, ctx, c_ctx, router_w, router_b, final_norm, l0_mod_w, l0_mod_b, l0_norm_mix, l0_norm_ffn, l0_w_in, l0_q_norm, l0_w_uq, l0_kv_norm, l0_w_ukv, l0_conv_w, l0_conv_b, l0_gate_a_w, l0_gate_a_b, l0_gate_x_w, l0_gate_x_b, l0_lru_lambda, l0_w_out, l0_exp_gate, l0_exp_up, l0_exp_down, l1_mod_w, l1_mod_b, l1_norm_mix, l1_norm_ffn, l1_w_in, l1_sink, l1_rpb, l1_w_out, l1_exp_gate, l1_exp_up, l1_exp_down):
    row = lambda v: v.reshape(1, -1)
    xs = jnp.concatenate([ctx, x], axis=1).reshape(TOK, D_MODEL)

    cs = jnp.concatenate([c, c_ctx[None], jnp.zeros((SUBLANES - BATCH - 1, D_MODEL), F32)], axis=0)
    mod0 = _modulation(cs, l0_mod_w, l0_mod_b)
    mod1 = _modulation(cs, l1_mod_w, l1_mod_b)

    perm = jnp.arange(N_EXPERTS).reshape(N_GROUPS, EXPERTS_PER_GROUP).T.reshape(-1)
    rwt = router_w.T[perm]
    rwh = rwt.astype(BF16)
    rwl = (rwt - rwh.astype(F32)).astype(BF16)
    rb = router_b[perm].reshape(N_EXPERTS, 1).astype(F32)
    tri = jnp.triu(jnp.ones((TM, TM), F32), k=1).astype(BF16)
    router = (rwh, rwl, rb, tri)

    split = MLA_Q_RANK + MLA_KV_RANK + MLA_ROPE
    w_in0 = jnp.concatenate([l0_w_in[:, :split], jnp.zeros((D_MODEL, LANES - MLA_ROPE), F32), l0_w_in[:, split:]],
                            axis=1).astype(BF16)
    cqkv, xr, gr = _in0(xs, row(l0_norm_mix), mod0, w_in0)
    wq = jnp.pad(l0_w_uq.reshape(MLA_Q_RANK, MLA_HEADS, MLA_NOPE + MLA_ROPE),
                 ((0, 0), (0, 0), (0, MLA_QK - MLA_NOPE - MLA_ROPE))).reshape(MLA_Q_RANK, MLA_HEADS * MLA_QK)
    wkv = l0_w_ukv.reshape(MLA_KV_RANK, MLA_HEADS, MLA_NOPE + MLA_V)
    wk = wkv[:, :, :MLA_NOPE].reshape(MLA_KV_RANK, MLA_HEADS * MLA_NOPE)
    wv = wkv[:, :, MLA_NOPE:].reshape(MLA_KV_RANK, MLA_HEADS * MLA_V)
    cos0, sin0 = _rope_tables(MLA_ROPE)
    q, k, v = _mla_proj(cqkv, row(l0_q_norm), row(l0_kv_norm), wq.astype(BF16), wk.astype(BF16), wv.astype(BF16),
                        cos0, sin0)
    att = _mla_attn(q, k, v)
    rnn = _rglru(xr, gr, l0_conv_w, l0_conv_b, l0_gate_a_w, l0_gate_a_b, l0_gate_x_w, l0_gate_x_b, l0_lru_lambda)
    n0 = TOK // TM
    xs, h, info, counts = _out_proj(att, rnn, xs, l0_w_out.astype(BF16), mod0, row(l0_norm_ffn), *router,
                                    _comb_tile, n0)
    xs = _moe_block(xs, h, info, counts, mod0, row(final_norm), l0_exp_gate, l0_exp_up, l0_exp_down,
                    _comb_tile, n0, False)

    cos1, sin1 = _rope_tables(HEAD_DIM)
    qw, kw, vw, qn, kn, vn = _in1(xs, row(l1_norm_mix), mod1, l1_w_in.astype(BF16), cos1, sin1)
    win = _win_attn(l1_sink.astype(F32), qw, kw, vw)
    na = _na_attn(qn, kn, vn, _na_bias_table(l1_rpb))
    n1 = BATCH * LAT_TPB
    xl, h, info, counts = _out_proj(win, na, xs, l1_w_out.astype(BF16), mod1, row(l1_norm_ffn), *router,
                                    _lat_tile, n1)
    out = _moe_block(xl, h, info, counts, mod1, row(final_norm), l1_exp_gate, l1_exp_up, l1_exp_down,
                     lambda i: (i, i // LAT_TPB), n1, True)
    return out.reshape(BATCH, SEQ, D_MODEL)
```

```python
import functools

import jax
import jax.numpy as jnp
from jax import lax
from jax.experimental import pallas as pl
from jax.experimental.pallas import tpu as pltpu

F32 = jnp.float32
BF16 = jnp.bfloat16

D_MODEL = 2048
BATCH = 4
SEQ = 2048
GRID_W = 64
CTX_LEN = 256
EPS = 1e-6
NEG_INF = -1e30
ROPE_THETA = 10000.0
N_MOD = 6

MLA_HEADS = 8
MLA_Q_RANK = 512
MLA_KV_RANK = 256
MLA_NOPE = 128
MLA_ROPE = 64
MLA_V = 128

LRU_WIDTH = 1024
LRU_BLOCKS = 8
LRU_C = 8.0

HEAD_DIM = 128
WIN_HEADS = 8
WIN_KV_HEADS = 2
WINDOW = 128
NA_HEADS = 8
NA_ROWS = 8
NA_COLS = 16

N_EXPERTS = 32
N_GROUPS = 8
EXPERTS_PER_GROUP = 4
EXPERT_FF = 512

LANES = 128
SUBLANES = 8
VMEM_LIMIT = 56 * 1024 * 1024

NB = CTX_LEN + SEQ
TOK = BATCH * NB
TM = 256
TPB = NB // TM
LAT_TPB = SEQ // TM
TMX = 256
MLA_QK = 2 * LANES
LOG2E = 1.4426950408889634


def _cparams(sem):
    return pltpu.CompilerParams(dimension_semantics=sem, vmem_limit_bytes=VMEM_LIMIT)


def _resident(shape):
    nd = len(shape)
    return pl.BlockSpec(shape, lambda *_: (0,) * nd, pipeline_mode=pl.Buffered(1))


def _rms(x, g):
    return x * lax.rsqrt(jnp.mean(x * x, axis=-1, keepdims=True) + EPS) * g


def _sigmoid(x):
    return 0.5 * jnp.tanh(0.5 * x) + 0.5


def _dot(a, b):
    return jnp.dot(a, b, preferred_element_type=F32)


def _dot_nt(a, b):
    return lax.dot_general(a, b, (((1,), (1,)), ((), ())), preferred_element_type=F32)


def _swap_blocks(x, blk):
    lane = lax.broadcasted_iota(jnp.int32, x.shape, 1)
    nxt = pltpu.roll(x, LANES - blk, axis=1)
    prv = pltpu.roll(x, blk, axis=1)
    return jnp.where((lane % (2 * blk)) < blk, nxt, prv)


def _rope(x, cos, sin, blk):
    return x * cos + _swap_blocks(x, blk) * sin


def _comb_tile(i):
    b = i // TPB
    return i, jnp.where(i % TPB == 0, BATCH, b)


def _lat_tile(i):
    b = i // LAT_TPB
    return b * TPB + 1 + i % LAT_TPB, b


def _mod_spec(tile_fn, k):
    return pl.BlockSpec((1, 1, D_MODEL), lambda i, *_: (tile_fn(i)[1] * N_MOD + k, 0, 0))


def _mod_kernel(c_ref, w_ref, b_ref, o_ref):
    c = c_ref[...]
    a = (c * jax.nn.sigmoid(c)).astype(BF16)
    o_ref[...] = _dot(a, w_ref[...].astype(BF16)) + b_ref[...]


def _modulation(cs, w, b):
    n = N_MOD * D_MODEL
    tn = 1024
    out = pl.pallas_call(
        _mod_kernel,
        grid=(n // tn,),
        in_specs=[pl.BlockSpec((SUBLANES, D_MODEL), lambda j: (0, 0)),
                  pl.BlockSpec((D_MODEL, tn), lambda j: (0, j)),
                  pl.BlockSpec((1, tn), lambda j: (0, j))],
        out_specs=pl.BlockSpec((SUBLANES, tn), lambda j: (0, j)),
        out_shape=jax.ShapeDtypeStruct((SUBLANES, n), F32),
        compiler_params=_cparams(("arbitrary",)),
        name="modulation",
    )(cs, w, b.reshape(1, n))
    return out.reshape(SUBLANES * N_MOD, 1, D_MODEL)


L0_CQKV = MLA_Q_RANK + MLA_KV_RANK + LANES
L0_IN_PAD = L0_CQKV + 2 * LRU_WIDTH


def _in0_kernel(x_ref, g_ref, sh_ref, sc_ref, w_ref, cqkv_ref, xr_ref, gr_ref):
    h = _rms(x_ref[...], g_ref[...]) * (1.0 + sc_ref[0]) + sh_ref[0]
    y = _dot(h.astype(BF16), w_ref[...])
    cqkv_ref[...] = y[:, :L0_CQKV]
    xr_ref[...] = y[:, L0_CQKV:L0_CQKV + LRU_WIDTH]
    gr_ref[...] = y[:, L0_CQKV + LRU_WIDTH:]


def _in0(x, g, mod, w):
    row = lambda i: (i, 0)
    return pl.pallas_call(
        _in0_kernel,
        grid=(TOK // TM,),
        in_specs=[pl.BlockSpec((TM, D_MODEL), row), _resident((1, D_MODEL)),
                  _mod_spec(_comb_tile, 0), _mod_spec(_comb_tile, 1), _resident((D_MODEL, L0_IN_PAD))],
        out_specs=[pl.BlockSpec((TM, L0_CQKV), row), pl.BlockSpec((TM, LRU_WIDTH), row),
                   pl.BlockSpec((TM, LRU_WIDTH), row)],
        out_shape=[jax.ShapeDtypeStruct((TOK, L0_CQKV), F32), jax.ShapeDtypeStruct((TOK, LRU_WIDTH), F32),
                   jax.ShapeDtypeStruct((TOK, LRU_WIDTH), F32)],
        compiler_params=_cparams(("parallel",)),
        name="l0_in_proj",
    )(x, g, mod, mod, w)


def _mla_proj_kernel(c_ref, qn_ref, kvn_ref, wq_ref, wk_ref, wv_ref, cos_ref, sin_ref, q_ref, k_ref, v_ref):
    c = c_ref[...]
    cos = cos_ref[...]
    sin = sin_ref[...]
    nq = _rms(c[:, :MLA_Q_RANK], qn_ref[...]).astype(BF16)
    q = _dot(nq, wq_ref[...]) * (LOG2E * (MLA_NOPE + MLA_ROPE) ** -0.5)
    nkv = _rms(c[:, MLA_Q_RANK:MLA_Q_RANK + MLA_KV_RANK], kvn_ref[...]).astype(BF16)
    kn = _dot(nkv, wk_ref[...])
    v_ref[...] = _dot(nkv, wv_ref[...]).astype(BF16)
    kr = _rope(c[:, MLA_Q_RANK + MLA_KV_RANK:], cos, sin, MLA_ROPE // 4).astype(BF16)
    for h in range(MLA_HEADS):
        lo = h * MLA_QK
        q_ref[:, lo:lo + LANES] = q[:, lo:lo + LANES].astype(BF16)
        q_ref[:, lo + LANES:lo + MLA_QK] = _rope(q[:, lo + LANES:lo + MLA_QK], cos, sin, MLA_ROPE // 4).astype(BF16)
        k_ref[:, lo:lo + LANES] = kn[:, h * LANES:(h + 1) * LANES].astype(BF16)
        k_ref[:, lo + LANES:lo + MLA_QK] = kr


def _mla_proj(cqkv, qn, kvn, wq, wk, wv, cos, sin):
    row = lambda i: (i, 0)
    pos = lambda i: (i % TPB, 0)
    hq = MLA_HEADS * MLA_QK
    hv = MLA_HEADS * MLA_V
    return pl.pallas_call(
        _mla_proj_kernel,
        grid=(TOK // TM,),
        in_specs=[pl.BlockSpec((TM, L0_CQKV), row), _resident((1, MLA_Q_RANK)), _resident((1, MLA_KV_RANK)),
                  _resident((MLA_Q_RANK, hq)), _resident((MLA_KV_RANK, hv)), _resident((MLA_KV_RANK, hv)),
                  pl.BlockSpec((TM, LANES), pos), pl.BlockSpec((TM, LANES), pos)],
        out_specs=[pl.BlockSpec((TM, hq), row), pl.BlockSpec((TM, hq), row), pl.BlockSpec((TM, hv), row)],
        out_shape=[jax.ShapeDtypeStruct((TOK, hq), BF16), jax.ShapeDtypeStruct((TOK, hq), BF16),
                   jax.ShapeDtypeStruct((TOK, hv), BF16)],
        compiler_params=_cparams(("parallel",)),
        name="mla_proj",
    )(cqkv, qn, kvn, wq, wk, wv, cos, sin)


MLA_TQ = 256


def _softmax_pv(s, v):
    m = jnp.max(s, axis=-1, keepdims=True)
    p = jnp.exp2(s - m)
    l = jnp.sum(p, axis=-1, keepdims=True)
    return _dot(p.astype(BF16), v) / l


def _mla_attn_kernel(q_ref, k_ref, v_ref, o_ref):
    s = _dot_nt(q_ref[0:CTX_LEN, :], k_ref[0:CTX_LEN, :])
    o_ref[0:CTX_LEN, :] = _softmax_pv(s, v_ref[0:CTX_LEN, :]).astype(o_ref.dtype)

    def body(i, carry):
        for t in range(2):
            r0 = pl.multiple_of(CTX_LEN + (2 * i + t) * MLA_TQ, MLA_TQ)
            s = _dot_nt(q_ref[pl.ds(r0, MLA_TQ), :], k_ref[...])
            o_ref[pl.ds(r0, MLA_TQ), :] = _softmax_pv(s, v_ref[...]).astype(o_ref.dtype)
        return carry

    lax.fori_loop(0, SEQ // (2 * MLA_TQ), body, 0)


def _mla_attn(q, k, v):
    blk = lambda b, h: (b, h)
    return pl.pallas_call(
        _mla_attn_kernel,
        grid=(BATCH, MLA_HEADS),
        in_specs=[pl.BlockSpec((NB, MLA_QK), blk), pl.BlockSpec((NB, MLA_QK), blk), pl.BlockSpec((NB, MLA_V), blk)],
        out_specs=pl.BlockSpec((NB, MLA_V), blk),
        out_shape=jax.ShapeDtypeStruct((TOK, MLA_HEADS * MLA_V), BF16),
        compiler_params=_cparams(("parallel", "parallel")),
        name="mla_attn",
    )(q, k, v)


LRU_BW = LRU_WIDTH // LRU_BLOCKS
CTX_GROUPS = CTX_LEN // SUBLANES
LAT_GROUPS = SEQ // SUBLANES


def _scan_group(a, b, reverse):
    row = lax.broadcasted_iota(jnp.int32, a.shape, 0)
    for d in (1, 2, 4):
        shift = SUBLANES - d if reverse else d
        a_s = pltpu.roll(a, shift, axis=0)
        b_s = pltpu.roll(b, shift, axis=0)
        m = (row < SUBLANES - d) if reverse else (row >= d)
        b = jnp.where(m, a * b_s + b, b)
        a = jnp.where(m, a * a_s, a)
    return a, b


def _rglru_kernel(xr_ref, gr_ref, cw_ref, cb_ref, wa_ref, ba_ref, wx_ref, bx_ref, lam_ref, y_ref,
                  af_ref, bf_ref, ab_ref, bb_ref, hf_ref, hb_ref):
    row8 = lax.broadcasted_iota(jnp.int32, (SUBLANES, LRU_BW), 0)

    def taps(seg):
        n = seg.shape[0]

        def shifted(shift, keep, first):
            r = pltpu.roll(seg, shift % n, axis=0)
            if first:
                return jnp.concatenate([jnp.where(keep, r[:SUBLANES], 0.0), r[SUBLANES:]], axis=0)
            return jnp.concatenate([r[:-SUBLANES], jnp.where(keep, r[-SUBLANES:], 0.0)], axis=0)

        return shifted(2, row8 >= 2, True), shifted(1, row8 >= 1, True), shifted(-1, row8 < SUBLANES - 1, False)

    x = xr_ref[...]
    tc = taps(x[:CTX_LEN])
    tl = taps(x[CTX_LEN:])
    xm2, xm1, xp1 = [jnp.concatenate([a, b], axis=0) for a, b in zip(tc, tl)]
    u = cb_ref[...] + xm2 * cw_ref[0:1, :] + xm1 * cw_ref[1:2, :] + x * cw_ref[2:3, :] + xp1 * cw_ref[3:4, :]
    ub = u.astype(BF16)
    for d, (a_ref, b_ref) in enumerate(((af_ref, bf_ref), (ab_ref, bb_ref))):
        r = _sigmoid(_dot(ub, wa_ref[d, 0].astype(BF16)) + ba_ref[d:d + 1, :])
        ig = _sigmoid(_dot(ub, wx_ref[d, 0].astype(BF16)) + bx_ref[d:d + 1, :])
        z = -lam_ref[d:d + 1, :]
        softplus = jnp.maximum(z, 0.0) + jnp.log(1.0 + jnp.exp(-jnp.abs(z)))
        log_a = -LRU_C * r * softplus
        a = jnp.exp(log_a)
        a_ref[...] = a
        t = 1.0 - a * a
        b_ref[...] = jnp.where(t > 0.0, t * lax.rsqrt(t), 0.0) * (ig * u)

    def step(gf, gb, hf, hb):
        rf = pl.multiple_of(gf * SUBLANES, SUBLANES)
        a, b = _scan_group(af_ref[pl.ds(rf, SUBLANES), :], bf_ref[pl.ds(rf, SUBLANES), :], False)
        h = a * hf + b
        hf_ref[pl.ds(rf, SUBLANES), :] = h
        hf = jnp.broadcast_to(h[SUBLANES - 1:SUBLANES, :], h.shape)
        rb = pl.multiple_of(gb * SUBLANES, SUBLANES)
        a, b = _scan_group(ab_ref[pl.ds(rb, SUBLANES), :], bb_ref[pl.ds(rb, SUBLANES), :], True)
        h = a * hb + b
        hb_ref[pl.ds(rb, SUBLANES), :] = h
        hb = jnp.broadcast_to(h[0:1, :], h.shape)
        return hf, hb

    zero = jnp.zeros((SUBLANES, LRU_BW), F32)
    carry = lax.fori_loop(0, CTX_GROUPS, lambda i, c: step(i, CTX_GROUPS - 1 - i, *c), (zero, zero), unroll=2)
    lax.fori_loop(0, LAT_GROUPS, lambda i, c: step(CTX_GROUPS + i, CTX_GROUPS + LAT_GROUPS - 1 - i, *c), carry, unroll=2)
    y_ref[...] = ((hf_ref[...] + hb_ref[...]) * jax.nn.gelu(gr_ref[...])).astype(y_ref.dtype)


def _rglru(xr, gr, conv_w, conv_b, wa, ba, wx, bx, lam):
    blk = lambda b, n: (b, n)
    col = lambda b, n: (0, n)
    gate = lambda b, n: (0, n, 0, 0)
    seg = pltpu.VMEM((NB, LRU_BW), F32)
    return pl.pallas_call(
        _rglru_kernel,
        grid=(BATCH, LRU_BLOCKS),
        in_specs=[pl.BlockSpec((NB, LRU_BW), blk), pl.BlockSpec((NB, LRU_BW), blk),
                  pl.BlockSpec((4, LRU_BW), col), pl.BlockSpec((1, LRU_BW), col),
                  pl.BlockSpec((2, 1, LRU_BW, LRU_BW), gate), pl.BlockSpec((2, LRU_BW), col),
                  pl.BlockSpec((2, 1, LRU_BW, LRU_BW), gate), pl.BlockSpec((2, LRU_BW), col),
                  pl.BlockSpec((2, LRU_BW), col)],
        out_specs=pl.BlockSpec((NB, LRU_BW), blk),
        out_shape=jax.ShapeDtypeStruct((TOK, LRU_WIDTH), BF16),
        scratch_shapes=[seg, seg, seg, seg, seg, seg],
        compiler_params=_cparams(("parallel", "parallel")),
        name="rglru",
    )(xr, gr, conv_w, conv_b.reshape(1, LRU_WIDTH), wa, ba, wx, bx, lam)


L1_Q = WIN_HEADS * HEAD_DIM
L1_KV = WIN_KV_HEADS * HEAD_DIM
L1_NA = NA_HEADS * HEAD_DIM
L1_IN = L1_Q + 2 * L1_KV + 3 * L1_NA


def _in1_kernel(x_ref, g_ref, sh_ref, sc_ref, w_ref, cos_ref, sin_ref,
                qw_ref, kw_ref, vw_ref, qn_ref, kn_ref, vn_ref):
    h = _rms(x_ref[...], g_ref[...]) * (1.0 + sc_ref[0]) + sh_ref[0]
    y = _dot(h.astype(BF16), w_ref[...])
    cos = cos_ref[...]
    sin = sin_ref[...]
    scale = LOG2E * HEAD_DIM ** -0.5
    for hd in range(WIN_HEADS):
        lo = hd * HEAD_DIM
        qw_ref[:, lo:lo + HEAD_DIM] = _rope(y[:, lo:lo + HEAD_DIM] * scale, cos, sin, HEAD_DIM // 4).astype(BF16)
    for hd in range(WIN_KV_HEADS):
        lo = hd * HEAD_DIM
        kw_ref[:, lo:lo + HEAD_DIM] = _rope(y[:, L1_Q + lo:L1_Q + lo + HEAD_DIM], cos, sin, HEAD_DIM // 4).astype(BF16)
    o = L1_Q + L1_KV
    vw_ref[...] = y[:, o:o + L1_KV].astype(BF16)
    o += L1_KV
    qn_ref[...] = (y[:, o:o + L1_NA] * scale).astype(BF16)
    kn_ref[...] = y[:, o + L1_NA:o + 2 * L1_NA].astype(BF16)
    vn_ref[...] = y[:, o + 2 * L1_NA:].astype(BF16)


def _in1(x, g, mod, w, cos, sin):
    row = lambda i: (i, 0)
    pos = lambda i: (i % TPB, 0)
    widths = (L1_Q, L1_KV, L1_KV, L1_NA, L1_NA, L1_NA)
    return pl.pallas_call(
        _in1_kernel,
        grid=(TOK // TM,),
        in_specs=[pl.BlockSpec((TM, D_MODEL), row), _resident((1, D_MODEL)),
                  _mod_spec(_comb_tile, 0), _mod_spec(_comb_tile, 1), _resident((D_MODEL, L1_IN)),
                  pl.BlockSpec((TM, LANES), pos), pl.BlockSpec((TM, LANES), pos)],
        out_specs=[pl.BlockSpec((TM, n), row) for n in widths],
        out_shape=[jax.ShapeDtypeStruct((TOK, n), BF16) for n in widths],
        compiler_params=_cparams(("parallel",)),
        name="l1_in_proj",
    )(x, g, mod, mod, w, cos, sin)


WIN_TQ = 128
WIN_SPAN = WIN_TQ + 2 * WINDOW
WIN_G = WIN_HEADS // WIN_KV_HEADS


def _win_kernel(sink_ref, q_ref, k_ref, v_ref, o_ref):
    hk = pl.program_id(1)
    o_ref[0:CTX_LEN, :] = jnp.zeros((CTX_LEN, WIN_G * HEAD_DIM), o_ref.dtype)
    rows = WIN_G * WIN_TQ
    head = lax.broadcasted_iota(jnp.int32, (rows, 1), 0) // WIN_TQ
    sink = jnp.zeros((rows, 1), F32)
    for g in range(WIN_G):
        sink = jnp.where(head == g, sink_ref[hk * WIN_G + g] * LOG2E, sink)
    qoff = lax.broadcasted_iota(jnp.int32, (rows, WIN_SPAN), 0) % WIN_TQ
    koff = lax.broadcasted_iota(jnp.int32, (rows, WIN_SPAN), 1)

    def body(n, carry):
        r0 = pl.multiple_of(CTX_LEN + n * WIN_TQ, WIN_TQ)
        start = jnp.clip((n - 1) * WIN_TQ, 0, SEQ - WIN_SPAN)
        ks = pl.multiple_of(CTX_LEN + start, WIN_TQ)
        q4 = q_ref[pl.ds(r0, WIN_TQ), :]
        q = jnp.concatenate([q4[:, g * HEAD_DIM:(g + 1) * HEAD_DIM] for g in range(WIN_G)], axis=0)
        s_c = _dot_nt(q, k_ref[0:CTX_LEN, :])
        s_w = _dot_nt(q, k_ref[pl.ds(ks, WIN_SPAN), :])
        valid = jnp.abs(n * WIN_TQ + qoff - (start + koff)) <= WINDOW
        s_w = jnp.where(valid, s_w, NEG_INF)
        m = jnp.maximum(jnp.maximum(jnp.max(s_c, axis=-1, keepdims=True), jnp.max(s_w, axis=-1, keepdims=True)), sink)
        p_c = jnp.exp2(s_c - m)
        p_w = jnp.exp2(s_w - m)
        l = jnp.sum(p_c, axis=-1, keepdims=True) + jnp.sum(p_w, axis=-1, keepdims=True) + jnp.exp2(sink - m)
        o = (_dot(p_w.astype(BF16), v_ref[pl.ds(ks, WIN_SPAN), :]) + _dot(p_c.astype(BF16), v_ref[0:CTX_LEN, :])) / l
        for g in range(WIN_G):
            o_ref[pl.ds(r0, WIN_TQ), g * HEAD_DIM:(g + 1) * HEAD_DIM] = o[g * WIN_TQ:(g + 1) * WIN_TQ].astype(o_ref.dtype)
        return carry

    lax.fori_loop(0, SEQ // WIN_TQ, body, 0)


def _win_attn(sink, q, k, v):
    blk = lambda b, h, *_: (b, h)
    return pl.pallas_call(
        _win_kernel,
        grid_spec=pltpu.PrefetchScalarGridSpec(
            num_scalar_prefetch=1,
            grid=(BATCH, WIN_KV_HEADS),
            in_specs=[pl.BlockSpec((NB, WIN_G * HEAD_DIM), blk), pl.BlockSpec((NB, HEAD_DIM), blk),
                      pl.BlockSpec((NB, HEAD_DIM), blk)],
            out_specs=pl.BlockSpec((NB, WIN_G * HEAD_DIM), blk)),
        out_shape=jax.ShapeDtypeStruct((TOK, L1_Q), BF16),
        compiler_params=_cparams(("parallel", "parallel")),
        name="window_attn",
    )(sink, q, k, v)


NA_GRID_ROWS = SEQ // GRID_W
NA_BAND = NA_ROWS * GRID_W


NA_RPI = 4


def _na_kernel(q_ref, k_ref, v_ref, bias_ref, o_ref):
    o_ref[0:CTX_LEN, :] = jnp.zeros((CTX_LEN, HEAD_DIM), o_ref.dtype)

    def body(i, carry):
        qs = pl.multiple_of(CTX_LEN + i * (NA_RPI * GRID_W), NA_RPI * GRID_W)
        q = q_ref[pl.ds(qs, NA_RPI * GRID_W), :]
        s_c = _dot_nt(q, k_ref[0:CTX_LEN, :])
        starts = []
        s_w = []
        for j in range(NA_RPI):
            r = i * NA_RPI + j
            r0 = jnp.clip(r - NA_ROWS // 2, 0, NA_GRID_ROWS - NA_ROWS)
            ks = pl.multiple_of(CTX_LEN + r0 * GRID_W, GRID_W)
            starts.append(ks)
            s_w.append(_dot_nt(q[j * GRID_W:(j + 1) * GRID_W], k_ref[pl.ds(ks, NA_BAND), :])
                       + bias_ref[0, r0 - r + NA_ROWS - 1])
        s_w = jnp.concatenate(s_w, axis=0)
        m = jnp.maximum(jnp.max(s_c, axis=-1, keepdims=True), jnp.max(s_w, axis=-1, keepdims=True))
        p_c = jnp.exp2(s_c - m)
        p_w = jnp.exp2(s_w - m)
        l = jnp.sum(p_c, axis=-1, keepdims=True) + jnp.sum(p_w, axis=-1, keepdims=True)
        p_w = p_w.astype(BF16)
        o_w = jnp.concatenate([_dot(p_w[j * GRID_W:(j + 1) * GRID_W], v_ref[pl.ds(starts[j], NA_BAND), :])
                               for j in range(NA_RPI)], axis=0)
        o = (o_w + _dot(p_c.astype(BF16), v_ref[0:CTX_LEN, :])) / l
        o_ref[pl.ds(qs, NA_RPI * GRID_W), :] = o.astype(o_ref.dtype)
        return carry

    lax.fori_loop(0, NA_GRID_ROWS // NA_RPI, body, 0)


def _na_attn(q, k, v, bias):
    blk = lambda b, h: (b, h)
    return pl.pallas_call(
        _na_kernel,
        grid=(BATCH, NA_HEADS),
        in_specs=[pl.BlockSpec((NB, HEAD_DIM), blk), pl.BlockSpec((NB, HEAD_DIM), blk),
                  pl.BlockSpec((NB, HEAD_DIM), blk),
                  pl.BlockSpec((1, NA_ROWS, GRID_W, NA_BAND), lambda b, h: (h, 0, 0, 0))],
        out_specs=pl.BlockSpec((NB, HEAD_DIM), blk),
        out_shape=jax.ShapeDtypeStruct((TOK, L1_NA), BF16),
        compiler_params=_cparams(("parallel", "parallel")),
        name="na_attn",
    )(q, k, v, bias)


def _na_bias_table(rpb):
    cols = jnp.arange(GRID_W)
    c0 = jnp.clip(cols - NA_COLS // 2, 0, GRID_W - NA_COLS)
    kc = cols[None, :]
    valid = (kc >= c0[:, None]) & (kc < c0[:, None] + NA_COLS)
    lo = GRID_W - NA_COLS
    ext = jnp.pad(rpb.astype(F32) * LOG2E, ((0, 0), (0, 0), (lo, lo)))
    tbl = jnp.stack([ext[:, :, GRID_W - 1 - c:2 * GRID_W - 1 - c] for c in range(GRID_W)], axis=2)
    tbl = jnp.where(valid[None, None], tbl, NEG_INF)
    tbl = jnp.stack([tbl[:, d:d + NA_ROWS] for d in range(NA_ROWS)], axis=1)
    return jnp.transpose(tbl, (0, 1, 3, 2, 4)).reshape(NA_HEADS, NA_ROWS, GRID_W, NA_BAND)


def _router(h, rwh_ref, rwl_ref, rb_ref, tri_ref, carry_ref, info_ref, counts_ref):
    tm = h.shape[0]
    hh = h.astype(BF16)
    hl = (h - hh.astype(F32)).astype(BF16)
    logits = _dot_nt(rwh_ref[...], hh) + (_dot_nt(rwh_ref[...], hl) + _dot_nt(rwl_ref[...], hh))
    scores = jax.nn.sigmoid(logits)
    biased = scores + rb_ref[...]
    nj = EXPERTS_PER_GROUP
    s = [biased[j * N_GROUPS:(j + 1) * N_GROUPS] for j in range(nj)]
    u = [scores[j * N_GROUPS:(j + 1) * N_GROUPS] for j in range(nj)]
    gs = None
    for a in range(nj):
        for b in range(a + 1, nj):
            pair = s[a] + s[b]
            gs = pair if gs is None else jnp.maximum(gs, pair)
    giota = lax.broadcasted_iota(jnp.int32, (N_GROUPS, tm), 0).astype(F32)
    gmax = jnp.max(gs, axis=0, keepdims=True)
    gidx = jnp.min(jnp.where(gs == gmax, giota, float(N_GROUPS)), axis=0, keepdims=True)
    gm = giota == gidx
    v = [jnp.sum(jnp.where(gm, s[j], 0.0), axis=0, keepdims=True) for j in range(nj)]
    w = [jnp.sum(jnp.where(gm, u[j], 0.0), axis=0, keepdims=True) for j in range(nj)]
    sel = []
    for j in range(nj):
        beaten = jnp.zeros((1, tm), F32)
        for i in range(nj):
            if i != j:
                ahead = (v[i] >= v[j]) if i < j else (v[i] > v[j])
                beaten = beaten + jnp.where(ahead, 1.0, 0.0)
        sel.append(beaten < 2.0)
    wsum = sum(jnp.where(sel[j], w[j], 0.0) for j in range(nj))
    first = functools.reduce(jnp.minimum, [jnp.where(sel[j], float(j), float(nj)) for j in range(nj)])
    last = functools.reduce(jnp.maximum, [jnp.where(sel[j], float(j), -1.0) for j in range(nj)])
    gmf = jnp.where(gm, 1.0, 0.0)
    cnt = jnp.concatenate([jnp.where(sel[j], gmf, 0.0) for j in range(nj)], axis=0)
    pos = _dot(cnt.astype(BF16), tri_ref[...]) + carry_ref[:, 0:1]
    rank = [jnp.sum(cnt[j * N_GROUPS:(j + 1) * N_GROUPS] * pos[j * N_GROUPS:(j + 1) * N_GROUPS], axis=0, keepdims=True)
            for j in range(nj)]
    pick = lambda which, vals: sum(jnp.where(which == float(j), vals[j], 0.0) for j in range(nj))
    gate = [w[j] / wsum for j in range(nj)]
    zero = jnp.zeros((1, tm), F32)
    info_ref[...] = jnp.concatenate(
        [gidx * nj + first, gidx * nj + last, pick(first, rank), pick(last, rank), pick(first, gate), pick(last, gate),
         zero, zero], axis=0)
    new = carry_ref[...] + jnp.sum(cnt, axis=1, keepdims=True)
    carry_ref[...] = new
    counts_ref[...] = new


def _out_kernel(ya_ref, yb_ref, x_ref, wa_ref, wb_ref, g1_ref, n_ref, sh_ref, sc_ref,
                rwh_ref, rwl_ref, rb_ref, tri_ref, xo_ref, h_ref, info_ref, counts_ref, carry_ref):
    @pl.when(pl.program_id(0) == 0)
    def _():
        carry_ref[...] = jnp.zeros_like(carry_ref)

    y = _dot(ya_ref[...], wa_ref[...]) + _dot(yb_ref[...], wb_ref[...])
    x = x_ref[...] + g1_ref[0] * y
    xo_ref[...] = x
    h = _rms(x, n_ref[...]) * (1.0 + sc_ref[0]) + sh_ref[0]
    h_ref[...] = h
    _router(h, rwh_ref, rwl_ref, rb_ref, tri_ref, carry_ref, info_ref, counts_ref)


def _out_proj(ya, yb, x, w_out, mod, norm, rwh, rwl, rb, tri, tile_fn, n_tiles):
    half = w_out.shape[0] // 2
    src = lambda i: (tile_fn(i)[0], 0)
    dst = lambda i: (i, 0)
    n_tok = n_tiles * TM
    return pl.pallas_call(
        _out_kernel,
        grid=(n_tiles,),
        in_specs=[pl.BlockSpec((TM, half), src), pl.BlockSpec((TM, half), src), pl.BlockSpec((TM, D_MODEL), src),
                  _resident((half, D_MODEL)), _resident((half, D_MODEL)),
                  _mod_spec(tile_fn, 2), _resident((1, D_MODEL)), _mod_spec(tile_fn, 3), _mod_spec(tile_fn, 4),
                  _resident((N_EXPERTS, D_MODEL)), _resident((N_EXPERTS, D_MODEL)), _resident((N_EXPERTS, 1)),
                  _resident((TM, TM))],
        out_specs=[pl.BlockSpec((TM, D_MODEL), dst), pl.BlockSpec((TM, D_MODEL), dst),
                   pl.BlockSpec((SUBLANES, TM), lambda i: (0, i)),
                   pl.BlockSpec((N_EXPERTS, LANES), lambda i: (0, 0))],
        out_shape=[jax.ShapeDtypeStruct((n_tok, D_MODEL), F32), jax.ShapeDtypeStruct((n_tok, D_MODEL), F32),
                   jax.ShapeDtypeStruct((SUBLANES, n_tok), F32), jax.ShapeDtypeStruct((N_EXPERTS, LANES), F32)],
        scratch_shapes=[pltpu.VMEM((N_EXPERTS, LANES), F32)],
        compiler_params=_cparams(("arbitrary",)),
        name="out_proj_router",
    )(ya, yb, x, w_out[:half], w_out[half:], mod, norm, mod, mod, rwh, rwl, rb, tri)


def _moe_kernel(te_ref, nv_ref, src_ref, h_hbm, wg_ref, wu_ref, wd_ref, y_ref, x_buf, wg_s, wu_s, wd_s, sem):
    g = pl.program_id(0)
    first = jnp.logical_or(g == 0, te_ref[g] != te_ref[jnp.maximum(g - 1, 0)])

    @pl.when(g < nv_ref[0])
    def _():
        def row_copy(r, src):
            return pltpu.make_async_copy(h_hbm.at[pl.ds(src, 1)], x_buf.at[pl.ds(r, 1)], sem)

        def start(r, c):
            row_copy(r, src_ref[g * TMX + r]).start()
            return c

        lax.fori_loop(0, TMX, start, 0, unroll=8)

        @pl.when(first)
        def _():
            wg_s[...] = wg_ref[0].astype(BF16)
            wu_s[...] = wu_ref[0].astype(BF16)
            wd_s[...] = wd_ref[0].astype(BF16)

        def wait(r, c):
            row_copy(r, 0).wait()
            return c

        lax.fori_loop(0, TMX, wait, 0, unroll=8)
        x = x_buf[...].astype(BF16)
        hg = _dot(x, wg_s[...])
        he = (hg * jax.nn.sigmoid(hg)) * _dot(x, wu_s[...])
        y_ref[...] = _dot(he.astype(BF16), wd_s[...])

    @pl.when(g >= nv_ref[0])
    def _():
        y_ref[...] = jnp.zeros_like(y_ref)


def _moe(tile_expert, n_valid, src_row, h, w_gate, w_up, w_down):
    n_tiles = tile_expert.shape[0]
    wsel = lambda g, te, nv, src: (te[g], 0, 0)
    return pl.pallas_call(
        _moe_kernel,
        grid_spec=pltpu.PrefetchScalarGridSpec(
            num_scalar_prefetch=3,
            grid=(n_tiles,),
            in_specs=[pl.BlockSpec(memory_space=pl.ANY),
                      pl.BlockSpec((1, D_MODEL, EXPERT_FF), wsel), pl.BlockSpec((1, D_MODEL, EXPERT_FF), wsel),
                      pl.BlockSpec((1, EXPERT_FF, D_MODEL), wsel)],
            out_specs=pl.BlockSpec((TMX, D_MODEL), lambda g, te, nv, src: (g, 0)),
            scratch_shapes=[pltpu.VMEM((TMX, D_MODEL), F32), pltpu.VMEM((D_MODEL, EXPERT_FF), BF16),
                            pltpu.VMEM((D_MODEL, EXPERT_FF), BF16), pltpu.VMEM((EXPERT_FF, D_MODEL), BF16),
                            pltpu.SemaphoreType.DMA(())]),
        out_shape=jax.ShapeDtypeStruct((n_tiles * TMX, D_MODEL), F32),
        compiler_params=_cparams(("arbitrary",)),
        name="moe_experts",
    )(tile_expert, n_valid, src_row, h, w_gate, w_up, w_down)


def _dispatch_plan(info, counts, n_tok):
    n_tiles = 2 * n_tok // TMX + N_EXPERTS
    cnt = counts[:, 0].astype(jnp.int32).reshape(EXPERTS_PER_GROUP, N_GROUPS).T.reshape(N_EXPERTS)
    ntile = (cnt + TMX - 1) // TMX
    tile_end = jnp.cumsum(ntile)
    n_valid = tile_end[-1]
    row0 = (tile_end - ntile) * TMX
    e_sel = info[0:2].astype(jnp.int32)
    onehot = e_sel[:, :, None] == jnp.arange(N_EXPERTS, dtype=jnp.int32)
    dest = jnp.sum(jnp.where(onehot, row0, 0), axis=-1) + info[2:4].astype(jnp.int32)
    tok = jnp.broadcast_to(jnp.arange(n_tok, dtype=jnp.int32), (2, n_tok))
    src_row = jnp.zeros((n_tiles * TMX,), jnp.int32).at[dest.reshape(-1)].set(tok.reshape(-1))
    tile_id = jnp.minimum(jnp.arange(n_tiles, dtype=jnp.int32), n_valid - 1)
    tile_expert = jnp.sum(tile_end[None, :] <= tile_id[:, None], axis=1).astype(jnp.int32)
    return tile_expert, n_valid.reshape(1).astype(jnp.int32), src_row, dest.reshape(-1)


def _combine_kernel(dest_ref, y_hbm, x_ref, w_ref, g2_ref, n_ref, o_ref, y_buf, sem, *, n_tok, final):
    i = pl.program_id(0)

    def row_copy(k, r, d):
        return pltpu.make_async_copy(y_hbm.at[pl.ds(d, 1)], y_buf.at[k, pl.ds(r, 1)], sem)

    def start(r, c):
        for k in range(2):
            row_copy(k, r, dest_ref[k * n_tok + i * TM + r]).start()
        return c

    lax.fori_loop(0, TM, start, 0, unroll=8)

    def wait(r, c):
        for k in range(2):
            row_copy(k, r, 0).wait()
        return c

    lax.fori_loop(0, TM, wait, 0, unroll=8)
    w = w_ref[...]
    x = x_ref[...] + g2_ref[0] * (w[:, 4:5] * y_buf[0] + w[:, 5:6] * y_buf[1])
    o_ref[...] = _rms(x, n_ref[...]) if final else x


def _combine(dest, y, x, winfo, mod, norm, tile_fn, n_tiles, final):
    n_tok = n_tiles * TM
    row = lambda i, d: (i, 0)
    return pl.pallas_call(
        functools.partial(_combine_kernel, n_tok=n_tok, final=final),
        grid_spec=pltpu.PrefetchScalarGridSpec(
            num_scalar_prefetch=1,
            grid=(n_tiles,),
            in_specs=[pl.BlockSpec(memory_space=pl.ANY), pl.BlockSpec((TM, D_MODEL), row),
                      pl.BlockSpec((TM, SUBLANES), row), _mod_spec(tile_fn, 5),
                      pl.BlockSpec((1, D_MODEL), lambda i, d: (0, 0))],
            out_specs=pl.BlockSpec((TM, D_MODEL), row),
            scratch_shapes=[pltpu.VMEM((2, TM, D_MODEL), F32), pltpu.SemaphoreType.DMA(())]),
        out_shape=jax.ShapeDtypeStruct((n_tok, D_MODEL), F32),
        compiler_params=_cparams(("arbitrary",)),
        name="moe_combine",
    )(dest, y, x, winfo, mod, norm)


def _moe_block(x, h, info, counts, mod, norm, w_gate, w_up, w_down, tile_fn, n_tiles, final):
    n_tok = n_tiles * TM
    tile_expert, n_valid, src_row, dest = _dispatch_plan(info, counts, n_tok)
    y = _moe(tile_expert, n_valid, src_row, h, w_gate, w_up, w_down)
    return _combine(dest, y, x, info.T, mod, norm, tile_fn, n_tiles, final)


def _rope_tables(dim):
    half, quarter = dim // 2, dim // 4
    t = jnp.arange(SEQ)
    pos = jnp.stack([t // GRID_W, t % GRID_W], axis=-1).astype(F32)
    inv_freq = ROPE_THETA ** (-jnp.arange(0, half, 2, dtype=F32) / half)
    ang = pos[:, :, None] * inv_freq
    cos = jnp.cos(ang)
    sin = jnp.sin(ang)
    cos = jnp.concatenate([cos[:, 0], cos[:, 0], cos[:, 1], cos[:, 1]], axis=-1)
    sin = jnp.concatenate([-sin[:, 0], sin[:, 0], -sin[:, 1], sin[:, 1]], axis=-1)
    pad = LANES - dim
    cos = jnp.pad(cos, ((0, 0), (0, pad)))
    sin = jnp.pad(sin, ((0, 0), (0, pad)))
    ctx_cos = jnp.pad(jnp.ones((CTX_LEN, dim), F32), ((0, 0), (0, pad)))
    return jnp.concatenate([ctx_cos, cos], axis=0), jnp.concatenate([jnp.zeros((CTX_LEN, LANES), F32), sin], axis=0)


def kernel(x, c, ctx, c_ctx, router_w, router_b, final_norm, l0_mod_w, l0_mod_b, l0_norm_mix, l0_norm_ffn, l0_w_in, l0_q_norm, l0_w_uq, l0_kv_norm, l0_w_ukv, l0_conv_w, l0_conv_b, l0_gate_a_w, l0_gate_a_b, l0_gate_x_w, l0_gate_x_b, l0_lru_lambda, l0_w_out, l0_exp_gate, l0_exp_up, l0_exp_down, l1_mod_w, l1_mod_b, l1_norm_mix, l1_norm_ffn, l1_w_in, l1_sink, l1_rpb, l1_w_out, l1_exp_gate, l1_exp_up, l1_exp_down):
    row = lambda v: v.reshape(1, -1)
    xs = jnp.concatenate([ctx, x], axis=1).reshape(TOK, D_MODEL)

    cs = jnp.concatenate([c, c_ctx[None], jnp.zeros((SUBLANES - BATCH - 1, D_MODEL), F32)], axis=0)
    mod0 = _modulation(cs, l0_mod_w, l0_mod_b)
    mod1 = _modulation(cs, l1_mod_w, l1_mod_b)

    perm = jnp.arange(N_EXPERTS).reshape(N_GROUPS, EXPERTS_PER_GROUP).T.reshape(-1)
    rwt = router_w.T[perm]
    rwh = rwt.astype(BF16)
    rwl = (rwt - rwh.astype(F32)).astype(BF16)
    rb = router_b[perm].reshape(N_EXPERTS, 1).astype(F32)
    tri = jnp.triu(jnp.ones((TM, TM), F32), k=1).astype(BF16)
    router = (rwh, rwl, rb, tri)

    split = MLA_Q_RANK + MLA_KV_RANK + MLA_ROPE
    w_in0 = jnp.concatenate([l0_w_in[:, :split], jnp.zeros((D_MODEL, LANES - MLA_ROPE), F32), l0_w_in[:, split:]],
                            axis=1).astype(BF16)
    cqkv, xr, gr = _in0(xs, row(l0_norm_mix), mod0, w_in0)
    wq = jnp.pad(l0_w_uq.reshape(MLA_Q_RANK, MLA_HEADS, MLA_NOPE + MLA_ROPE),
                 ((0, 0), (0, 0), (0, MLA_QK - MLA_NOPE - MLA_ROPE))).reshape(MLA_Q_RANK, MLA_HEADS * MLA_QK)
    wkv = l0_w_ukv.reshape(MLA_KV_RANK, MLA_HEADS, MLA_NOPE + MLA_V)
    wk = wkv[:, :, :MLA_NOPE].reshape(MLA_KV_RANK, MLA_HEADS * MLA_NOPE)
    wv = wkv[:, :, MLA_NOPE:].reshape(MLA_KV_RANK, MLA_HEADS * MLA_V)
    cos0, sin0 = _rope_tables(MLA_ROPE)
    q, k, v = _mla_proj(cqkv, row(l0_q_norm), row(l0_kv_norm), wq.astype(BF16), wk.astype(BF16), wv.astype(BF16),
                        cos0, sin0)
    att = _mla_attn(q, k, v)
    rnn = _rglru(xr, gr, l0_conv_w, l0_conv_b, l0_gate_a_w, l0_gate_a_b, l0_gate_x_w, l0_gate_x_b, l0_lru_lambda)
    n0 = TOK // TM
    xs, h, info, counts = _out_proj(att, rnn, xs, l0_w_out.astype(BF16), mod0, row(l0_norm_ffn), *router,
                                    _comb_tile, n0)
    xs = _moe_block(xs, h, info, counts, mod0, row(final_norm), l0_exp_gate, l0_exp_up, l0_exp_down,
                    _comb_tile, n0, False)

    cos1, sin1 = _rope_tables(HEAD_DIM)
    qw, kw, vw, qn, kn, vn = _in1(xs, row(l1_norm_mix), mod1, l1_w_in.astype(BF16), cos1, sin1)
    win = _win_attn(l1_sink.astype(F32), qw, kw, vw)
    na = _na_attn(qn, kn, vn, _na_bias_table(l1_rpb))
    n1 = BATCH * LAT_TPB
    xl, h, info, counts = _out_proj(win, na, xs, l1_w_out.astype(BF16), mod1, row(l1_norm_ffn), *router,
                                    _lat_tile, n1)
    out = _moe_block(xl, h, info, counts, mod1, row(final_norm), l1_exp_gate, l1_exp_up, l1_exp_down,
                     lambda i: (i, i // LAT_TPB), n1, True)
    return out.reshape(BATCH, SEQ, D_MODEL)
```

```python
import functools

import jax
import jax.numpy as jnp
from jax import lax
from jax.experimental import pallas as pl
from jax.experimental.pallas import tpu as pltpu

F32 = jnp.float32
BF16 = jnp.bfloat16

D_MODEL = 2048
BATCH = 4
SEQ = 2048
GRID_W = 64
CTX_LEN = 256
EPS = 1e-6
NEG_INF = -1e30
ROPE_THETA = 10000.0
N_MOD = 6

MLA_HEADS = 8
MLA_Q_RANK = 512
MLA_KV_RANK = 256
MLA_NOPE = 128
MLA_ROPE = 64
MLA_V = 128

LRU_WIDTH = 1024
LRU_BLOCKS = 8
LRU_C = 8.0

HEAD_DIM = 128
WIN_HEADS = 8
WIN_KV_HEADS = 2
WINDOW = 128
NA_HEADS = 8
NA_ROWS = 8
NA_COLS = 16

N_EXPERTS = 32
N_GROUPS = 8
EXPERTS_PER_GROUP = 4
EXPERT_FF = 512

LANES = 128
SUBLANES = 8
VMEM_LIMIT = 56 * 1024 * 1024

NB = CTX_LEN + SEQ
TOK = BATCH * NB
TM = 256
TPB = NB // TM
LAT_TPB = SEQ // TM
TMX = 256
MLA_QK = 2 * LANES
LOG2E = 1.4426950408889634


def _cparams(sem):
    return pltpu.CompilerParams(dimension_semantics=sem, vmem_limit_bytes=VMEM_LIMIT)


def _resident(shape):
    nd = len(shape)
    return pl.BlockSpec(shape, lambda *_: (0,) * nd, pipeline_mode=pl.Buffered(1))


def _rms(x, g):
    return x * lax.rsqrt(jnp.mean(x * x, axis=-1, keepdims=True) + EPS) * g


def _sigmoid(x):
    return 0.5 * jnp.tanh(0.5 * x) + 0.5


def _dot(a, b):
    return jnp.dot(a, b, preferred_element_type=F32)


def _dot_nt(a, b):
    return lax.dot_general(a, b, (((1,), (1,)), ((), ())), preferred_element_type=F32)


def _swap_blocks(x, blk):
    lane = lax.broadcasted_iota(jnp.int32, x.shape, 1)
    nxt = pltpu.roll(x, LANES - blk, axis=1)
    prv = pltpu.roll(x, blk, axis=1)
    return jnp.where((lane % (2 * blk)) < blk, nxt, prv)


def _rope(x, cos, sin, blk):
    return x * cos + _swap_blocks(x, blk) * sin


def _comb_tile(i):
    b = i // TPB
    return i, jnp.where(i % TPB == 0, BATCH, b)


def _lat_tile(i):
    b = i // LAT_TPB
    return b * TPB + 1 + i % LAT_TPB, b


def _mod_spec(tile_fn, k):
    return pl.BlockSpec((1, 1, D_MODEL), lambda i, *_: (tile_fn(i)[1] * N_MOD + k, 0, 0))


def _mod_kernel(c_ref, w_ref, b_ref, o_ref):
    c = c_ref[...]
    a = (c * jax.nn.sigmoid(c)).astype(BF16)
    o_ref[...] = _dot(a, w_ref[...].astype(BF16)) + b_ref[...]


def _modulation(cs, w, b):
    n = N_MOD * D_MODEL
    tn = 1024
    out = pl.pallas_call(
        _mod_kernel,
        grid=(n // tn,),
        in_specs=[pl.BlockSpec((SUBLANES, D_MODEL), lambda j: (0, 0)),
                  pl.BlockSpec((D_MODEL, tn), lambda j: (0, j)),
                  pl.BlockSpec((1, tn), lambda j: (0, j))],
        out_specs=pl.BlockSpec((SUBLANES, tn), lambda j: (0, j)),
        out_shape=jax.ShapeDtypeStruct((SUBLANES, n), F32),
        compiler_params=_cparams(("arbitrary",)),
        name="modulation",
    )(cs, w, b.reshape(1, n))
    return out.reshape(SUBLANES * N_MOD, 1, D_MODEL)


L0_CQKV = MLA_Q_RANK + MLA_KV_RANK + LANES
L0_IN_PAD = L0_CQKV + 2 * LRU_WIDTH


def _in0_kernel(x_ref, g_ref, sh_ref, sc_ref, w_ref, cqkv_ref, xr_ref, gr_ref):
    h = _rms(x_ref[...], g_ref[...]) * (1.0 + sc_ref[0]) + sh_ref[0]
    y = _dot(h.astype(BF16), w_ref[...])
    cqkv_ref[...] = y[:, :L0_CQKV]
    xr_ref[...] = y[:, L0_CQKV:L0_CQKV + LRU_WIDTH]
    gr_ref[...] = y[:, L0_CQKV + LRU_WIDTH:]


def _in0(x, g, mod, w):
    row = lambda i: (i, 0)
    return pl.pallas_call(
        _in0_kernel,
        grid=(TOK // TM,),
        in_specs=[pl.BlockSpec((TM, D_MODEL), row), _resident((1, D_MODEL)),
                  _mod_spec(_comb_tile, 0), _mod_spec(_comb_tile, 1), _resident((D_MODEL, L0_IN_PAD))],
        out_specs=[pl.BlockSpec((TM, L0_CQKV), row), pl.BlockSpec((TM, LRU_WIDTH), row),
                   pl.BlockSpec((TM, LRU_WIDTH), row)],
        out_shape=[jax.ShapeDtypeStruct((TOK, L0_CQKV), F32), jax.ShapeDtypeStruct((TOK, LRU_WIDTH), F32),
                   jax.ShapeDtypeStruct((TOK, LRU_WIDTH), F32)],
        compiler_params=_cparams(("parallel",)),
        name="l0_in_proj",
    )(x, g, mod, mod, w)


def _mla_proj_kernel(c_ref, qn_ref, kvn_ref, wq_ref, wk_ref, wv_ref, cos_ref, sin_ref, q_ref, k_ref, v_ref):
    c = c_ref[...]
    cos = cos_ref[...]
    sin = sin_ref[...]
    nq = _rms(c[:, :MLA_Q_RANK], qn_ref[...]).astype(BF16)
    q = _dot(nq, wq_ref[...]) * (LOG2E * (MLA_NOPE + MLA_ROPE) ** -0.5)
    nkv = _rms(c[:, MLA_Q_RANK:MLA_Q_RANK + MLA_KV_RANK], kvn_ref[...]).astype(BF16)
    kn = _dot(nkv, wk_ref[...])
    v_ref[...] = _dot(nkv, wv_ref[...]).astype(BF16)
    kr = _rope(c[:, MLA_Q_RANK + MLA_KV_RANK:], cos, sin, MLA_ROPE // 4).astype(BF16)
    for h in range(MLA_HEADS):
        lo = h * MLA_QK
        q_ref[:, lo:lo + LANES] = q[:, lo:lo + LANES].astype(BF16)
        q_ref[:, lo + LANES:lo + MLA_QK] = _rope(q[:, lo + LANES:lo + MLA_QK], cos, sin, MLA_ROPE // 4).astype(BF16)
        k_ref[:, lo:lo + LANES] = kn[:, h * LANES:(h + 1) * LANES].astype(BF16)
        k_ref[:, lo + LANES:lo + MLA_QK] = kr


def _mla_proj(cqkv, qn, kvn, wq, wk, wv, cos, sin):
    row = lambda i: (i, 0)
    pos = lambda i: (i % TPB, 0)
    hq = MLA_HEADS * MLA_QK
    hv = MLA_HEADS * MLA_V
    return pl.pallas_call(
        _mla_proj_kernel,
        grid=(TOK // TM,),
        in_specs=[pl.BlockSpec((TM, L0_CQKV), row), _resident((1, MLA_Q_RANK)), _resident((1, MLA_KV_RANK)),
                  _resident((MLA_Q_RANK, hq)), _resident((MLA_KV_RANK, hv)), _resident((MLA_KV_RANK, hv)),
                  pl.BlockSpec((TM, LANES), pos), pl.BlockSpec((TM, LANES), pos)],
        out_specs=[pl.BlockSpec((TM, hq), row), pl.BlockSpec((TM, hq), row), pl.BlockSpec((TM, hv), row)],
        out_shape=[jax.ShapeDtypeStruct((TOK, hq), BF16), jax.ShapeDtypeStruct((TOK, hq), BF16),
                   jax.ShapeDtypeStruct((TOK, hv), BF16)],
        compiler_params=_cparams(("parallel",)),
        name="mla_proj",
    )(cqkv, qn, kvn, wq, wk, wv, cos, sin)


MLA_TQ = 256


def _softmax_pv(s, v):
    m = jnp.max(s, axis=-1, keepdims=True)
    p = jnp.exp2(s - m)
    l = jnp.sum(p, axis=-1, keepdims=True)
    return _dot(p.astype(BF16), v) / l


def _mla_attn_kernel(q_ref, k_ref, v_ref, o_ref):
    s = _dot_nt(q_ref[0:CTX_LEN, :], k_ref[0:CTX_LEN, :])
    o_ref[0:CTX_LEN, :] = _softmax_pv(s, v_ref[0:CTX_LEN, :]).astype(o_ref.dtype)

    def body(i, carry):
        for t in range(2):
            r0 = pl.multiple_of(CTX_LEN + (2 * i + t) * MLA_TQ, MLA_TQ)
            s = _dot_nt(q_ref[pl.ds(r0, MLA_TQ), :], k_ref[...])
            o_ref[pl.ds(r0, MLA_TQ), :] = _softmax_pv(s, v_ref[...]).astype(o_ref.dtype)
        return carry

    lax.fori_loop(0, SEQ // (2 * MLA_TQ), body, 0)


def _mla_attn(q, k, v):
    blk = lambda b, h: (b, h)
    return pl.pallas_call(
        _mla_attn_kernel,
        grid=(BATCH, MLA_HEADS),
        in_specs=[pl.BlockSpec((NB, MLA_QK), blk), pl.BlockSpec((NB, MLA_QK), blk), pl.BlockSpec((NB, MLA_V), blk)],
        out_specs=pl.BlockSpec((NB, MLA_V), blk),
        out_shape=jax.ShapeDtypeStruct((TOK, MLA_HEADS * MLA_V), BF16),
        compiler_params=_cparams(("parallel", "parallel")),
        name="mla_attn",
    )(q, k, v)


LRU_BW = LRU_WIDTH // LRU_BLOCKS
CTX_GROUPS = CTX_LEN // SUBLANES
LAT_GROUPS = SEQ // SUBLANES


def _scan_group(a, b, reverse):
    row = lax.broadcasted_iota(jnp.int32, a.shape, 0)
    for d in (1, 2, 4):
        shift = SUBLANES - d if reverse else d
        a_s = pltpu.roll(a, shift, axis=0)
        b_s = pltpu.roll(b, shift, axis=0)
        m = (row < SUBLANES - d) if reverse else (row >= d)
        b = jnp.where(m, a * b_s + b, b)
        a = jnp.where(m, a * a_s, a)
    return a, b


def _rglru_kernel(xr_ref, gr_ref, cw_ref, cb_ref, wa_ref, ba_ref, wx_ref, bx_ref, lam_ref, y_ref,
                  af_ref, bf_ref, ab_ref, bb_ref, hf_ref, hb_ref):
    row8 = lax.broadcasted_iota(jnp.int32, (SUBLANES, LRU_BW), 0)

    def taps(seg):
        n = seg.shape[0]

        def shifted(shift, keep, first):
            r = pltpu.roll(seg, shift % n, axis=0)
            if first:
                return jnp.concatenate([jnp.where(keep, r[:SUBLANES], 0.0), r[SUBLANES:]], axis=0)
            return jnp.concatenate([r[:-SUBLANES], jnp.where(keep, r[-SUBLANES:], 0.0)], axis=0)

        return shifted(2, row8 >= 2, True), shifted(1, row8 >= 1, True), shifted(-1, row8 < SUBLANES - 1, False)

    x = xr_ref[...]
    tc = taps(x[:CTX_LEN])
    tl = taps(x[CTX_LEN:])
    xm2, xm1, xp1 = [jnp.concatenate([a, b], axis=0) for a, b in zip(tc, tl)]
    u = cb_ref[...] + xm2 * cw_ref[0:1, :] + xm1 * cw_ref[1:2, :] + x * cw_ref[2:3, :] + xp1 * cw_ref[3:4, :]
    ub = u.astype(BF16)
    for d, (a_ref, b_ref) in enumerate(((af_ref, bf_ref), (ab_ref, bb_ref))):
        r = _sigmoid(_dot(ub, wa_ref[d, 0].astype(BF16)) + ba_ref[d:d + 1, :])
        ig = _sigmoid(_dot(ub, wx_ref[d, 0].astype(BF16)) + bx_ref[d:d + 1, :])
        z = -lam_ref[d:d + 1, :]
        softplus = jnp.maximum(z, 0.0) + jnp.log(1.0 + jnp.exp(-jnp.abs(z)))
        log_a = -LRU_C * r * softplus
        a = jnp.exp(log_a)
        a_ref[...] = a
        t = 1.0 - a * a
        b_ref[...] = jnp.where(t > 0.0, t * lax.rsqrt(t), 0.0) * (ig * u)

    def step(gf, gb, hf, hb):
        rf = pl.multiple_of(gf * SUBLANES, SUBLANES)
        a, b = _scan_group(af_ref[pl.ds(rf, SUBLANES), :], bf_ref[pl.ds(rf, SUBLANES), :], False)
        h = a * hf + b
        hf_ref[pl.ds(rf, SUBLANES), :] = h
        hf = jnp.broadcast_to(h[SUBLANES - 1:SUBLANES, :], h.shape)
        rb = pl.multiple_of(gb * SUBLANES, SUBLANES)
        a, b = _scan_group(ab_ref[pl.ds(rb, SUBLANES), :], bb_ref[pl.ds(rb, SUBLANES), :], True)
        h = a * hb + b
        hb_ref[pl.ds(rb, SUBLANES), :] = h
        hb = jnp.broadcast_to(h[0:1, :], h.shape)
        return hf, hb

    zero = jnp.zeros((SUBLANES, LRU_BW), F32)
    carry = lax.fori_loop(0, CTX_GROUPS, lambda i, c: step(i, CTX_GROUPS - 1 - i, *c), (zero, zero), unroll=2)
    lax.fori_loop(0, LAT_GROUPS, lambda i, c: step(CTX_GROUPS + i, CTX_GROUPS + LAT_GROUPS - 1 - i, *c), carry, unroll=2)
    y_ref[...] = ((hf_ref[...] + hb_ref[...]) * jax.nn.gelu(gr_ref[...])).astype(y_ref.dtype)


def _rglru(xr, gr, conv_w, conv_b, wa, ba, wx, bx, lam):
    blk = lambda b, n: (b, n)
    col = lambda b, n: (0, n)
    gate = lambda b, n: (0, n, 0, 0)
    seg = pltpu.VMEM((NB, LRU_BW), F32)
    return pl.pallas_call(
        _rglru_kernel,
        grid=(BATCH, LRU_BLOCKS),
        in_specs=[pl.BlockSpec((NB, LRU_BW), blk), pl.BlockSpec((NB, LRU_BW), blk),
                  pl.BlockSpec((4, LRU_BW), col), pl.BlockSpec((1, LRU_BW), col),
                  pl.BlockSpec((2, 1, LRU_BW, LRU_BW), gate), pl.BlockSpec((2, LRU_BW), col),
                  pl.BlockSpec((2, 1, LRU_BW, LRU_BW), gate), pl.BlockSpec((2, LRU_BW), col),
                  pl.BlockSpec((2, LRU_BW), col)],
        out_specs=pl.BlockSpec((NB, LRU_BW), blk),
        out_shape=jax.ShapeDtypeStruct((TOK, LRU_WIDTH), BF16),
        scratch_shapes=[seg, seg, seg, seg, seg, seg],
        compiler_params=_cparams(("parallel", "parallel")),
        name="rglru",
    )(xr, gr, conv_w, conv_b.reshape(1, LRU_WIDTH), wa, ba, wx, bx, lam)


L1_Q = WIN_HEADS * HEAD_DIM
L1_KV = WIN_KV_HEADS * HEAD_DIM
L1_NA = NA_HEADS * HEAD_DIM
L1_IN = L1_Q + 2 * L1_KV + 3 * L1_NA


def _in1_kernel(x_ref, g_ref, sh_ref, sc_ref, w_ref, cos_ref, sin_ref,
                qw_ref, kw_ref, vw_ref, qn_ref, kn_ref, vn_ref):
    h = _rms(x_ref[...], g_ref[...]) * (1.0 + sc_ref[0]) + sh_ref[0]
    y = _dot(h.astype(BF16), w_ref[...])
    cos = cos_ref[...]
    sin = sin_ref[...]
    scale = LOG2E * HEAD_DIM ** -0.5
    for hd in range(WIN_HEADS):
        lo = hd * HEAD_DIM
        qw_ref[:, lo:lo + HEAD_DIM] = _rope(y[:, lo:lo + HEAD_DIM] * scale, cos, sin, HEAD_DIM // 4).astype(BF16)
    for hd in range(WIN_KV_HEADS):
        lo = hd * HEAD_DIM
        kw_ref[:, lo:lo + HEAD_DIM] = _rope(y[:, L1_Q + lo:L1_Q + lo + HEAD_DIM], cos, sin, HEAD_DIM // 4).astype(BF16)
    o = L1_Q + L1_KV
    vw_ref[...] = y[:, o:o + L1_KV].astype(BF16)
    o += L1_KV
    qn_ref[...] = (y[:, o:o + L1_NA] * scale).astype(BF16)
    kn_ref[...] = y[:, o + L1_NA:o + 2 * L1_NA].astype(BF16)
    vn_ref[...] = y[:, o + 2 * L1_NA:].astype(BF16)


def _in1(x, g, mod, w, cos, sin):
    row = lambda i: (i, 0)
    pos = lambda i: (i % TPB, 0)
    widths = (L1_Q, L1_KV, L1_KV, L1_NA, L1_NA, L1_NA)
    return pl.pallas_call(
        _in1_kernel,
        grid=(TOK // TM,),
        in_specs=[pl.BlockSpec((TM, D_MODEL), row), _resident((1, D_MODEL)),
                  _mod_spec(_comb_tile, 0), _mod_spec(_comb_tile, 1), _resident((D_MODEL, L1_IN)),
                  pl.BlockSpec((TM, LANES), pos), pl.BlockSpec((TM, LANES), pos)],
        out_specs=[pl.BlockSpec((TM, n), row) for n in widths],
        out_shape=[jax.ShapeDtypeStruct((TOK, n), BF16) for n in widths],
        compiler_params=_cparams(("parallel",)),
        name="l1_in_proj",
    )(x, g, mod, mod, w, cos, sin)


WIN_TQ = 128
WIN_SPAN = WIN_TQ + 2 * WINDOW
WIN_G = WIN_HEADS // WIN_KV_HEADS


def _win_kernel(sink_ref, q_ref, k_ref, v_ref, o_ref):
    hk = pl.program_id(1)
    o_ref[0:CTX_LEN, :] = jnp.zeros((CTX_LEN, WIN_G * HEAD_DIM), o_ref.dtype)
    rows = WIN_G * WIN_TQ
    head = lax.broadcasted_iota(jnp.int32, (rows, 1), 0) // WIN_TQ
    sink = jnp.zeros((rows, 1), F32)
    for g in range(WIN_G):
        sink = jnp.where(head == g, sink_ref[hk * WIN_G + g] * LOG2E, sink)
    qoff = lax.broadcasted_iota(jnp.int32, (rows, WIN_SPAN), 0) % WIN_TQ
    koff = lax.broadcasted_iota(jnp.int32, (rows, WIN_SPAN), 1)

    def body(n, carry):
        r0 = pl.multiple_of(CTX_LEN + n * WIN_TQ, WIN_TQ)
        start = jnp.clip((n - 1) * WIN_TQ, 0, SEQ - WIN_SPAN)
        ks = pl.multiple_of(CTX_LEN + start, WIN_TQ)
        q4 = q_ref[pl.ds(r0, WIN_TQ), :]
        q = jnp.concatenate([q4[:, g * HEAD_DIM:(g + 1) * HEAD_DIM] for g in range(WIN_G)], axis=0)
        s_c = _dot_nt(q, k_ref[0:CTX_LEN, :])
        s_w = _dot_nt(q, k_ref[pl.ds(ks, WIN_SPAN), :])
        valid = jnp.abs(n * WIN_TQ + qoff - (start + koff)) <= WINDOW
        s_w = jnp.where(valid, s_w, NEG_INF)
        m = jnp.maximum(jnp.maximum(jnp.max(s_c, axis=-1, keepdims=True), jnp.max(s_w, axis=-1, keepdims=True)), sink)
        p_c = jnp.exp2(s_c - m)
        p_w = jnp.exp2(s_w - m)
        l = jnp.sum(p_c, axis=-1, keepdims=True) + jnp.sum(p_w, axis=-1, keepdims=True) + jnp.exp2(sink - m)
        o = (_dot(p_w.astype(BF16), v_ref[pl.ds(ks, WIN_SPAN), :]) + _dot(p_c.astype(BF16), v_ref[0:CTX_LEN, :])) / l
        for g in range(WIN_G):
            o_ref[pl.ds(r0, WIN_TQ), g * HEAD_DIM:(g + 1) * HEAD_DIM] = o[g * WIN_TQ:(g + 1) * WIN_TQ].astype(o_ref.dtype)
        return carry

    lax.fori_loop(0, SEQ // WIN_TQ, body, 0)


def _win_attn(sink, q, k, v):
    blk = lambda b, h, *_: (b, h)
    return pl.pallas_call(
        _win_kernel,
        grid_spec=pltpu.PrefetchScalarGridSpec(
            num_scalar_prefetch=1,
            grid=(BATCH, WIN_KV_HEADS),
            in_specs=[pl.BlockSpec((NB, WIN_G * HEAD_DIM), blk), pl.BlockSpec((NB, HEAD_DIM), blk),
                      pl.BlockSpec((NB, HEAD_DIM), blk)],
            out_specs=pl.BlockSpec((NB, WIN_G * HEAD_DIM), blk)),
        out_shape=jax.ShapeDtypeStruct((TOK, L1_Q), BF16),
        compiler_params=_cparams(("parallel", "parallel")),
        name="window_attn",
    )(sink, q, k, v)


NA_GRID_ROWS = SEQ // GRID_W
NA_BAND = NA_ROWS * GRID_W


NA_RPI = 4


def _na_kernel(q_ref, k_ref, v_ref, bias_ref, o_ref):
    o_ref[0:CTX_LEN, :] = jnp.zeros((CTX_LEN, HEAD_DIM), o_ref.dtype)

    def body(i, carry):
        qs = pl.multiple_of(CTX_LEN + i * (NA_RPI * GRID_W), NA_RPI * GRID_W)
        q = q_ref[pl.ds(qs, NA_RPI * GRID_W), :]
        s_c = _dot_nt(q, k_ref[0:CTX_LEN, :])
        starts = []
        s_w = []
        for j in range(NA_RPI):
            r = i * NA_RPI + j
            r0 = jnp.clip(r - NA_ROWS // 2, 0, NA_GRID_ROWS - NA_ROWS)
            ks = pl.multiple_of(CTX_LEN + r0 * GRID_W, GRID_W)
            starts.append(ks)
            s_w.append(_dot_nt(q[j * GRID_W:(j + 1) * GRID_W], k_ref[pl.ds(ks, NA_BAND), :])
                       + bias_ref[0, r0 - r + NA_ROWS - 1])
        s_w = jnp.concatenate(s_w, axis=0)
        m = jnp.maximum(jnp.max(s_c, axis=-1, keepdims=True), jnp.max(s_w, axis=-1, keepdims=True))
        p_c = jnp.exp2(s_c - m)
        p_w = jnp.exp2(s_w - m)
        l = jnp.sum(p_c, axis=-1, keepdims=True) + jnp.sum(p_w, axis=-1, keepdims=True)
        p_w = p_w.astype(BF16)
        o_w = jnp.concatenate([_dot(p_w[j * GRID_W:(j + 1) * GRID_W], v_ref[pl.ds(starts[j], NA_BAND), :])
                               for j in range(NA_RPI)], axis=0)
        o = (o_w + _dot(p_c.astype(BF16), v_ref[0:CTX_LEN, :])) / l
        o_ref[pl.ds(qs, NA_RPI * GRID_W), :] = o.astype(o_ref.dtype)
        return carry

    lax.fori_loop(0, NA_GRID_ROWS // NA_RPI, body, 0)


def _na_attn(q, k, v, bias):
    blk = lambda b, h: (b, h)
    return pl.pallas_call(
        _na_kernel,
        grid=(BATCH, NA_HEADS),
        in_specs=[pl.BlockSpec((NB, HEAD_DIM), blk), pl.BlockSpec((NB, HEAD_DIM), blk),
                  pl.BlockSpec((NB, HEAD_DIM), blk),
                  pl.BlockSpec((1, NA_ROWS, GRID_W, NA_BAND), lambda b, h: (h, 0, 0, 0))],
        out_specs=pl.BlockSpec((NB, HEAD_DIM), blk),
        out_shape=jax.ShapeDtypeStruct((TOK, L1_NA), BF16),
        compiler_params=_cparams(("parallel", "parallel")),
        name="na_attn",
    )(q, k, v, bias)


def _na_bias_table(rpb):
    cols = jnp.arange(GRID_W)
    c0 = jnp.clip(cols - NA_COLS // 2, 0, GRID_W - NA_COLS)
    kc = cols[None, :]
    valid = (kc >= c0[:, None]) & (kc < c0[:, None] + NA_COLS)
    lo = GRID_W - NA_COLS
    ext = jnp.pad(rpb.astype(F32) * LOG2E, ((0, 0), (0, 0), (lo, lo)))
    tbl = jnp.stack([ext[:, :, GRID_W - 1 - c:2 * GRID_W - 1 - c] for c in range(GRID_W)], axis=2)
    tbl = jnp.where(valid[None, None], tbl, NEG_INF)
    tbl = jnp.stack([tbl[:, d:d + NA_ROWS] for d in range(NA_ROWS)], axis=1)
    return jnp.transpose(tbl, (0, 1, 3, 2, 4)).reshape(NA_HEADS, NA_ROWS, GRID_W, NA_BAND)


def _router(h, rwh_ref, rwl_ref, rb_ref, tri_ref, carry_ref, info_ref, counts_ref):
    tm = h.shape[0]
    hh = h.astype(BF16)
    hl = (h - hh.astype(F32)).astype(BF16)
    logits = _dot_nt(rwh_ref[...], hh) + (_dot_nt(rwh_ref[...], hl) + _dot_nt(rwl_ref[...], hh))
    scores = jax.nn.sigmoid(logits)
    biased = scores + rb_ref[...]
    nj = EXPERTS_PER_GROUP
    s = [biased[j * N_GROUPS:(j + 1) * N_GROUPS] for j in range(nj)]
    u = [scores[j * N_GROUPS:(j + 1) * N_GROUPS] for j in range(nj)]
    gs = None
    for a in range(nj):
        for b in range(a + 1, nj):
            pair = s[a] + s[b]
            gs = pair if gs is None else jnp.maximum(gs, pair)
    giota = lax.broadcasted_iota(jnp.int32, (N_GROUPS, tm), 0).astype(F32)
    gmax = jnp.max(gs, axis=0, keepdims=True)
    gidx = jnp.min(jnp.where(gs == gmax, giota, float(N_GROUPS)), axis=0, keepdims=True)
    gm = giota == gidx
    v = [jnp.sum(jnp.where(gm, s[j], 0.0), axis=0, keepdims=True) for j in range(nj)]
    w = [jnp.sum(jnp.where(gm, u[j], 0.0), axis=0, keepdims=True) for j in range(nj)]
    sel = []
    for j in range(nj):
        beaten = jnp.zeros((1, tm), F32)
        for i in range(nj):
            if i != j:
                ahead = (v[i] >= v[j]) if i < j else (v[i] > v[j])
                beaten = beaten + jnp.where(ahead, 1.0, 0.0)
        sel.append(beaten < 2.0)
    wsum = sum(jnp.where(sel[j], w[j], 0.0) for j in range(nj))
    first = functools.reduce(jnp.minimum, [jnp.where(sel[j], float(j), float(nj)) for j in range(nj)])
    last = functools.reduce(jnp.maximum, [jnp.where(sel[j], float(j), -1.0) for j in range(nj)])
    gmf = jnp.where(gm, 1.0, 0.0)
    cnt = jnp.concatenate([jnp.where(sel[j], gmf, 0.0) for j in range(nj)], axis=0)
    pos = _dot(cnt.astype(BF16), tri_ref[...]) + carry_ref[:, 0:1]
    rank = [jnp.sum(cnt[j * N_GROUPS:(j + 1) * N_GROUPS] * pos[j * N_GROUPS:(j + 1) * N_GROUPS], axis=0, keepdims=True)
            for j in range(nj)]
    pick = lambda which, vals: sum(jnp.where(which == float(j), vals[j], 0.0) for j in range(nj))
    gate = [w[j] / wsum for j in range(nj)]
    zero = jnp.zeros((1, tm), F32)
    info_ref[...] = jnp.concatenate(
        [gidx * nj + first, gidx * nj + last, pick(first, rank), pick(last, rank), pick(first, gate), pick(last, gate),
         zero, zero], axis=0)
    new = carry_ref[...] + jnp.sum(cnt, axis=1, keepdims=True)
    carry_ref[...] = new
    counts_ref[...] = new


HALF = D_MODEL // 2


def _pack_bf16(x):
    r = x.astype(BF16).astype(F32)
    hi = pltpu.bitcast(r[:, :HALF], jnp.uint32)
    lo = pltpu.bitcast(r[:, HALF:], jnp.uint32)
    return hi | (lo >> 16)


def _unpack_bf16(p):
    return pltpu.bitcast(p & jnp.uint32(0xFFFF0000), F32), pltpu.bitcast(p << 16, F32)


def _out_kernel(ya_ref, yb_ref, x_ref, wa_ref, wb_ref, g1_ref, n_ref, sh_ref, sc_ref,
                rwh_ref, rwl_ref, rb_ref, tri_ref, xo_ref, hp_ref, info_ref, counts_ref, carry_ref):
    @pl.when(pl.program_id(0) == 0)
    def _():
        carry_ref[...] = jnp.zeros_like(carry_ref)

    y = _dot(ya_ref[...], wa_ref[...]) + _dot(yb_ref[...], wb_ref[...])
    x = x_ref[...] + g1_ref[0] * y
    xo_ref[...] = x
    h = _rms(x, n_ref[...]) * (1.0 + sc_ref[0]) + sh_ref[0]
    hp_ref[...] = _pack_bf16(h)
    _router(h, rwh_ref, rwl_ref, rb_ref, tri_ref, carry_ref, info_ref, counts_ref)


def _out_proj(ya, yb, x, w_out, mod, norm, rwh, rwl, rb, tri, tile_fn, n_tiles):
    half = w_out.shape[0] // 2
    src = lambda i: (tile_fn(i)[0], 0)
    dst = lambda i: (i, 0)
    n_tok = n_tiles * TM
    return pl.pallas_call(
        _out_kernel,
        grid=(n_tiles,),
        in_specs=[pl.BlockSpec((TM, half), src), pl.BlockSpec((TM, half), src), pl.BlockSpec((TM, D_MODEL), src),
                  _resident((half, D_MODEL)), _resident((half, D_MODEL)),
                  _mod_spec(tile_fn, 2), _resident((1, D_MODEL)), _mod_spec(tile_fn, 3), _mod_spec(tile_fn, 4),
                  _resident((N_EXPERTS, D_MODEL)), _resident((N_EXPERTS, D_MODEL)), _resident((N_EXPERTS, 1)),
                  _resident((TM, TM))],
        out_specs=[pl.BlockSpec((TM, D_MODEL), dst), pl.BlockSpec((TM, HALF), dst),
                   pl.BlockSpec((SUBLANES, TM), lambda i: (0, i)),
                   pl.BlockSpec((N_EXPERTS, LANES), lambda i: (0, 0))],
        out_shape=[jax.ShapeDtypeStruct((n_tok, D_MODEL), F32), jax.ShapeDtypeStruct((n_tok, HALF), jnp.uint32),
                   jax.ShapeDtypeStruct((SUBLANES, n_tok), F32), jax.ShapeDtypeStruct((N_EXPERTS, LANES), F32)],
        scratch_shapes=[pltpu.VMEM((N_EXPERTS, LANES), F32)],
        compiler_params=_cparams(("arbitrary",)),
        name="out_proj_router",
    )(ya, yb, x, w_out[:half], w_out[half:], mod, norm, mod, mod, rwh, rwl, rb, tri)


def _lookup(table, idx):
    onehot = idx[..., None] == jnp.arange(table.shape[0], dtype=jnp.int32)
    return jnp.sum(jnp.where(onehot, table, 0), axis=-1)


def _dispatch_plan(info, counts, n_tok):
    n_tiles = 2 * n_tok // TMX + N_EXPERTS
    experts = jnp.arange(N_EXPERTS, dtype=jnp.int32)
    cnt = counts[:, 0].astype(jnp.int32).reshape(EXPERTS_PER_GROUP, N_GROUPS).T.reshape(N_EXPERTS)
    ntile = (cnt + TMX - 1) // TMX
    tile_end = jnp.cumsum(ntile)
    tile_start = tile_end - ntile
    n_valid = tile_end[-1]
    dest = _lookup(tile_start * TMX, info[0:2].astype(jnp.int32)) + info[2:4].astype(jnp.int32)
    tile_id = jnp.minimum(jnp.arange(n_tiles, dtype=jnp.int32), n_valid - 1)
    tile_expert = jnp.sum(tile_end[None, :] <= tile_id[:, None], axis=1).astype(jnp.int32)
    first = tile_id == _lookup(tile_start, tile_expert)
    later = (experts[None, :] > experts[:, None]) & (ntile[None, :] > 0)
    nxt = jnp.min(jnp.where(later, experts[None, :], N_EXPERTS), axis=1)
    nxt = jnp.where(nxt == N_EXPERTS, -1, nxt)
    slot = (jnp.cumsum((ntile > 0).astype(jnp.int32)) - 1) % 2
    i32 = lambda v: v.astype(jnp.int32)
    return (dest.reshape(-1), i32(tile_expert), i32(n_valid.reshape(1)), i32(first),
            i32(_lookup(nxt, tile_expert)), i32(_lookup(slot, tile_expert)))


def _dispatch_kernel(dest_ref, hp_ref, zeros_hbm, xs_hbm, sem, *, n_tok):
    del zeros_hbm
    i = pl.program_id(0)

    def row_copy(r, d):
        return pltpu.make_async_copy(hp_ref.at[pl.ds(r, 1)], xs_hbm.at[pl.ds(d, 1)], sem)

    def start(r, c):
        for k in range(2):
            row_copy(r, dest_ref[k * n_tok + i * TM + r]).start()
        return c

    lax.fori_loop(0, TM, start, 0, unroll=8)

    def wait(r, c):
        for k in range(2):
            row_copy(r, 0).wait()
        return c

    lax.fori_loop(0, TM, wait, 0, unroll=8)


def _dispatch(dest, hp, n_rows):
    n_tok = hp.shape[0]
    return pl.pallas_call(
        functools.partial(_dispatch_kernel, n_tok=n_tok),
        grid_spec=pltpu.PrefetchScalarGridSpec(
            num_scalar_prefetch=1,
            grid=(n_tok // TM,),
            in_specs=[pl.BlockSpec((TM, HALF), lambda i, d: (i, 0)), pl.BlockSpec(memory_space=pl.ANY)],
            out_specs=pl.BlockSpec(memory_space=pl.ANY),
            scratch_shapes=[pltpu.SemaphoreType.DMA(())]),
        out_shape=jax.ShapeDtypeStruct((n_rows, HALF), jnp.uint32),
        input_output_aliases={2: 0},
        compiler_params=_cparams(("arbitrary",)),
        name="moe_dispatch",
    )(dest, hp, jnp.zeros((n_rows, HALF), jnp.uint32))


def _moe_kernel(te_ref, nv_ref, first_ref, nxt_ref, slot_ref, xs_ref, wg_hbm, wu_hbm, wd_hbm, ys_ref,
                wg_f, wu_f, wd_f, wg_s, wu_s, wd_s, sem):
    g = pl.program_id(0)

    def fetch(e, slot):
        return [pltpu.make_async_copy(src.at[e], dst.at[slot], sem.at[slot])
                for src, dst in ((wg_hbm, wg_f), (wu_hbm, wu_f), (wd_hbm, wd_f))]

    @pl.when(g < nv_ref[0])
    def _():
        @pl.when(first_ref[g] == 1)
        def _():
            slot = slot_ref[g]

            @pl.when(g == 0)
            def _():
                for c in fetch(te_ref[0], slot):
                    c.start()

            for c in fetch(te_ref[g], slot):
                c.wait()

            @pl.when(nxt_ref[g] >= 0)
            def _():
                for c in fetch(nxt_ref[g], 1 - slot):
                    c.start()

            wg_s[...] = wg_f[slot].astype(BF16)
            wu_s[...] = wu_f[slot].astype(BF16)
            wd_s[...] = wd_f[slot].astype(BF16)

        a, b = [v.astype(BF16) for v in _unpack_bf16(xs_ref[...])]
        hg = _dot(a, wg_s[0:HALF, :]) + _dot(b, wg_s[HALF:, :])
        hu = _dot(a, wu_s[0:HALF, :]) + _dot(b, wu_s[HALF:, :])
        he = (hg * jax.nn.sigmoid(hg)) * hu
        ys_ref[...] = _pack_bf16(_dot(he.astype(BF16), wd_s[...]))

    @pl.when(g >= nv_ref[0])
    def _():
        ys_ref[...] = jnp.zeros_like(ys_ref)


def _moe(plan, xs, w_gate, w_up, w_down):
    tile_expert, n_valid, first, nxt, slot = plan
    n_tiles = tile_expert.shape[0]
    tile = lambda g, te, nv, *_: (jnp.minimum(g, nv[0] - 1), 0)
    any_spec = pl.BlockSpec(memory_space=pl.ANY)
    return pl.pallas_call(
        _moe_kernel,
        grid_spec=pltpu.PrefetchScalarGridSpec(
            num_scalar_prefetch=5,
            grid=(n_tiles,),
            in_specs=[pl.BlockSpec((TMX, HALF), tile), any_spec, any_spec, any_spec],
            out_specs=pl.BlockSpec((TMX, HALF), lambda g, *_: (g, 0)),
            scratch_shapes=[pltpu.VMEM((2, D_MODEL, EXPERT_FF), F32), pltpu.VMEM((2, D_MODEL, EXPERT_FF), F32),
                            pltpu.VMEM((2, EXPERT_FF, D_MODEL), F32),
                            pltpu.VMEM((D_MODEL, EXPERT_FF), BF16), pltpu.VMEM((D_MODEL, EXPERT_FF), BF16),
                            pltpu.VMEM((EXPERT_FF, D_MODEL), BF16), pltpu.SemaphoreType.DMA((2,))]),
        out_shape=jax.ShapeDtypeStruct((n_tiles * TMX, HALF), jnp.uint32),
        compiler_params=_cparams(("arbitrary",)),
        name="moe_experts",
    )(tile_expert, n_valid, first, nxt, slot, xs, w_gate, w_up, w_down)


def _combine_kernel(dest_ref, ys_hbm, x_ref, w_ref, g2_ref, n_ref, o_ref, y_buf, sem, *, n_tok, final):
    i = pl.program_id(0)

    def row_copy(k, r, d):
        return pltpu.make_async_copy(ys_hbm.at[pl.ds(d, 1)], y_buf.at[k, pl.ds(r, 1)], sem)

    def start(r, c):
        for k in range(2):
            row_copy(k, r, dest_ref[k * n_tok + i * TM + r]).start()
        return c

    lax.fori_loop(0, TM, start, 0, unroll=8)

    def wait(r, c):
        for k in range(2):
            row_copy(k, r, 0).wait()
        return c

    lax.fori_loop(0, TM, wait, 0, unroll=8)
    w = w_ref[...]
    a0, b0 = _unpack_bf16(y_buf[0])
    a1, b1 = _unpack_bf16(y_buf[1])
    y = jnp.concatenate([w[:, 4:5] * a0 + w[:, 5:6] * a1, w[:, 4:5] * b0 + w[:, 5:6] * b1], axis=1)
    x = x_ref[...] + g2_ref[0] * y
    o_ref[...] = _rms(x, n_ref[...]) if final else x


def _combine(dest, ys, x, winfo, mod, norm, tile_fn, n_tiles, final):
    n_tok = n_tiles * TM
    row = lambda i, d: (i, 0)
    return pl.pallas_call(
        functools.partial(_combine_kernel, n_tok=n_tok, final=final),
        grid_spec=pltpu.PrefetchScalarGridSpec(
            num_scalar_prefetch=1,
            grid=(n_tiles,),
            in_specs=[pl.BlockSpec(memory_space=pl.ANY), pl.BlockSpec((TM, D_MODEL), row),
                      pl.BlockSpec((TM, SUBLANES), row), _mod_spec(tile_fn, 5),
                      pl.BlockSpec((1, D_MODEL), lambda i, d: (0, 0))],
            out_specs=pl.BlockSpec((TM, D_MODEL), row),
            scratch_shapes=[pltpu.VMEM((2, TM, HALF), jnp.uint32), pltpu.SemaphoreType.DMA(())]),
        out_shape=jax.ShapeDtypeStruct((n_tok, D_MODEL), F32),
        compiler_params=_cparams(("arbitrary",)),
        name="moe_combine",
    )(dest, ys, x, winfo, mod, norm)


def _moe_block(x, hp, info, counts, mod, norm, w_gate, w_up, w_down, tile_fn, n_tiles, final):
    n_tok = n_tiles * TM
    dest, *plan = _dispatch_plan(info, counts, n_tok)
    xs = _dispatch(dest, hp, plan[0].shape[0] * TMX)
    ys = _moe(plan, xs, w_gate, w_up, w_down)
    return _combine(dest, ys, x, info.T, mod, norm, tile_fn, n_tiles, final)


def _rope_tables(dim):
    half, quarter = dim // 2, dim // 4
    t = jnp.arange(SEQ)
    pos = jnp.stack([t // GRID_W, t % GRID_W], axis=-1).astype(F32)
    inv_freq = ROPE_THETA ** (-jnp.arange(0, half, 2, dtype=F32) / half)
    ang = pos[:, :, None] * inv_freq
    cos = jnp.cos(ang)
    sin = jnp.sin(ang)
    cos = jnp.concatenate([cos[:, 0], cos[:, 0], cos[:, 1], cos[:, 1]], axis=-1)
    sin = jnp.concatenate([-sin[:, 0], sin[:, 0], -sin[:, 1], sin[:, 1]], axis=-1)
    pad = LANES - dim
    cos = jnp.pad(cos, ((0, 0), (0, pad)))
    sin = jnp.pad(sin, ((0, 0), (0, pad)))
    ctx_cos = jnp.pad(jnp.ones((CTX_LEN, dim), F32), ((0, 0), (0, pad)))
    return jnp.concatenate([ctx_cos, cos], axis=0), jnp.concatenate([jnp.zeros((CTX_LEN, LANES), F32), sin], axis=0)


def kernel(x, c, ctx, c_ctx, router_w, router_b, final_norm, l0_mod_w, l0_mod_b, l0_norm_mix, l0_norm_ffn, l0_w_in, l0_q_norm, l0_w_uq, l0_kv_norm, l0_w_ukv, l0_conv_w, l0_conv_b, l0_gate_a_w, l0_gate_a_b, l0_gate_x_w, l0_gate_x_b, l0_lru_lambda, l0_w_out, l0_exp_gate, l0_exp_up, l0_exp_down, l1_mod_w, l1_mod_b, l1_norm_mix, l1_norm_ffn, l1_w_in, l1_sink, l1_rpb, l1_w_out, l1_exp_gate, l1_exp_up, l1_exp_down):
    row = lambda v: v.reshape(1, -1)
    xs = jnp.concatenate([ctx, x], axis=1).reshape(TOK, D_MODEL)

    cs = jnp.concatenate([c, c_ctx[None], jnp.zeros((SUBLANES - BATCH - 1, D_MODEL), F32)], axis=0)
    mod0 = _modulation(cs, l0_mod_w, l0_mod_b)
    mod1 = _modulation(cs, l1_mod_w, l1_mod_b)

    perm = jnp.arange(N_EXPERTS).reshape(N_GROUPS, EXPERTS_PER_GROUP).T.reshape(-1)
    rwt = router_w.T[perm]
    rwh = rwt.astype(BF16)
    rwl = (rwt - rwh.astype(F32)).astype(BF16)
    rb = router_b[perm].reshape(N_EXPERTS, 1).astype(F32)
    tri = jnp.triu(jnp.ones((TM, TM), F32), k=1).astype(BF16)
    router = (rwh, rwl, rb, tri)

    split = MLA_Q_RANK + MLA_KV_RANK + MLA_ROPE
    w_in0 = jnp.concatenate([l0_w_in[:, :split], jnp.zeros((D_MODEL, LANES - MLA_ROPE), F32), l0_w_in[:, split:]],
                            axis=1).astype(BF16)
    cqkv, xr, gr = _in0(xs, row(l0_norm_mix), mod0, w_in0)
    wq = jnp.pad(l0_w_uq.reshape(MLA_Q_RANK, MLA_HEADS, MLA_NOPE + MLA_ROPE),
                 ((0, 0), (0, 0), (0, MLA_QK - MLA_NOPE - MLA_ROPE))).reshape(MLA_Q_RANK, MLA_HEADS * MLA_QK)
    wkv = l0_w_ukv.reshape(MLA_KV_RANK, MLA_HEADS, MLA_NOPE + MLA_V)
    wk = wkv[:, :, :MLA_NOPE].reshape(MLA_KV_RANK, MLA_HEADS * MLA_NOPE)
    wv = wkv[:, :, MLA_NOPE:].reshape(MLA_KV_RANK, MLA_HEADS * MLA_V)
    cos0, sin0 = _rope_tables(MLA_ROPE)
    q, k, v = _mla_proj(cqkv, row(l0_q_norm), row(l0_kv_norm), wq.astype(BF16), wk.astype(BF16), wv.astype(BF16),
                        cos0, sin0)
    att = _mla_attn(q, k, v)
    rnn = _rglru(xr, gr, l0_conv_w, l0_conv_b, l0_gate_a_w, l0_gate_a_b, l0_gate_x_w, l0_gate_x_b, l0_lru_lambda)
    n0 = TOK // TM
    xs, h, info, counts = _out_proj(att, rnn, xs, l0_w_out.astype(BF16), mod0, row(l0_norm_ffn), *router,
                                    _comb_tile, n0)
    xs = _moe_block(xs, h, info, counts, mod0, row(final_norm), l0_exp_gate, l0_exp_up, l0_exp_down,
                    _comb_tile, n0, False)

    cos1, sin1 = _rope_tables(HEAD_DIM)
    qw, kw, vw, qn, kn, vn = _in1(xs, row(l1_norm_mix), mod1, l1_w_in.astype(BF16), cos1, sin1)
    win = _win_attn(l1_sink.astype(F32), qw, kw, vw)
    na = _na_attn(qn, kn, vn, _na_bias_table(l1_rpb))
    n1 = BATCH * LAT_TPB
    xl, h, info, counts = _out_proj(win, na, xs, l1_w_out.astype(BF16), mod1, row(l1_norm_ffn), *router,
                                    _lat_tile, n1)
    out = _moe_block(xl, h, info, counts, mod1, row(final_norm), l1_exp_gate, l1_exp_up, l1_exp_down,
                     lambda i: (i, i // LAT_TPB), n1, True)
    return out.reshape(BATCH, SEQ, D_MODEL)
```

```python
import functools

import jax
import jax.numpy as jnp
from jax import lax
from jax.experimental import pallas as pl
from jax.experimental.pallas import tpu as pltpu

F32 = jnp.float32
BF16 = jnp.bfloat16

D_MODEL = 2048
BATCH = 4
SEQ = 2048
GRID_W = 64
CTX_LEN = 256
EPS = 1e-6
NEG_INF = -1e30
ROPE_THETA = 10000.0
N_MOD = 6

MLA_HEADS = 8
MLA_Q_RANK = 512
MLA_KV_RANK = 256
MLA_NOPE = 128
MLA_ROPE = 64
MLA_V = 128

LRU_WIDTH = 1024
LRU_BLOCKS = 8
LRU_C = 8.0

HEAD_DIM = 128
WIN_HEADS = 8
WIN_KV_HEADS = 2
WINDOW = 128
NA_HEADS = 8
NA_ROWS = 8
NA_COLS = 16

N_EXPERTS = 32
N_GROUPS = 8
EXPERTS_PER_GROUP = 4
EXPERT_FF = 512

LANES = 128
SUBLANES = 8
VMEM_LIMIT = 56 * 1024 * 1024

NB = CTX_LEN + SEQ
TOK = BATCH * NB
TM = 256
TPB = NB // TM
LAT_TPB = SEQ // TM
TMX = 256
MLA_QK = 2 * LANES
LOG2E = 1.4426950408889634


def _cparams(sem):
    return pltpu.CompilerParams(dimension_semantics=sem, vmem_limit_bytes=VMEM_LIMIT)


def _resident(shape):
    nd = len(shape)
    return pl.BlockSpec(shape, lambda *_: (0,) * nd, pipeline_mode=pl.Buffered(1))


def _rms(x, g):
    return x * lax.rsqrt(jnp.mean(x * x, axis=-1, keepdims=True) + EPS) * g


def _sigmoid(x):
    return 0.5 * jnp.tanh(0.5 * x) + 0.5


def _dot(a, b):
    return jnp.dot(a, b, preferred_element_type=F32)


def _dot_nt(a, b):
    return lax.dot_general(a, b, (((1,), (1,)), ((), ())), preferred_element_type=F32)


def _swap_blocks(x, blk):
    lane = lax.broadcasted_iota(jnp.int32, x.shape, 1)
    nxt = pltpu.roll(x, LANES - blk, axis=1)
    prv = pltpu.roll(x, blk, axis=1)
    return jnp.where((lane % (2 * blk)) < blk, nxt, prv)


def _rope(x, cos, sin, blk):
    return x * cos + _swap_blocks(x, blk) * sin


def _comb_tile(i):
    b = i // TPB
    return i, jnp.where(i % TPB == 0, BATCH, b)


def _lat_tile(i):
    b = i // LAT_TPB
    return b * TPB + 1 + i % LAT_TPB, b


def _mod_spec(tile_fn, k):
    return pl.BlockSpec((1, 1, D_MODEL), lambda i, *_: (tile_fn(i)[1] * N_MOD + k, 0, 0))


def _mod_kernel(c_ref, w_ref, b_ref, o_ref):
    c = c_ref[...]
    a = (c * jax.nn.sigmoid(c)).astype(BF16)
    o_ref[...] = _dot(a, w_ref[...].astype(BF16)) + b_ref[...]


def _modulation(cs, w, b):
    n = N_MOD * D_MODEL
    tn = 1024
    out = pl.pallas_call(
        _mod_kernel,
        grid=(n // tn,),
        in_specs=[pl.BlockSpec((SUBLANES, D_MODEL), lambda j: (0, 0)),
                  pl.BlockSpec((D_MODEL, tn), lambda j: (0, j)),
                  pl.BlockSpec((1, tn), lambda j: (0, j))],
        out_specs=pl.BlockSpec((SUBLANES, tn), lambda j: (0, j)),
        out_shape=jax.ShapeDtypeStruct((SUBLANES, n), F32),
        compiler_params=_cparams(("arbitrary",)),
        name="modulation",
    )(cs, w, b.reshape(1, n))
    return out.reshape(SUBLANES * N_MOD, 1, D_MODEL)


L0_CQKV = MLA_Q_RANK + MLA_KV_RANK + LANES
L0_IN_PAD = L0_CQKV + 2 * LRU_WIDTH


def _in0_kernel(x_ref, g_ref, sh_ref, sc_ref, w_ref, cqkv_ref, xr_ref, gr_ref):
    h = _rms(x_ref[...], g_ref[...]) * (1.0 + sc_ref[0]) + sh_ref[0]
    y = _dot(h.astype(BF16), w_ref[...])
    cqkv_ref[...] = y[:, :L0_CQKV]
    xr_ref[...] = y[:, L0_CQKV:L0_CQKV + LRU_WIDTH]
    gr_ref[...] = y[:, L0_CQKV + LRU_WIDTH:]


def _in0(x, g, mod, w):
    row = lambda i: (i, 0)
    return pl.pallas_call(
        _in0_kernel,
        grid=(TOK // TM,),
        in_specs=[pl.BlockSpec((TM, D_MODEL), row), _resident((1, D_MODEL)),
                  _mod_spec(_comb_tile, 0), _mod_spec(_comb_tile, 1), _resident((D_MODEL, L0_IN_PAD))],
        out_specs=[pl.BlockSpec((TM, L0_CQKV), row), pl.BlockSpec((TM, LRU_WIDTH), row),
                   pl.BlockSpec((TM, LRU_WIDTH), row)],
        out_shape=[jax.ShapeDtypeStruct((TOK, L0_CQKV), F32), jax.ShapeDtypeStruct((TOK, LRU_WIDTH), F32),
                   jax.ShapeDtypeStruct((TOK, LRU_WIDTH), F32)],
        compiler_params=_cparams(("parallel",)),
        name="l0_in_proj",
    )(x, g, mod, mod, w)


def _mla_proj_kernel(c_ref, qn_ref, kvn_ref, wq_ref, wk_ref, wv_ref, cos_ref, sin_ref, q_ref, k_ref, v_ref):
    c = c_ref[...]
    cos = cos_ref[...]
    sin = sin_ref[...]
    nq = _rms(c[:, :MLA_Q_RANK], qn_ref[...]).astype(BF16)
    q = _dot(nq, wq_ref[...]) * (LOG2E * (MLA_NOPE + MLA_ROPE) ** -0.5)
    nkv = _rms(c[:, MLA_Q_RANK:MLA_Q_RANK + MLA_KV_RANK], kvn_ref[...]).astype(BF16)
    kn = _dot(nkv, wk_ref[...])
    v_ref[...] = _dot(nkv, wv_ref[...]).astype(BF16)
    kr = _rope(c[:, MLA_Q_RANK + MLA_KV_RANK:], cos, sin, MLA_ROPE // 4).astype(BF16)
    for h in range(MLA_HEADS):
        lo = h * MLA_QK
        q_ref[:, lo:lo + LANES] = q[:, lo:lo + LANES].astype(BF16)
        q_ref[:, lo + LANES:lo + MLA_QK] = _rope(q[:, lo + LANES:lo + MLA_QK], cos, sin, MLA_ROPE // 4).astype(BF16)
        k_ref[:, lo:lo + LANES] = kn[:, h * LANES:(h + 1) * LANES].astype(BF16)
        k_ref[:, lo + LANES:lo + MLA_QK] = kr


def _mla_proj(cqkv, qn, kvn, wq, wk, wv, cos, sin):
    row = lambda i: (i, 0)
    pos = lambda i: (i % TPB, 0)
    hq = MLA_HEADS * MLA_QK
    hv = MLA_HEADS * MLA_V
    return pl.pallas_call(
        _mla_proj_kernel,
        grid=(TOK // TM,),
        in_specs=[pl.BlockSpec((TM, L0_CQKV), row), _resident((1, MLA_Q_RANK)), _resident((1, MLA_KV_RANK)),
                  _resident((MLA_Q_RANK, hq)), _resident((MLA_KV_RANK, hv)), _resident((MLA_KV_RANK, hv)),
                  pl.BlockSpec((TM, LANES), pos), pl.BlockSpec((TM, LANES), pos)],
        out_specs=[pl.BlockSpec((TM, hq), row), pl.BlockSpec((TM, hq), row), pl.BlockSpec((TM, hv), row)],
        out_shape=[jax.ShapeDtypeStruct((TOK, hq), BF16), jax.ShapeDtypeStruct((TOK, hq), BF16),
                   jax.ShapeDtypeStruct((TOK, hv), BF16)],
        compiler_params=_cparams(("parallel",)),
        name="mla_proj",
    )(cqkv, qn, kvn, wq, wk, wv, cos, sin)


MLA_TQ = 256


def _softmax_pv(s, v):
    m = jnp.max(s, axis=-1, keepdims=True)
    p = jnp.exp2(s - m)
    l = jnp.sum(p, axis=-1, keepdims=True)
    return _dot(p.astype(BF16), v) / l


def _mla_attn_kernel(q_ref, k_ref, v_ref, o_ref):
    s = _dot_nt(q_ref[0:CTX_LEN, :], k_ref[0:CTX_LEN, :])
    o_ref[0:CTX_LEN, :] = _softmax_pv(s, v_ref[0:CTX_LEN, :]).astype(o_ref.dtype)

    def body(i, carry):
        for t in range(2):
            r0 = pl.multiple_of(CTX_LEN + (2 * i + t) * MLA_TQ, MLA_TQ)
            s = _dot_nt(q_ref[pl.ds(r0, MLA_TQ), :], k_ref[...])
            o_ref[pl.ds(r0, MLA_TQ), :] = _softmax_pv(s, v_ref[...]).astype(o_ref.dtype)
        return carry

    lax.fori_loop(0, SEQ // (2 * MLA_TQ), body, 0)


def _mla_attn(q, k, v):
    blk = lambda b, h: (b, h)
    return pl.pallas_call(
        _mla_attn_kernel,
        grid=(BATCH, MLA_HEADS),
        in_specs=[pl.BlockSpec((NB, MLA_QK), blk), pl.BlockSpec((NB, MLA_QK), blk), pl.BlockSpec((NB, MLA_V), blk)],
        out_specs=pl.BlockSpec((NB, MLA_V), blk),
        out_shape=jax.ShapeDtypeStruct((TOK, MLA_HEADS * MLA_V), BF16),
        compiler_params=_cparams(("parallel", "parallel")),
        name="mla_attn",
    )(q, k, v)


LRU_BW = LRU_WIDTH // LRU_BLOCKS
CTX_GROUPS = CTX_LEN // SUBLANES
LAT_GROUPS = SEQ // SUBLANES


def _scan_group(a, b, reverse):
    row = lax.broadcasted_iota(jnp.int32, a.shape, 0)
    for d in (1, 2, 4):
        shift = SUBLANES - d if reverse else d
        a_s = pltpu.roll(a, shift, axis=0)
        b_s = pltpu.roll(b, shift, axis=0)
        m = (row < SUBLANES - d) if reverse else (row >= d)
        b = jnp.where(m, a * b_s + b, b)
        a = jnp.where(m, a * a_s, a)
    return a, b


def _rglru_kernel(xr_ref, gr_ref, cw_ref, cb_ref, wa_ref, ba_ref, wx_ref, bx_ref, lam_ref, y_ref,
                  af_ref, bf_ref, ab_ref, bb_ref, hf_ref, hb_ref):
    row8 = lax.broadcasted_iota(jnp.int32, (SUBLANES, LRU_BW), 0)

    def taps(seg):
        n = seg.shape[0]

        def shifted(shift, keep, first):
            r = pltpu.roll(seg, shift % n, axis=0)
            if first:
                return jnp.concatenate([jnp.where(keep, r[:SUBLANES], 0.0), r[SUBLANES:]], axis=0)
            return jnp.concatenate([r[:-SUBLANES], jnp.where(keep, r[-SUBLANES:], 0.0)], axis=0)

        return shifted(2, row8 >= 2, True), shifted(1, row8 >= 1, True), shifted(-1, row8 < SUBLANES - 1, False)

    x = xr_ref[...]
    tc = taps(x[:CTX_LEN])
    tl = taps(x[CTX_LEN:])
    xm2, xm1, xp1 = [jnp.concatenate([a, b], axis=0) for a, b in zip(tc, tl)]
    u = cb_ref[...] + xm2 * cw_ref[0:1, :] + xm1 * cw_ref[1:2, :] + x * cw_ref[2:3, :] + xp1 * cw_ref[3:4, :]
    ub = u.astype(BF16)
    for d, (a_ref, b_ref) in enumerate(((af_ref, bf_ref), (ab_ref, bb_ref))):
        r = _sigmoid(_dot(ub, wa_ref[d, 0].astype(BF16)) + ba_ref[d:d + 1, :])
        ig = _sigmoid(_dot(ub, wx_ref[d, 0].astype(BF16)) + bx_ref[d:d + 1, :])
        z = -lam_ref[d:d + 1, :]
        softplus = jnp.maximum(z, 0.0) + jnp.log(1.0 + jnp.exp(-jnp.abs(z)))
        log_a = -LRU_C * r * softplus
        a = jnp.exp(log_a)
        a_ref[...] = a
        t = 1.0 - a * a
        b_ref[...] = jnp.where(t > 0.0, t * lax.rsqrt(t), 0.0) * (ig * u)

    def step(gf, gb, hf, hb):
        rf = pl.multiple_of(gf * SUBLANES, SUBLANES)
        a, b = _scan_group(af_ref[pl.ds(rf, SUBLANES), :], bf_ref[pl.ds(rf, SUBLANES), :], False)
        h = a * hf + b
        hf_ref[pl.ds(rf, SUBLANES), :] = h
        hf = jnp.broadcast_to(h[SUBLANES - 1:SUBLANES, :], h.shape)
        rb = pl.multiple_of(gb * SUBLANES, SUBLANES)
        a, b = _scan_group(ab_ref[pl.ds(rb, SUBLANES), :], bb_ref[pl.ds(rb, SUBLANES), :], True)
        h = a * hb + b
        hb_ref[pl.ds(rb, SUBLANES), :] = h
        hb = jnp.broadcast_to(h[0:1, :], h.shape)
        return hf, hb

    zero = jnp.zeros((SUBLANES, LRU_BW), F32)
    carry = lax.fori_loop(0, CTX_GROUPS, lambda i, c: step(i, CTX_GROUPS - 1 - i, *c), (zero, zero), unroll=2)
    lax.fori_loop(0, LAT_GROUPS, lambda i, c: step(CTX_GROUPS + i, CTX_GROUPS + LAT_GROUPS - 1 - i, *c), carry, unroll=2)
    y_ref[...] = ((hf_ref[...] + hb_ref[...]) * jax.nn.gelu(gr_ref[...])).astype(y_ref.dtype)


def _rglru(xr, gr, conv_w, conv_b, wa, ba, wx, bx, lam):
    blk = lambda b, n: (b, n)
    col = lambda b, n: (0, n)
    gate = lambda b, n: (0, n, 0, 0)
    seg = pltpu.VMEM((NB, LRU_BW), F32)
    return pl.pallas_call(
        _rglru_kernel,
        grid=(BATCH, LRU_BLOCKS),
        in_specs=[pl.BlockSpec((NB, LRU_BW), blk), pl.BlockSpec((NB, LRU_BW), blk),
                  pl.BlockSpec((4, LRU_BW), col), pl.BlockSpec((1, LRU_BW), col),
                  pl.BlockSpec((2, 1, LRU_BW, LRU_BW), gate), pl.BlockSpec((2, LRU_BW), col),
                  pl.BlockSpec((2, 1, LRU_BW, LRU_BW), gate), pl.BlockSpec((2, LRU_BW), col),
                  pl.BlockSpec((2, LRU_BW), col)],
        out_specs=pl.BlockSpec((NB, LRU_BW), blk),
        out_shape=jax.ShapeDtypeStruct((TOK, LRU_WIDTH), BF16),
        scratch_shapes=[seg, seg, seg, seg, seg, seg],
        compiler_params=_cparams(("parallel", "parallel")),
        name="rglru",
    )(xr, gr, conv_w, conv_b.reshape(1, LRU_WIDTH), wa, ba, wx, bx, lam)


L1_Q = WIN_HEADS * HEAD_DIM
L1_KV = WIN_KV_HEADS * HEAD_DIM
L1_NA = NA_HEADS * HEAD_DIM
L1_IN = L1_Q + 2 * L1_KV + 3 * L1_NA


def _in1_kernel(x_ref, g_ref, sh_ref, sc_ref, w_ref, cos_ref, sin_ref,
                qw_ref, kw_ref, vw_ref, qn_ref, kn_ref, vn_ref):
    h = _rms(x_ref[...], g_ref[...]) * (1.0 + sc_ref[0]) + sh_ref[0]
    y = _dot(h.astype(BF16), w_ref[...])
    cos = cos_ref[...]
    sin = sin_ref[...]
    scale = LOG2E * HEAD_DIM ** -0.5
    for hd in range(WIN_HEADS):
        lo = hd * HEAD_DIM
        qw_ref[:, lo:lo + HEAD_DIM] = _rope(y[:, lo:lo + HEAD_DIM] * scale, cos, sin, HEAD_DIM // 4).astype(BF16)
    for hd in range(WIN_KV_HEADS):
        lo = hd * HEAD_DIM
        kw_ref[:, lo:lo + HEAD_DIM] = _rope(y[:, L1_Q + lo:L1_Q + lo + HEAD_DIM], cos, sin, HEAD_DIM // 4).astype(BF16)
    o = L1_Q + L1_KV
    vw_ref[...] = y[:, o:o + L1_KV].astype(BF16)
    o += L1_KV
    qn_ref[...] = (y[:, o:o + L1_NA] * scale).astype(BF16)
    kn_ref[...] = y[:, o + L1_NA:o + 2 * L1_NA].astype(BF16)
    vn_ref[...] = y[:, o + 2 * L1_NA:].astype(BF16)


def _in1(x, g, mod, w, cos, sin):
    row = lambda i: (i, 0)
    pos = lambda i: (i % TPB, 0)
    widths = (L1_Q, L1_KV, L1_KV, L1_NA, L1_NA, L1_NA)
    return pl.pallas_call(
        _in1_kernel,
        grid=(TOK // TM,),
        in_specs=[pl.BlockSpec((TM, D_MODEL), row), _resident((1, D_MODEL)),
                  _mod_spec(_comb_tile, 0), _mod_spec(_comb_tile, 1), _resident((D_MODEL, L1_IN)),
                  pl.BlockSpec((TM, LANES), pos), pl.BlockSpec((TM, LANES), pos)],
        out_specs=[pl.BlockSpec((TM, n), row) for n in widths],
        out_shape=[jax.ShapeDtypeStruct((TOK, n), BF16) for n in widths],
        compiler_params=_cparams(("parallel",)),
        name="l1_in_proj",
    )(x, g, mod, mod, w, cos, sin)


WIN_TQ = 128
WIN_SPAN = WIN_TQ + 2 * WINDOW
WIN_G = WIN_HEADS // WIN_KV_HEADS


def _win_kernel(sink_ref, q_ref, k_ref, v_ref, o_ref):
    hk = pl.program_id(1)
    o_ref[0:CTX_LEN, :] = jnp.zeros((CTX_LEN, WIN_G * HEAD_DIM), o_ref.dtype)
    rows = WIN_G * WIN_TQ
    head = lax.broadcasted_iota(jnp.int32, (rows, 1), 0) // WIN_TQ
    sink = jnp.zeros((rows, 1), F32)
    for g in range(WIN_G):
        sink = jnp.where(head == g, sink_ref[hk * WIN_G + g] * LOG2E, sink)
    qoff = lax.broadcasted_iota(jnp.int32, (rows, WIN_SPAN), 0) % WIN_TQ
    koff = lax.broadcasted_iota(jnp.int32, (rows, WIN_SPAN), 1)

    def tile(n):
        r0 = pl.multiple_of(CTX_LEN + n * WIN_TQ, WIN_TQ)
        start = jnp.clip((n - 1) * WIN_TQ, 0, SEQ - WIN_SPAN)
        ks = pl.multiple_of(CTX_LEN + start, WIN_TQ)
        q4 = q_ref[pl.ds(r0, WIN_TQ), :]
        q = jnp.concatenate([q4[:, g * HEAD_DIM:(g + 1) * HEAD_DIM] for g in range(WIN_G)], axis=0)
        s_c = _dot_nt(q, k_ref[0:CTX_LEN, :])
        s_w = _dot_nt(q, k_ref[pl.ds(ks, WIN_SPAN), :])
        valid = jnp.abs(n * WIN_TQ + qoff - (start + koff)) <= WINDOW
        s_w = jnp.where(valid, s_w, NEG_INF)
        m = jnp.maximum(jnp.maximum(jnp.max(s_c, axis=-1, keepdims=True), jnp.max(s_w, axis=-1, keepdims=True)), sink)
        p_c = jnp.exp2(s_c - m)
        p_w = jnp.exp2(s_w - m)
        l = jnp.sum(p_c, axis=-1, keepdims=True) + jnp.sum(p_w, axis=-1, keepdims=True) + jnp.exp2(sink - m)
        o = (_dot(p_w.astype(BF16), v_ref[pl.ds(ks, WIN_SPAN), :]) + _dot(p_c.astype(BF16), v_ref[0:CTX_LEN, :])) / l
        for g in range(WIN_G):
            o_ref[pl.ds(r0, WIN_TQ), g * HEAD_DIM:(g + 1) * HEAD_DIM] = o[g * WIN_TQ:(g + 1) * WIN_TQ].astype(o_ref.dtype)

    def body(i, carry):
        tile(2 * i)
        tile(2 * i + 1)
        return carry

    lax.fori_loop(0, SEQ // (2 * WIN_TQ), body, 0)


def _win_attn(sink, q, k, v):
    blk = lambda b, h, *_: (b, h)
    return pl.pallas_call(
        _win_kernel,
        grid_spec=pltpu.PrefetchScalarGridSpec(
            num_scalar_prefetch=1,
            grid=(BATCH, WIN_KV_HEADS),
            in_specs=[pl.BlockSpec((NB, WIN_G * HEAD_DIM), blk), pl.BlockSpec((NB, HEAD_DIM), blk),
                      pl.BlockSpec((NB, HEAD_DIM), blk)],
            out_specs=pl.BlockSpec((NB, WIN_G * HEAD_DIM), blk)),
        out_shape=jax.ShapeDtypeStruct((TOK, L1_Q), BF16),
        compiler_params=_cparams(("parallel", "parallel")),
        name="window_attn",
    )(sink, q, k, v)


NA_GRID_ROWS = SEQ // GRID_W
NA_BAND = NA_ROWS * GRID_W


NA_RPI = 4


def _na_kernel(q_ref, k_ref, v_ref, bias_ref, o_ref):
    o_ref[0:CTX_LEN, :] = jnp.zeros((CTX_LEN, HEAD_DIM), o_ref.dtype)

    def body(i, carry):
        qs = pl.multiple_of(CTX_LEN + i * (NA_RPI * GRID_W), NA_RPI * GRID_W)
        q = q_ref[pl.ds(qs, NA_RPI * GRID_W), :]
        s_c = _dot_nt(q, k_ref[0:CTX_LEN, :])
        starts = []
        s_w = []
        for j in range(NA_RPI):
            r = i * NA_RPI + j
            r0 = jnp.clip(r - NA_ROWS // 2, 0, NA_GRID_ROWS - NA_ROWS)
            ks = pl.multiple_of(CTX_LEN + r0 * GRID_W, GRID_W)
            starts.append(ks)
            s_w.append(_dot_nt(q[j * GRID_W:(j + 1) * GRID_W], k_ref[pl.ds(ks, NA_BAND), :])
                       + bias_ref[0, r0 - r + NA_ROWS - 1])
        s_w = jnp.concatenate(s_w, axis=0)
        m = jnp.maximum(jnp.max(s_c, axis=-1, keepdims=True), jnp.max(s_w, axis=-1, keepdims=True))
        p_c = jnp.exp2(s_c - m)
        p_w = jnp.exp2(s_w - m)
        l = jnp.sum(p_c, axis=-1, keepdims=True) + jnp.sum(p_w, axis=-1, keepdims=True)
        p_w = p_w.astype(BF16)
        o_w = jnp.concatenate([_dot(p_w[j * GRID_W:(j + 1) * GRID_W], v_ref[pl.ds(starts[j], NA_BAND), :])
                               for j in range(NA_RPI)], axis=0)
        o = (o_w + _dot(p_c.astype(BF16), v_ref[0:CTX_LEN, :])) / l
        o_ref[pl.ds(qs, NA_RPI * GRID_W), :] = o.astype(o_ref.dtype)
        return carry

    lax.fori_loop(0, NA_GRID_ROWS // NA_RPI, body, 0)


def _na_attn(q, k, v, bias):
    blk = lambda b, h: (b, h)
    return pl.pallas_call(
        _na_kernel,
        grid=(BATCH, NA_HEADS),
        in_specs=[pl.BlockSpec((NB, HEAD_DIM), blk), pl.BlockSpec((NB, HEAD_DIM), blk),
                  pl.BlockSpec((NB, HEAD_DIM), blk),
                  pl.BlockSpec((1, NA_ROWS, GRID_W, NA_BAND), lambda b, h: (h, 0, 0, 0))],
        out_specs=pl.BlockSpec((NB, HEAD_DIM), blk),
        out_shape=jax.ShapeDtypeStruct((TOK, L1_NA), BF16),
        compiler_params=_cparams(("parallel", "parallel")),
        name="na_attn",
    )(q, k, v, bias)


def _na_bias_table(rpb):
    cols = jnp.arange(GRID_W)
    c0 = jnp.clip(cols - NA_COLS // 2, 0, GRID_W - NA_COLS)
    kc = cols[None, :]
    valid = (kc >= c0[:, None]) & (kc < c0[:, None] + NA_COLS)
    lo = GRID_W - NA_COLS
    ext = jnp.pad(rpb.astype(F32) * LOG2E, ((0, 0), (0, 0), (lo, lo)))
    tbl = jnp.stack([ext[:, :, GRID_W - 1 - c:2 * GRID_W - 1 - c] for c in range(GRID_W)], axis=2)
    tbl = jnp.where(valid[None, None], tbl, NEG_INF)
    tbl = jnp.stack([tbl[:, d:d + NA_ROWS] for d in range(NA_ROWS)], axis=1)
    return jnp.transpose(tbl, (0, 1, 3, 2, 4)).reshape(NA_HEADS, NA_ROWS, GRID_W, NA_BAND)


def _router(h, rwh_ref, rwl_ref, rb_ref, tri_ref, carry):
    tm = h.shape[0]
    hh = h.astype(BF16)
    hl = (h - hh.astype(F32)).astype(BF16)
    logits = _dot_nt(rwh_ref[...], hh) + (_dot_nt(rwh_ref[...], hl) + _dot_nt(rwl_ref[...], hh))
    scores = jax.nn.sigmoid(logits)
    biased = scores + rb_ref[...]
    nj = EXPERTS_PER_GROUP
    s = [biased[j * N_GROUPS:(j + 1) * N_GROUPS] for j in range(nj)]
    u = [scores[j * N_GROUPS:(j + 1) * N_GROUPS] for j in range(nj)]
    gs = None
    for a in range(nj):
        for b in range(a + 1, nj):
            pair = s[a] + s[b]
            gs = pair if gs is None else jnp.maximum(gs, pair)
    giota = lax.broadcasted_iota(jnp.int32, (N_GROUPS, tm), 0).astype(F32)
    gmax = jnp.max(gs, axis=0, keepdims=True)
    gidx = jnp.min(jnp.where(gs == gmax, giota, float(N_GROUPS)), axis=0, keepdims=True)
    gm = giota == gidx
    v = [jnp.sum(jnp.where(gm, s[j], 0.0), axis=0, keepdims=True) for j in range(nj)]
    w = [jnp.sum(jnp.where(gm, u[j], 0.0), axis=0, keepdims=True) for j in range(nj)]
    sel = []
    for j in range(nj):
        beaten = jnp.zeros((1, tm), F32)
        for i in range(nj):
            if i != j:
                ahead = (v[i] >= v[j]) if i < j else (v[i] > v[j])
                beaten = beaten + jnp.where(ahead, 1.0, 0.0)
        sel.append(beaten < 2.0)
    wsum = sum(jnp.where(sel[j], w[j], 0.0) for j in range(nj))
    first = functools.reduce(jnp.minimum, [jnp.where(sel[j], float(j), float(nj)) for j in range(nj)])
    last = functools.reduce(jnp.maximum, [jnp.where(sel[j], float(j), -1.0) for j in range(nj)])
    gmf = jnp.where(gm, 1.0, 0.0)
    cnt = jnp.concatenate([jnp.where(sel[j], gmf, 0.0) for j in range(nj)], axis=0)
    pos = _dot(cnt.astype(BF16), tri_ref[...]) + carry
    rank = [jnp.sum(cnt[j * N_GROUPS:(j + 1) * N_GROUPS] * pos[j * N_GROUPS:(j + 1) * N_GROUPS], axis=0, keepdims=True)
            for j in range(nj)]
    pick = lambda which, vals: sum(jnp.where(which == float(j), vals[j], 0.0) for j in range(nj))
    gate = [w[j] / wsum for j in range(nj)]
    zero = jnp.zeros((1, tm), F32)
    info = jnp.concatenate(
        [gidx * nj + first, gidx * nj + last, pick(first, rank), pick(last, rank), pick(first, gate), pick(last, gate),
         zero, zero], axis=0)
    return info, carry + jnp.sum(cnt, axis=1, keepdims=True)


OUT_SUB = 128


def _out_kernel(ya_ref, yb_ref, x_ref, wa_ref, wb_ref, g1_ref, n_ref, sh_ref, sc_ref,
                rwh_ref, rwl_ref, rb_ref, tri_ref, xo_ref, h_ref, info_ref, counts_ref, carry_ref):
    @pl.when(pl.program_id(0) == 0)
    def _():
        carry_ref[...] = jnp.zeros_like(carry_ref)

    carry = carry_ref[:, 0:1]
    for r0 in range(0, TM, OUT_SUB):
        rows = slice(r0, r0 + OUT_SUB)
        y = _dot(ya_ref[rows, :], wa_ref[...]) + _dot(yb_ref[rows, :], wb_ref[...])
        x = x_ref[rows, :] + g1_ref[0] * y
        xo_ref[rows, :] = x
        h = _rms(x, n_ref[...]) * (1.0 + sc_ref[0]) + sh_ref[0]
        h_ref[rows, :] = h
        info_ref[:, rows], carry = _router(h, rwh_ref, rwl_ref, rb_ref, tri_ref, carry)
    counts = jnp.broadcast_to(carry, carry_ref.shape)
    carry_ref[...] = counts
    counts_ref[...] = counts


def _out_proj(ya, yb, x, w_out, mod, norm, rwh, rwl, rb, tri, tile_fn, n_tiles):
    half = w_out.shape[0] // 2
    src = lambda i: (tile_fn(i)[0], 0)
    dst = lambda i: (i, 0)
    n_tok = n_tiles * TM
    return pl.pallas_call(
        _out_kernel,
        grid=(n_tiles,),
        in_specs=[pl.BlockSpec((TM, half), src), pl.BlockSpec((TM, half), src), pl.BlockSpec((TM, D_MODEL), src),
                  _resident((half, D_MODEL)), _resident((half, D_MODEL)),
                  _mod_spec(tile_fn, 2), _resident((1, D_MODEL)), _mod_spec(tile_fn, 3), _mod_spec(tile_fn, 4),
                  _resident((N_EXPERTS, D_MODEL)), _resident((N_EXPERTS, D_MODEL)), _resident((N_EXPERTS, 1)),
                  _resident((OUT_SUB, OUT_SUB))],
        out_specs=[pl.BlockSpec((TM, D_MODEL), dst), pl.BlockSpec((TM, D_MODEL), dst),
                   pl.BlockSpec((SUBLANES, TM), lambda i: (0, i)),
                   pl.BlockSpec((N_EXPERTS, LANES), lambda i: (0, 0))],
        out_shape=[jax.ShapeDtypeStruct((n_tok, D_MODEL), F32), jax.ShapeDtypeStruct((n_tok, D_MODEL), F32),
                   jax.ShapeDtypeStruct((SUBLANES, n_tok), F32), jax.ShapeDtypeStruct((N_EXPERTS, LANES), F32)],
        scratch_shapes=[pltpu.VMEM((N_EXPERTS, LANES), F32)],
        compiler_params=_cparams(("arbitrary",)),
        name="out_proj_router",
    )(ya, yb, x, w_out[:half], w_out[half:], mod, norm, mod, mod, rwh, rwl, rb, tri)


def _lookup(table, idx):
    onehot = idx[..., None] == jnp.arange(table.shape[0], dtype=jnp.int32)
    return jnp.sum(jnp.where(onehot, table, 0), axis=-1)


def _dispatch_plan(info, counts, n_tok):
    n_steps = 2 * n_tok // TMX + N_EXPERTS - 1
    experts = jnp.arange(N_EXPERTS, dtype=jnp.int32)
    cnt = counts[:, 0].astype(jnp.int32).reshape(EXPERTS_PER_GROUP, N_GROUPS).T.reshape(N_EXPERTS)
    end = jnp.cumsum(cnt)
    off = end - cnt
    first_tile = off // TMX
    visits = jnp.where(cnt > 0, (end - 1) // TMX - first_tile + 1, 0)
    visit_end = jnp.cumsum(visits)
    n_valid = visit_end[-1]
    dest = _lookup(off, info[0:2].astype(jnp.int32)) + info[2:4].astype(jnp.int32)
    step = jnp.minimum(jnp.arange(n_steps, dtype=jnp.int32), n_valid - 1)
    e = jnp.sum(visit_end[None, :] <= step[:, None], axis=1).astype(jnp.int32)
    k = step - _lookup(visit_end - visits, e)
    tile = _lookup(first_tile, e) + k
    lo = jnp.clip(_lookup(off, e) - tile * TMX, 0, TMX)
    hi = jnp.clip(_lookup(end, e) - tile * TMX, 0, TMX)
    later = (experts[None, :] > experts[:, None]) & (cnt[None, :] > 0)
    nxt = jnp.min(jnp.where(later, experts[None, :], N_EXPERTS), axis=1)
    nxt = jnp.where(nxt == N_EXPERTS, -1, nxt)
    slot = (jnp.cumsum((cnt > 0).astype(jnp.int32)) - 1) % 2
    i32 = lambda v: v.astype(jnp.int32)
    return (dest.reshape(-1), i32(tile), i32(e), i32(n_valid.reshape(1)), i32(lo), i32(hi), i32(k == 0),
            i32(_lookup(nxt, e)), i32(_lookup(slot, e)))


def _dispatch_kernel(dest_ref, h_ref, xs_hbm, sem, *, n_tok):
    i = pl.program_id(0)

    def row_copy(r, d):
        return pltpu.make_async_copy(h_ref.at[pl.ds(r, 1)], xs_hbm.at[pl.ds(d, 1)], sem)

    def start(r, c):
        for k in range(2):
            row_copy(r, dest_ref[k * n_tok + i * TM + r]).start()
        return c

    lax.fori_loop(0, TM, start, 0, unroll=8)

    def wait(r, c):
        for k in range(2):
            row_copy(r, 0).wait()
        return c

    lax.fori_loop(0, TM, wait, 0, unroll=8)


def _dispatch(dest, h):
    n_tok = h.shape[0]
    return pl.pallas_call(
        functools.partial(_dispatch_kernel, n_tok=n_tok),
        grid_spec=pltpu.PrefetchScalarGridSpec(
            num_scalar_prefetch=1,
            grid=(n_tok // TM,),
            in_specs=[pl.BlockSpec((TM, D_MODEL), lambda i, d: (i, 0))],
            out_specs=pl.BlockSpec(memory_space=pl.ANY),
            scratch_shapes=[pltpu.SemaphoreType.DMA(())]),
        out_shape=jax.ShapeDtypeStruct((2 * n_tok, D_MODEL), F32),
        compiler_params=_cparams(("arbitrary",)),
        name="moe_dispatch",
    )(dest, h)


def _moe_kernel(tile_ref, te_ref, nv_ref, lo_ref, hi_ref, first_ref, nxt_ref, slot_ref,
                xs_ref, wg_hbm, wu_hbm, wd_hbm, ys_ref, wg_f, wu_f, wd_f, wg_s, wu_s, wd_s, sem):
    s = pl.program_id(0)

    def fetch(e, slot):
        return [pltpu.make_async_copy(src.at[e], dst.at[slot], sem.at[slot])
                for src, dst in ((wg_hbm, wg_f), (wu_hbm, wu_f), (wd_hbm, wd_f))]

    @pl.when(s < nv_ref[0])
    def _():
        @pl.when(first_ref[s] == 1)
        def _():
            slot = slot_ref[s]

            @pl.when(s == 0)
            def _():
                for c in fetch(te_ref[0], slot):
                    c.start()

            for c in fetch(te_ref[s], slot):
                c.wait()

            @pl.when(nxt_ref[s] >= 0)
            def _():
                for c in fetch(nxt_ref[s], 1 - slot):
                    c.start()

            wg_s[...] = wg_f[slot].astype(BF16)
            wu_s[...] = wu_f[slot].astype(BF16)
            wd_s[...] = wd_f[slot].astype(BF16)

        lo = lo_ref[s]
        hi = hi_ref[s]

        @pl.when(lo == 0)
        def _():
            ys_ref[...] = jnp.zeros_like(ys_ref)

        def visit(r0, n):
            x = xs_ref[r0:r0 + n, :].astype(BF16)
            hg = _dot(x, wg_s[...])
            he = (hg * jax.nn.sigmoid(hg)) * _dot(x, wu_s[...])
            y = _dot(he.astype(BF16), wd_s[...])
            row = r0 + lax.broadcasted_iota(jnp.int32, (n, 1), 0)
            ys_ref[r0:r0 + n, :] = jnp.where((row >= lo) & (row < hi), y, ys_ref[r0:r0 + n, :])

        half = TMX // 2
        lower = hi <= half
        upper = lo >= half
        pl.when(lower)(lambda: visit(0, half))
        pl.when(upper)(lambda: visit(half, half))
        pl.when(jnp.logical_not(jnp.logical_or(lower, upper)))(lambda: visit(0, TMX))


def _moe(plan, xs, w_gate, w_up, w_down):
    n_steps = plan[0].shape[0]
    tile = lambda s, t, *_: (t[s], 0)
    any_spec = pl.BlockSpec(memory_space=pl.ANY)
    return pl.pallas_call(
        _moe_kernel,
        grid_spec=pltpu.PrefetchScalarGridSpec(
            num_scalar_prefetch=len(plan),
            grid=(n_steps,),
            in_specs=[pl.BlockSpec((TMX, D_MODEL), tile), any_spec, any_spec, any_spec],
            out_specs=pl.BlockSpec((TMX, D_MODEL), tile),
            scratch_shapes=[pltpu.VMEM((2, D_MODEL, EXPERT_FF), F32), pltpu.VMEM((2, D_MODEL, EXPERT_FF), F32),
                            pltpu.VMEM((2, EXPERT_FF, D_MODEL), F32),
                            pltpu.VMEM((D_MODEL, EXPERT_FF), BF16), pltpu.VMEM((D_MODEL, EXPERT_FF), BF16),
                            pltpu.VMEM((EXPERT_FF, D_MODEL), BF16), pltpu.SemaphoreType.DMA((2,))]),
        out_shape=jax.ShapeDtypeStruct(xs.shape, F32),
        compiler_params=_cparams(("arbitrary",)),
        name="moe_experts",
    )(*plan, xs, w_gate, w_up, w_down)


def _combine_kernel(dest_ref, ys_hbm, x_ref, w_ref, g2_ref, n_ref, o_ref, y_buf, sem, *, n_tok, final):
    i = pl.program_id(0)

    def row_copy(k, r, d):
        return pltpu.make_async_copy(ys_hbm.at[pl.ds(d, 1)], y_buf.at[k, pl.ds(r, 1)], sem)

    def start(r, c):
        for k in range(2):
            row_copy(k, r, dest_ref[k * n_tok + i * TM + r]).start()
        return c

    lax.fori_loop(0, TM, start, 0, unroll=8)

    def wait(r, c):
        for k in range(2):
            row_copy(k, r, 0).wait()
        return c

    lax.fori_loop(0, TM, wait, 0, unroll=8)
    w = w_ref[...]
    x = x_ref[...] + g2_ref[0] * (w[:, 4:5] * y_buf[0] + w[:, 5:6] * y_buf[1])
    o_ref[...] = _rms(x, n_ref[...]) if final else x


def _combine(dest, ys, x, winfo, mod, norm, tile_fn, n_tiles, final):
    n_tok = n_tiles * TM
    row = lambda i, d: (i, 0)
    return pl.pallas_call(
        functools.partial(_combine_kernel, n_tok=n_tok, final=final),
        grid_spec=pltpu.PrefetchScalarGridSpec(
            num_scalar_prefetch=1,
            grid=(n_tiles,),
            in_specs=[pl.BlockSpec(memory_space=pl.ANY), pl.BlockSpec((TM, D_MODEL), row),
                      pl.BlockSpec((TM, SUBLANES), row), _mod_spec(tile_fn, 5),
                      pl.BlockSpec((1, D_MODEL), lambda i, d: (0, 0))],
            out_specs=pl.BlockSpec((TM, D_MODEL), row),
            scratch_shapes=[pltpu.VMEM((2, TM, D_MODEL), F32), pltpu.SemaphoreType.DMA(())]),
        out_shape=jax.ShapeDtypeStruct((n_tok, D_MODEL), F32),
        compiler_params=_cparams(("arbitrary",)),
        name="moe_combine",
    )(dest, ys, x, winfo, mod, norm)


def _moe_block(x, h, info, counts, mod, norm, w_gate, w_up, w_down, tile_fn, n_tiles, final):
    dest, *plan = _dispatch_plan(info, counts, n_tiles * TM)
    ys = _moe(plan, _dispatch(dest, h), w_gate, w_up, w_down)
    return _combine(dest, ys, x, info.T, mod, norm, tile_fn, n_tiles, final)


def _rope_tables(dim):
    half, quarter = dim // 2, dim // 4
    t = jnp.arange(SEQ)
    pos = jnp.stack([t // GRID_W, t % GRID_W], axis=-1).astype(F32)
    inv_freq = ROPE_THETA ** (-jnp.arange(0, half, 2, dtype=F32) / half)
    ang = pos[:, :, None] * inv_freq
    cos = jnp.cos(ang)
    sin = jnp.sin(ang)
    cos = jnp.concatenate([cos[:, 0], cos[:, 0], cos[:, 1], cos[:, 1]], axis=-1)
    sin = jnp.concatenate([-sin[:, 0], sin[:, 0], -sin[:, 1], sin[:, 1]], axis=-1)
    pad = LANES - dim
    cos = jnp.pad(cos, ((0, 0), (0, pad)))
    sin = jnp.pad(sin, ((0, 0), (0, pad)))
    ctx_cos = jnp.pad(jnp.ones((CTX_LEN, dim), F32), ((0, 0), (0, pad)))
    return jnp.concatenate([ctx_cos, cos], axis=0), jnp.concatenate([jnp.zeros((CTX_LEN, LANES), F32), sin], axis=0)


def kernel(x, c, ctx, c_ctx, router_w, router_b, final_norm, l0_mod_w, l0_mod_b, l0_norm_mix, l0_norm_ffn, l0_w_in, l0_q_norm, l0_w_uq, l0_kv_norm, l0_w_ukv, l0_conv_w, l0_conv_b, l0_gate_a_w, l0_gate_a_b, l0_gate_x_w, l0_gate_x_b, l0_lru_lambda, l0_w_out, l0_exp_gate, l0_exp_up, l0_exp_down, l1_mod_w, l1_mod_b, l1_norm_mix, l1_norm_ffn, l1_w_in, l1_sink, l1_rpb, l1_w_out, l1_exp_gate, l1_exp_up, l1_exp_down):
    row = lambda v: v.reshape(1, -1)
    xs = jnp.concatenate([ctx, x], axis=1).reshape(TOK, D_MODEL)

    cs = jnp.concatenate([c, c_ctx[None], jnp.zeros((SUBLANES - BATCH - 1, D_MODEL), F32)], axis=0)
    mod0 = _modulation(cs, l0_mod_w, l0_mod_b)
    mod1 = _modulation(cs, l1_mod_w, l1_mod_b)

    perm = jnp.arange(N_EXPERTS).reshape(N_GROUPS, EXPERTS_PER_GROUP).T.reshape(-1)
    rwt = router_w.T[perm]
    rwh = rwt.astype(BF16)
    rwl = (rwt - rwh.astype(F32)).astype(BF16)
    rb = router_b[perm].reshape(N_EXPERTS, 1).astype(F32)
    tri = jnp.triu(jnp.ones((OUT_SUB, OUT_SUB), F32), k=1).astype(BF16)
    router = (rwh, rwl, rb, tri)

    split = MLA_Q_RANK + MLA_KV_RANK + MLA_ROPE
    w_in0 = jnp.concatenate([l0_w_in[:, :split], jnp.zeros((D_MODEL, LANES - MLA_ROPE), F32), l0_w_in[:, split:]],
                            axis=1).astype(BF16)
    cqkv, xr, gr = _in0(xs, row(l0_norm_mix), mod0, w_in0)
    wq = jnp.pad(l0_w_uq.reshape(MLA_Q_RANK, MLA_HEADS, MLA_NOPE + MLA_ROPE),
                 ((0, 0), (0, 0), (0, MLA_QK - MLA_NOPE - MLA_ROPE))).reshape(MLA_Q_RANK, MLA_HEADS * MLA_QK)
    wkv = l0_w_ukv.reshape(MLA_KV_RANK, MLA_HEADS, MLA_NOPE + MLA_V)
    wk = wkv[:, :, :MLA_NOPE].reshape(MLA_KV_RANK, MLA_HEADS * MLA_NOPE)
    wv = wkv[:, :, MLA_NOPE:].reshape(MLA_KV_RANK, MLA_HEADS * MLA_V)
    cos0, sin0 = _rope_tables(MLA_ROPE)
    q, k, v = _mla_proj(cqkv, row(l0_q_norm), row(l0_kv_norm), wq.astype(BF16), wk.astype(BF16), wv.astype(BF16),
                        cos0, sin0)
    att = _mla_attn(q, k, v)
    rnn = _rglru(xr, gr, l0_conv_w, l0_conv_b, l0_gate_a_w, l0_gate_a_b, l0_gate_x_w, l0_gate_x_b, l0_lru_lambda)
    n0 = TOK // TM
    xs, h, info, counts = _out_proj(att, rnn, xs, l0_w_out.astype(BF16), mod0, row(l0_norm_ffn), *router,
                                    _comb_tile, n0)
    xs = _moe_block(xs, h, info, counts, mod0, row(final_norm), l0_exp_gate, l0_exp_up, l0_exp_down,
                    _comb_tile, n0, False)

    cos1, sin1 = _rope_tables(HEAD_DIM)
    qw, kw, vw, qn, kn, vn = _in1(xs, row(l1_norm_mix), mod1, l1_w_in.astype(BF16), cos1, sin1)
    win = _win_attn(l1_sink.astype(F32), qw, kw, vw)
    na = _na_attn(qn, kn, vn, _na_bias_table(l1_rpb))
    n1 = BATCH * LAT_TPB
    xl, h, info, counts = _out_proj(win, na, xs, l1_w_out.astype(BF16), mod1, row(l1_norm_ffn), *router,
                                    _lat_tile, n1)
    out = _moe_block(xl, h, info, counts, mod1, row(final_norm), l1_exp_gate, l1_exp_up, l1_exp_down,
                     lambda i: (i, i // LAT_TPB), n1, True)
    return out.reshape(BATCH, SEQ, D_MODEL)
```

```python
import functools

import jax
import jax.numpy as jnp
from jax import lax
from jax.experimental import pallas as pl
from jax.experimental.pallas import tpu as pltpu

F32 = jnp.float32
BF16 = jnp.bfloat16

D_MODEL = 2048
BATCH = 4
SEQ = 2048
GRID_W = 64
CTX_LEN = 256
EPS = 1e-6
NEG_INF = -1e30
ROPE_THETA = 10000.0
N_MOD = 6

MLA_HEADS = 8
MLA_Q_RANK = 512
MLA_KV_RANK = 256
MLA_NOPE = 128
MLA_ROPE = 64
MLA_V = 128

LRU_WIDTH = 1024
LRU_BLOCKS = 8
LRU_C = 8.0

HEAD_DIM = 128
WIN_HEADS = 8
WIN_KV_HEADS = 2
WINDOW = 128
NA_HEADS = 8
NA_ROWS = 8
NA_COLS = 16

N_EXPERTS = 32
N_GROUPS = 8
EXPERTS_PER_GROUP = 4
EXPERT_FF = 512

LANES = 128
SUBLANES = 8
VMEM_LIMIT = 56 * 1024 * 1024

NB = CTX_LEN + SEQ
TOK = BATCH * NB
TM = 256
TPB = NB // TM
LAT_TPB = SEQ // TM
TMX = 256
MLA_QK = 2 * LANES
LOG2E = 1.4426950408889634


def _cparams(sem):
    return pltpu.CompilerParams(dimension_semantics=sem, vmem_limit_bytes=VMEM_LIMIT)


def _resident(shape):
    nd = len(shape)
    return pl.BlockSpec(shape, lambda *_: (0,) * nd, pipeline_mode=pl.Buffered(1))


def _rms(x, g):
    return x * lax.rsqrt(jnp.mean(x * x, axis=-1, keepdims=True) + EPS) * g


def _sigmoid(x):
    return 0.5 * jnp.tanh(0.5 * x) + 0.5


def _dot(a, b):
    return jnp.dot(a, b, preferred_element_type=F32)


def _dot_nt(a, b):
    return lax.dot_general(a, b, (((1,), (1,)), ((), ())), preferred_element_type=F32)


def _swap_blocks(x, blk):
    lane = lax.broadcasted_iota(jnp.int32, x.shape, 1)
    nxt = pltpu.roll(x, LANES - blk, axis=1)
    prv = pltpu.roll(x, blk, axis=1)
    return jnp.where((lane % (2 * blk)) < blk, nxt, prv)


def _rope(x, cos, sin, blk):
    return x * cos + _swap_blocks(x, blk) * sin


def _comb_tile(i):
    b = i // TPB
    return i, jnp.where(i % TPB == 0, BATCH, b)


def _lat_tile(i):
    b = i // LAT_TPB
    return b * TPB + 1 + i % LAT_TPB, b


def _mod_spec(tile_fn, k):
    return pl.BlockSpec((1, 1, D_MODEL), lambda i, *_: (tile_fn(i)[1] * N_MOD + k, 0, 0))


def _mod_kernel(c_ref, w_ref, b_ref, o_ref):
    c = c_ref[...]
    a = (c * jax.nn.sigmoid(c)).astype(BF16)
    o_ref[...] = _dot(a, w_ref[...].astype(BF16)) + b_ref[...]


def _modulation(cs, w, b):
    n = N_MOD * D_MODEL
    tn = 1024
    out = pl.pallas_call(
        _mod_kernel,
        grid=(n // tn,),
        in_specs=[pl.BlockSpec((SUBLANES, D_MODEL), lambda j: (0, 0)),
                  pl.BlockSpec((D_MODEL, tn), lambda j: (0, j)),
                  pl.BlockSpec((1, tn), lambda j: (0, j))],
        out_specs=pl.BlockSpec((SUBLANES, tn), lambda j: (0, j)),
        out_shape=jax.ShapeDtypeStruct((SUBLANES, n), F32),
        compiler_params=_cparams(("arbitrary",)),
        name="modulation",
    )(cs, w, b.reshape(1, n))
    return out.reshape(SUBLANES * N_MOD, 1, D_MODEL)


L0_CQKV = MLA_Q_RANK + MLA_KV_RANK + LANES
L0_IN_PAD = L0_CQKV + 2 * LRU_WIDTH


def _in0_kernel(x_ref, g_ref, sh_ref, sc_ref, w_ref, cqkv_ref, xr_ref, gr_ref):
    h = _rms(x_ref[...], g_ref[...]) * (1.0 + sc_ref[0]) + sh_ref[0]
    y = _dot(h.astype(BF16), w_ref[...])
    cqkv_ref[...] = y[:, :L0_CQKV]
    xr_ref[...] = y[:, L0_CQKV:L0_CQKV + LRU_WIDTH]
    gr_ref[...] = y[:, L0_CQKV + LRU_WIDTH:]


def _in0(x, g, mod, w):
    row = lambda i: (i, 0)
    return pl.pallas_call(
        _in0_kernel,
        grid=(TOK // TM,),
        in_specs=[pl.BlockSpec((TM, D_MODEL), row), _resident((1, D_MODEL)),
                  _mod_spec(_comb_tile, 0), _mod_spec(_comb_tile, 1), _resident((D_MODEL, L0_IN_PAD))],
        out_specs=[pl.BlockSpec((TM, L0_CQKV), row), pl.BlockSpec((TM, LRU_WIDTH), row),
                   pl.BlockSpec((TM, LRU_WIDTH), row)],
        out_shape=[jax.ShapeDtypeStruct((TOK, L0_CQKV), F32), jax.ShapeDtypeStruct((TOK, LRU_WIDTH), F32),
                   jax.ShapeDtypeStruct((TOK, LRU_WIDTH), F32)],
        compiler_params=_cparams(("parallel",)),
        name="l0_in_proj",
    )(x, g, mod, mod, w)


def _mla_proj_kernel(c_ref, qn_ref, kvn_ref, wq_ref, wk_ref, wv_ref, cos_ref, sin_ref, q_ref, k_ref, v_ref):
    c = c_ref[...]
    cos = cos_ref[...]
    sin = sin_ref[...]
    nq = _rms(c[:, :MLA_Q_RANK], qn_ref[...]).astype(BF16)
    q = _dot(nq, wq_ref[...]) * (LOG2E * (MLA_NOPE + MLA_ROPE) ** -0.5)
    nkv = _rms(c[:, MLA_Q_RANK:MLA_Q_RANK + MLA_KV_RANK], kvn_ref[...]).astype(BF16)
    kn = _dot(nkv, wk_ref[...])
    v_ref[...] = _dot(nkv, wv_ref[...]).astype(BF16)
    kr = _rope(c[:, MLA_Q_RANK + MLA_KV_RANK:], cos, sin, MLA_ROPE // 4).astype(BF16)
    for h in range(MLA_HEADS):
        lo = h * MLA_QK
        q_ref[:, lo:lo + LANES] = q[:, lo:lo + LANES].astype(BF16)
        q_ref[:, lo + LANES:lo + MLA_QK] = _rope(q[:, lo + LANES:lo + MLA_QK], cos, sin, MLA_ROPE // 4).astype(BF16)
        k_ref[:, lo:lo + LANES] = kn[:, h * LANES:(h + 1) * LANES].astype(BF16)
        k_ref[:, lo + LANES:lo + MLA_QK] = kr


def _mla_proj(cqkv, qn, kvn, wq, wk, wv, cos, sin):
    row = lambda i: (i, 0)
    pos = lambda i: (i % TPB, 0)
    hq = MLA_HEADS * MLA_QK
    hv = MLA_HEADS * MLA_V
    return pl.pallas_call(
        _mla_proj_kernel,
        grid=(TOK // TM,),
        in_specs=[pl.BlockSpec((TM, L0_CQKV), row), _resident((1, MLA_Q_RANK)), _resident((1, MLA_KV_RANK)),
                  _resident((MLA_Q_RANK, hq)), _resident((MLA_KV_RANK, hv)), _resident((MLA_KV_RANK, hv)),
                  pl.BlockSpec((TM, LANES), pos), pl.BlockSpec((TM, LANES), pos)],
        out_specs=[pl.BlockSpec((TM, hq), row), pl.BlockSpec((TM, hq), row), pl.BlockSpec((TM, hv), row)],
        out_shape=[jax.ShapeDtypeStruct((TOK, hq), BF16), jax.ShapeDtypeStruct((TOK, hq), BF16),
                   jax.ShapeDtypeStruct((TOK, hv), BF16)],
        compiler_params=_cparams(("parallel",)),
        name="mla_proj",
    )(cqkv, qn, kvn, wq, wk, wv, cos, sin)


MLA_TQ = 256


def _softmax_pv(s, v):
    m = jnp.max(s, axis=-1, keepdims=True)
    p = jnp.exp2(s - m)
    l = jnp.sum(p, axis=-1, keepdims=True)
    return _dot(p.astype(BF16), v) / l


def _mla_attn_kernel(q_ref, k_ref, v_ref, o_ref):
    s = _dot_nt(q_ref[0:CTX_LEN, :], k_ref[0:CTX_LEN, :])
    o_ref[0:CTX_LEN, :] = _softmax_pv(s, v_ref[0:CTX_LEN, :]).astype(o_ref.dtype)
    for t in range(SEQ // MLA_TQ):
        r0 = CTX_LEN + t * MLA_TQ
        s = _dot_nt(q_ref[r0:r0 + MLA_TQ, :], k_ref[...])
        o_ref[r0:r0 + MLA_TQ, :] = _softmax_pv(s, v_ref[...]).astype(o_ref.dtype)


def _mla_attn(q, k, v):
    blk = lambda b, h: (b, h)
    return pl.pallas_call(
        _mla_attn_kernel,
        grid=(BATCH, MLA_HEADS),
        in_specs=[pl.BlockSpec((NB, MLA_QK), blk), pl.BlockSpec((NB, MLA_QK), blk), pl.BlockSpec((NB, MLA_V), blk)],
        out_specs=pl.BlockSpec((NB, MLA_V), blk),
        out_shape=jax.ShapeDtypeStruct((TOK, MLA_HEADS * MLA_V), BF16),
        compiler_params=_cparams(("parallel", "parallel")),
        name="mla_attn",
    )(q, k, v)


LRU_BW = LRU_WIDTH // LRU_BLOCKS
CTX_GROUPS = CTX_LEN // SUBLANES
LAT_GROUPS = SEQ // SUBLANES


def _scan_group(a, b, reverse):
    row = lax.broadcasted_iota(jnp.int32, a.shape, 0)
    for d in (1, 2, 4):
        shift = SUBLANES - d if reverse else d
        a_s = pltpu.roll(a, shift, axis=0)
        b_s = pltpu.roll(b, shift, axis=0)
        m = (row < SUBLANES - d) if reverse else (row >= d)
        b = jnp.where(m, a * b_s + b, b)
        a = jnp.where(m, a * a_s, a)
    return a, b


def _rglru_kernel(xr_ref, gr_ref, cw_ref, cb_ref, wa_ref, ba_ref, wx_ref, bx_ref, lam_ref, y_ref,
                  af_ref, bf_ref, ab_ref, bb_ref, hf_ref, hb_ref):
    row8 = lax.broadcasted_iota(jnp.int32, (SUBLANES, LRU_BW), 0)

    def taps(seg):
        n = seg.shape[0]

        def shifted(shift, keep, first):
            r = pltpu.roll(seg, shift % n, axis=0)
            if first:
                return jnp.concatenate([jnp.where(keep, r[:SUBLANES], 0.0), r[SUBLANES:]], axis=0)
            return jnp.concatenate([r[:-SUBLANES], jnp.where(keep, r[-SUBLANES:], 0.0)], axis=0)

        return shifted(2, row8 >= 2, True), shifted(1, row8 >= 1, True), shifted(-1, row8 < SUBLANES - 1, False)

    x = xr_ref[...]
    tc = taps(x[:CTX_LEN])
    tl = taps(x[CTX_LEN:])
    xm2, xm1, xp1 = [jnp.concatenate([a, b], axis=0) for a, b in zip(tc, tl)]
    u = cb_ref[...] + xm2 * cw_ref[0:1, :] + xm1 * cw_ref[1:2, :] + x * cw_ref[2:3, :] + xp1 * cw_ref[3:4, :]
    ub = u.astype(BF16)
    for d, (a_ref, b_ref) in enumerate(((af_ref, bf_ref), (ab_ref, bb_ref))):
        r = _sigmoid(_dot(ub, wa_ref[d, 0].astype(BF16)) + ba_ref[d:d + 1, :])
        ig = _sigmoid(_dot(ub, wx_ref[d, 0].astype(BF16)) + bx_ref[d:d + 1, :])
        z = -lam_ref[d:d + 1, :]
        softplus = jnp.maximum(z, 0.0) + jnp.log(1.0 + jnp.exp(-jnp.abs(z)))
        log_a = -LRU_C * r * softplus
        a = jnp.exp(log_a)
        a_ref[...] = a
        t = 1.0 - a * a
        b_ref[...] = jnp.where(t > 0.0, t * lax.rsqrt(t), 0.0) * (ig * u)

    def step(gf, gb, hf, hb):
        rf = pl.multiple_of(gf * SUBLANES, SUBLANES)
        a, b = _scan_group(af_ref[pl.ds(rf, SUBLANES), :], bf_ref[pl.ds(rf, SUBLANES), :], False)
        h = a * hf + b
        hf_ref[pl.ds(rf, SUBLANES), :] = h
        hf = jnp.broadcast_to(h[SUBLANES - 1:SUBLANES, :], h.shape)
        rb = pl.multiple_of(gb * SUBLANES, SUBLANES)
        a, b = _scan_group(ab_ref[pl.ds(rb, SUBLANES), :], bb_ref[pl.ds(rb, SUBLANES), :], True)
        h = a * hb + b
        hb_ref[pl.ds(rb, SUBLANES), :] = h
        hb = jnp.broadcast_to(h[0:1, :], h.shape)
        return hf, hb

    zero = jnp.zeros((SUBLANES, LRU_BW), F32)
    carry = lax.fori_loop(0, CTX_GROUPS, lambda i, c: step(i, CTX_GROUPS - 1 - i, *c), (zero, zero), unroll=2)
    lax.fori_loop(0, LAT_GROUPS, lambda i, c: step(CTX_GROUPS + i, CTX_GROUPS + LAT_GROUPS - 1 - i, *c), carry, unroll=2)
    y_ref[...] = ((hf_ref[...] + hb_ref[...]) * jax.nn.gelu(gr_ref[...])).astype(y_ref.dtype)


def _rglru(xr, gr, conv_w, conv_b, wa, ba, wx, bx, lam):
    blk = lambda b, n: (b, n)
    col = lambda b, n: (0, n)
    gate = lambda b, n: (0, n, 0, 0)
    seg = pltpu.VMEM((NB, LRU_BW), F32)
    return pl.pallas_call(
        _rglru_kernel,
        grid=(BATCH, LRU_BLOCKS),
        in_specs=[pl.BlockSpec((NB, LRU_BW), blk), pl.BlockSpec((NB, LRU_BW), blk),
                  pl.BlockSpec((4, LRU_BW), col), pl.BlockSpec((1, LRU_BW), col),
                  pl.BlockSpec((2, 1, LRU_BW, LRU_BW), gate), pl.BlockSpec((2, LRU_BW), col),
                  pl.BlockSpec((2, 1, LRU_BW, LRU_BW), gate), pl.BlockSpec((2, LRU_BW), col),
                  pl.BlockSpec((2, LRU_BW), col)],
        out_specs=pl.BlockSpec((NB, LRU_BW), blk),
        out_shape=jax.ShapeDtypeStruct((TOK, LRU_WIDTH), BF16),
        scratch_shapes=[seg, seg, seg, seg, seg, seg],
        compiler_params=_cparams(("parallel", "parallel")),
        name="rglru",
    )(xr, gr, conv_w, conv_b.reshape(1, LRU_WIDTH), wa, ba, wx, bx, lam)


L1_Q = WIN_HEADS * HEAD_DIM
L1_KV = WIN_KV_HEADS * HEAD_DIM
L1_NA = NA_HEADS * HEAD_DIM
L1_IN = L1_Q + 2 * L1_KV + 3 * L1_NA


def _in1_kernel(x_ref, g_ref, sh_ref, sc_ref, w_ref, cos_ref, sin_ref,
                qw_ref, kw_ref, vw_ref, qn_ref, kn_ref, vn_ref):
    h = _rms(x_ref[...], g_ref[...]) * (1.0 + sc_ref[0]) + sh_ref[0]
    y = _dot(h.astype(BF16), w_ref[...])
    cos = cos_ref[...]
    sin = sin_ref[...]
    scale = LOG2E * HEAD_DIM ** -0.5
    for hd in range(WIN_HEADS):
        lo = hd * HEAD_DIM
        qw_ref[:, lo:lo + HEAD_DIM] = _rope(y[:, lo:lo + HEAD_DIM] * scale, cos, sin, HEAD_DIM // 4).astype(BF16)
    for hd in range(WIN_KV_HEADS):
        lo = hd * HEAD_DIM
        kw_ref[:, lo:lo + HEAD_DIM] = _rope(y[:, L1_Q + lo:L1_Q + lo + HEAD_DIM], cos, sin, HEAD_DIM // 4).astype(BF16)
    o = L1_Q + L1_KV
    vw_ref[...] = y[:, o:o + L1_KV].astype(BF16)
    o += L1_KV
    qn_ref[...] = (y[:, o:o + L1_NA] * scale).astype(BF16)
    kn_ref[...] = y[:, o + L1_NA:o + 2 * L1_NA].astype(BF16)
    vn_ref[...] = y[:, o + 2 * L1_NA:].astype(BF16)


def _in1(x, g, mod, w, cos, sin):
    row = lambda i: (i, 0)
    pos = lambda i: (i % TPB, 0)
    widths = (L1_Q, L1_KV, L1_KV, L1_NA, L1_NA, L1_NA)
    return pl.pallas_call(
        _in1_kernel,
        grid=(TOK // TM,),
        in_specs=[pl.BlockSpec((TM, D_MODEL), row), _resident((1, D_MODEL)),
                  _mod_spec(_comb_tile, 0), _mod_spec(_comb_tile, 1), _resident((D_MODEL, L1_IN)),
                  pl.BlockSpec((TM, LANES), pos), pl.BlockSpec((TM, LANES), pos)],
        out_specs=[pl.BlockSpec((TM, n), row) for n in widths],
        out_shape=[jax.ShapeDtypeStruct((TOK, n), BF16) for n in widths],
        compiler_params=_cparams(("parallel",)),
        name="l1_in_proj",
    )(x, g, mod, mod, w, cos, sin)


WIN_TQ = 128
WIN_SPAN = WIN_TQ + 2 * WINDOW
WIN_G = WIN_HEADS // WIN_KV_HEADS


def _win_key_start(n):
    return min(max((n - 1) * WIN_TQ, 0), SEQ - WIN_SPAN)


WIN_OFFSETS = sorted({n * WIN_TQ - _win_key_start(n) for n in range(SEQ // WIN_TQ)})


def _win_mask_bias():
    qoff = jnp.arange(WIN_G * WIN_TQ)[:, None] % WIN_TQ
    koff = jnp.arange(WIN_SPAN)[None, :]
    return jnp.stack([jnp.where(jnp.abs(qoff - koff + d) <= WINDOW, 0.0, NEG_INF) for d in WIN_OFFSETS]).astype(F32)


def _win_kernel(sink_ref, q_ref, k_ref, v_ref, mask_ref, o_ref):
    hk = pl.program_id(1)
    o_ref[0:CTX_LEN, :] = jnp.zeros((CTX_LEN, WIN_G * HEAD_DIM), o_ref.dtype)
    rows = WIN_G * WIN_TQ
    head = lax.broadcasted_iota(jnp.int32, (rows, 1), 0) // WIN_TQ
    sink = jnp.zeros((rows, 1), F32)
    for g in range(WIN_G):
        sink = jnp.where(head == g, sink_ref[hk * WIN_G + g] * LOG2E, sink)

    for n in range(SEQ // WIN_TQ):
        r0 = CTX_LEN + n * WIN_TQ
        start = _win_key_start(n)
        ks = CTX_LEN + start
        q4 = q_ref[r0:r0 + WIN_TQ, :]
        q = jnp.concatenate([q4[:, g * HEAD_DIM:(g + 1) * HEAD_DIM] for g in range(WIN_G)], axis=0)
        s_c = _dot_nt(q, k_ref[0:CTX_LEN, :])
        s_w = _dot_nt(q, k_ref[ks:ks + WIN_SPAN, :]) + mask_ref[WIN_OFFSETS.index(n * WIN_TQ - start)]
        m = jnp.maximum(jnp.maximum(jnp.max(s_c, axis=-1, keepdims=True), jnp.max(s_w, axis=-1, keepdims=True)), sink)
        p_c = jnp.exp2(s_c - m)
        p_w = jnp.exp2(s_w - m)
        l = jnp.sum(p_c, axis=-1, keepdims=True) + jnp.sum(p_w, axis=-1, keepdims=True) + jnp.exp2(sink - m)
        o = (_dot(p_w.astype(BF16), v_ref[ks:ks + WIN_SPAN, :]) + _dot(p_c.astype(BF16), v_ref[0:CTX_LEN, :])) / l
        for g in range(WIN_G):
            o_ref[r0:r0 + WIN_TQ, g * HEAD_DIM:(g + 1) * HEAD_DIM] = o[g * WIN_TQ:(g + 1) * WIN_TQ].astype(o_ref.dtype)


def _win_attn(sink, q, k, v):
    blk = lambda b, h, *_: (b, h)
    mask = _win_mask_bias()
    return pl.pallas_call(
        _win_kernel,
        grid_spec=pltpu.PrefetchScalarGridSpec(
            num_scalar_prefetch=1,
            grid=(BATCH, WIN_KV_HEADS),
            in_specs=[pl.BlockSpec((NB, WIN_G * HEAD_DIM), blk), pl.BlockSpec((NB, HEAD_DIM), blk),
                      pl.BlockSpec((NB, HEAD_DIM), blk), _resident(mask.shape)],
            out_specs=pl.BlockSpec((NB, WIN_G * HEAD_DIM), blk)),
        out_shape=jax.ShapeDtypeStruct((TOK, L1_Q), BF16),
        compiler_params=_cparams(("parallel", "parallel")),
        name="window_attn",
    )(sink, q, k, v, mask)


NA_GRID_ROWS = SEQ // GRID_W
NA_BAND = NA_ROWS * GRID_W


NA_RPI = 4


def _na_kernel(q_ref, k_ref, v_ref, bias_ref, o_ref):
    o_ref[0:CTX_LEN, :] = jnp.zeros((CTX_LEN, HEAD_DIM), o_ref.dtype)

    for i in range(NA_GRID_ROWS // NA_RPI):
        qs = CTX_LEN + i * NA_RPI * GRID_W
        q = q_ref[qs:qs + NA_RPI * GRID_W, :]
        s_c = _dot_nt(q, k_ref[0:CTX_LEN, :])
        starts = []
        s_w = []
        for j in range(NA_RPI):
            r = i * NA_RPI + j
            r0 = min(max(r - NA_ROWS // 2, 0), NA_GRID_ROWS - NA_ROWS)
            ks = CTX_LEN + r0 * GRID_W
            starts.append(ks)
            s_w.append(_dot_nt(q[j * GRID_W:(j + 1) * GRID_W], k_ref[ks:ks + NA_BAND, :])
                       + bias_ref[0, r0 - r + NA_ROWS - 1])
        s_w = jnp.concatenate(s_w, axis=0)
        m = jnp.maximum(jnp.max(s_c, axis=-1, keepdims=True), jnp.max(s_w, axis=-1, keepdims=True))
        p_c = jnp.exp2(s_c - m)
        p_w = jnp.exp2(s_w - m)
        l = jnp.sum(p_c, axis=-1, keepdims=True) + jnp.sum(p_w, axis=-1, keepdims=True)
        p_w = p_w.astype(BF16)
        o_w = jnp.concatenate([_dot(p_w[j * GRID_W:(j + 1) * GRID_W], v_ref[starts[j]:starts[j] + NA_BAND, :])
                               for j in range(NA_RPI)], axis=0)
        o = (o_w + _dot(p_c.astype(BF16), v_ref[0:CTX_LEN, :])) / l
        o_ref[qs:qs + NA_RPI * GRID_W, :] = o.astype(o_ref.dtype)


def _na_attn(q, k, v, bias):
    blk = lambda b, h: (b, h)
    return pl.pallas_call(
        _na_kernel,
        grid=(BATCH, NA_HEADS),
        in_specs=[pl.BlockSpec((NB, HEAD_DIM), blk), pl.BlockSpec((NB, HEAD_DIM), blk),
                  pl.BlockSpec((NB, HEAD_DIM), blk),
                  pl.BlockSpec((1, NA_ROWS, GRID_W, NA_BAND), lambda b, h: (h, 0, 0, 0))],
        out_specs=pl.BlockSpec((NB, HEAD_DIM), blk),
        out_shape=jax.ShapeDtypeStruct((TOK, L1_NA), BF16),
        compiler_params=_cparams(("parallel", "parallel")),
        name="na_attn",
    )(q, k, v, bias)


def _na_bias_table(rpb):
    cols = jnp.arange(GRID_W)
    c0 = jnp.clip(cols - NA_COLS // 2, 0, GRID_W - NA_COLS)
    kc = cols[None, :]
    valid = (kc >= c0[:, None]) & (kc < c0[:, None] + NA_COLS)
    lo = GRID_W - NA_COLS
    ext = jnp.pad(rpb.astype(F32) * LOG2E, ((0, 0), (0, 0), (lo, lo)))
    tbl = jnp.stack([ext[:, :, GRID_W - 1 - c:2 * GRID_W - 1 - c] for c in range(GRID_W)], axis=2)
    tbl = jnp.where(valid[None, None], tbl, NEG_INF)
    tbl = jnp.stack([tbl[:, d:d + NA_ROWS] for d in range(NA_ROWS)], axis=1)
    return jnp.transpose(tbl, (0, 1, 3, 2, 4)).reshape(NA_HEADS, NA_ROWS, GRID_W, NA_BAND)


def _router(h, rwh_ref, rwl_ref, rb_ref, tri_ref, carry):
    tm = h.shape[0]
    hh = h.astype(BF16)
    hl = (h - hh.astype(F32)).astype(BF16)
    logits = _dot_nt(rwh_ref[...], hh) + (_dot_nt(rwh_ref[...], hl) + _dot_nt(rwl_ref[...], hh))
    scores = jax.nn.sigmoid(logits)
    biased = scores + rb_ref[...]
    nj = EXPERTS_PER_GROUP
    s = [biased[j * N_GROUPS:(j + 1) * N_GROUPS] for j in range(nj)]
    u = [scores[j * N_GROUPS:(j + 1) * N_GROUPS] for j in range(nj)]
    gs = None
    for a in range(nj):
        for b in range(a + 1, nj):
            pair = s[a] + s[b]
            gs = pair if gs is None else jnp.maximum(gs, pair)
    giota = lax.broadcasted_iota(jnp.int32, (N_GROUPS, tm), 0).astype(F32)
    gmax = jnp.max(gs, axis=0, keepdims=True)
    gidx = jnp.min(jnp.where(gs == gmax, giota, float(N_GROUPS)), axis=0, keepdims=True)
    gm = giota == gidx
    v = [jnp.sum(jnp.where(gm, s[j], 0.0), axis=0, keepdims=True) for j in range(nj)]
    w = [jnp.sum(jnp.where(gm, u[j], 0.0), axis=0, keepdims=True) for j in range(nj)]
    sel = []
    for j in range(nj):
        beaten = jnp.zeros((1, tm), F32)
        for i in range(nj):
            if i != j:
                ahead = (v[i] >= v[j]) if i < j else (v[i] > v[j])
                beaten = beaten + jnp.where(ahead, 1.0, 0.0)
        sel.append(beaten < 2.0)
    wsum = sum(jnp.where(sel[j], w[j], 0.0) for j in range(nj))
    first = functools.reduce(jnp.minimum, [jnp.where(sel[j], float(j), float(nj)) for j in range(nj)])
    last = functools.reduce(jnp.maximum, [jnp.where(sel[j], float(j), -1.0) for j in range(nj)])
    gmf = jnp.where(gm, 1.0, 0.0)
    cnt = jnp.concatenate([jnp.where(sel[j], gmf, 0.0) for j in range(nj)], axis=0)
    pos = _dot(cnt.astype(BF16), tri_ref[...]) + carry
    rank = [jnp.sum(cnt[j * N_GROUPS:(j + 1) * N_GROUPS] * pos[j * N_GROUPS:(j + 1) * N_GROUPS], axis=0, keepdims=True)
            for j in range(nj)]
    pick = lambda which, vals: sum(jnp.where(which == float(j), vals[j], 0.0) for j in range(nj))
    gate = [w[j] / wsum for j in range(nj)]
    zero = jnp.zeros((1, tm), F32)
    info = jnp.concatenate(
        [gidx * nj + first, gidx * nj + last, pick(first, rank), pick(last, rank), pick(first, gate), pick(last, gate),
         zero, zero], axis=0)
    return info, carry + jnp.sum(cnt, axis=1, keepdims=True)


OUT_SUB = 128


def _out_kernel(ya_ref, yb_ref, x_ref, wa_ref, wb_ref, g1_ref, n_ref, sh_ref, sc_ref,
                rwh_ref, rwl_ref, rb_ref, tri_ref, xo_ref, h_ref, info_ref, counts_ref, carry_ref):
    @pl.when(pl.program_id(0) == 0)
    def _():
        carry_ref[...] = jnp.zeros_like(carry_ref)

    carry = carry_ref[:, 0:1]
    for r0 in range(0, TM, OUT_SUB):
        rows = slice(r0, r0 + OUT_SUB)
        y = _dot(ya_ref[rows, :], wa_ref[...]) + _dot(yb_ref[rows, :], wb_ref[...])
        x = x_ref[rows, :] + g1_ref[0] * y
        xo_ref[rows, :] = x
        h = _rms(x, n_ref[...]) * (1.0 + sc_ref[0]) + sh_ref[0]
        h_ref[rows, :] = h
        info_ref[:, rows], carry = _router(h, rwh_ref, rwl_ref, rb_ref, tri_ref, carry)
    counts = jnp.broadcast_to(carry, carry_ref.shape)
    carry_ref[...] = counts
    counts_ref[...] = counts


def _out_proj(ya, yb, x, w_out, mod, norm, rwh, rwl, rb, tri, tile_fn, n_tiles):
    half = w_out.shape[0] // 2
    src = lambda i: (tile_fn(i)[0], 0)
    dst = lambda i: (i, 0)
    n_tok = n_tiles * TM
    return pl.pallas_call(
        _out_kernel,
        grid=(n_tiles,),
        in_specs=[pl.BlockSpec((TM, half), src), pl.BlockSpec((TM, half), src), pl.BlockSpec((TM, D_MODEL), src),
                  _resident((half, D_MODEL)), _resident((half, D_MODEL)),
                  _mod_spec(tile_fn, 2), _resident((1, D_MODEL)), _mod_spec(tile_fn, 3), _mod_spec(tile_fn, 4),
                  _resident((N_EXPERTS, D_MODEL)), _resident((N_EXPERTS, D_MODEL)), _resident((N_EXPERTS, 1)),
                  _resident((OUT_SUB, OUT_SUB))],
        out_specs=[pl.BlockSpec((TM, D_MODEL), dst), pl.BlockSpec((TM, D_MODEL), dst),
                   pl.BlockSpec((SUBLANES, TM), lambda i: (0, i)),
                   pl.BlockSpec((N_EXPERTS, LANES), lambda i: (0, 0))],
        out_shape=[jax.ShapeDtypeStruct((n_tok, D_MODEL), F32), jax.ShapeDtypeStruct((n_tok, D_MODEL), F32),
                   jax.ShapeDtypeStruct((SUBLANES, n_tok), F32), jax.ShapeDtypeStruct((N_EXPERTS, LANES), F32)],
        scratch_shapes=[pltpu.VMEM((N_EXPERTS, LANES), F32)],
        compiler_params=_cparams(("arbitrary",)),
        name="out_proj_router",
    )(ya, yb, x, w_out[:half], w_out[half:], mod, norm, mod, mod, rwh, rwl, rb, tri)


def _lookup(table, idx):
    onehot = idx[..., None] == jnp.arange(table.shape[0], dtype=jnp.int32)
    return jnp.sum(jnp.where(onehot, table, 0), axis=-1)


def _dispatch_plan(info, counts, n_tok):
    n_steps = 2 * n_tok // TMX + N_EXPERTS - 1
    experts = jnp.arange(N_EXPERTS, dtype=jnp.int32)
    cnt = counts[:, 0].astype(jnp.int32).reshape(EXPERTS_PER_GROUP, N_GROUPS).T.reshape(N_EXPERTS)
    end = jnp.cumsum(cnt)
    off = end - cnt
    first_tile = off // TMX
    visits = jnp.where(cnt > 0, (end - 1) // TMX - first_tile + 1, 0)
    visit_end = jnp.cumsum(visits)
    n_valid = visit_end[-1]
    dest = _lookup(off, info[0:2].astype(jnp.int32)) + info[2:4].astype(jnp.int32)
    step = jnp.minimum(jnp.arange(n_steps, dtype=jnp.int32), n_valid - 1)
    e = jnp.sum(visit_end[None, :] <= step[:, None], axis=1).astype(jnp.int32)
    k = step - _lookup(visit_end - visits, e)
    tile = _lookup(first_tile, e) + k
    lo = jnp.clip(_lookup(off, e) - tile * TMX, 0, TMX)
    hi = jnp.clip(_lookup(end, e) - tile * TMX, 0, TMX)
    later = (experts[None, :] > experts[:, None]) & (cnt[None, :] > 0)
    nxt = jnp.min(jnp.where(later, experts[None, :], N_EXPERTS), axis=1)
    nxt = jnp.where(nxt == N_EXPERTS, -1, nxt)
    slot = (jnp.cumsum((cnt > 0).astype(jnp.int32)) - 1) % 2
    i32 = lambda v: v.astype(jnp.int32)
    return (dest.reshape(-1), i32(tile), i32(e), i32(n_valid.reshape(1)), i32(lo), i32(hi), i32(k == 0),
            i32(_lookup(nxt, e)), i32(_lookup(slot, e)))


def _dispatch_kernel(dest_ref, h_ref, xs_hbm, sem, *, n_tok):
    i = pl.program_id(0)

    def row_copy(r, d):
        return pltpu.make_async_copy(h_ref.at[pl.ds(r, 1)], xs_hbm.at[pl.ds(d, 1)], sem)

    def start(r, c):
        for k in range(2):
            row_copy(r, dest_ref[k * n_tok + i * TM + r]).start(priority=k)
        return c

    lax.fori_loop(0, TM, start, 0, unroll=8)

    def wait(r, c):
        for k in range(2):
            row_copy(r, 0).wait()
        return c

    lax.fori_loop(0, TM, wait, 0, unroll=8)


def _dispatch(dest, h):
    n_tok = h.shape[0]
    return pl.pallas_call(
        functools.partial(_dispatch_kernel, n_tok=n_tok),
        grid_spec=pltpu.PrefetchScalarGridSpec(
            num_scalar_prefetch=1,
            grid=(n_tok // TM,),
            in_specs=[pl.BlockSpec((TM, D_MODEL), lambda i, d: (i, 0))],
            out_specs=pl.BlockSpec(memory_space=pl.ANY),
            scratch_shapes=[pltpu.SemaphoreType.DMA(())]),
        out_shape=jax.ShapeDtypeStruct((2 * n_tok, D_MODEL), F32),
        compiler_params=_cparams(("arbitrary",)),
        name="moe_dispatch",
    )(dest, h)


def _moe_kernel(tile_ref, te_ref, nv_ref, lo_ref, hi_ref, first_ref, nxt_ref, slot_ref,
                xs_ref, wg_hbm, wu_hbm, wd_hbm, ys_ref, wg_f, wu_f, wd_f, wg_s, wu_s, wd_s, sem):
    s = pl.program_id(0)

    def fetch(e, slot):
        return [pltpu.make_async_copy(src.at[e], dst.at[slot], sem.at[slot])
                for src, dst in ((wg_hbm, wg_f), (wu_hbm, wu_f), (wd_hbm, wd_f))]

    @pl.when(s < nv_ref[0])
    def _():
        @pl.when(first_ref[s] == 1)
        def _():
            slot = slot_ref[s]

            @pl.when(s == 0)
            def _():
                for c in fetch(te_ref[0], slot):
                    c.start()

            for c in fetch(te_ref[s], slot):
                c.wait()

            @pl.when(nxt_ref[s] >= 0)
            def _():
                for c in fetch(nxt_ref[s], 1 - slot):
                    c.start()

            wg_s[...] = wg_f[slot].astype(BF16)
            wu_s[...] = wu_f[slot].astype(BF16)
            wd_s[...] = wd_f[slot].astype(BF16)

        lo = lo_ref[s]
        hi = hi_ref[s]

        @pl.when(lo == 0)
        def _():
            ys_ref[...] = jnp.zeros_like(ys_ref)

        def visit(r0, n):
            x = xs_ref[r0:r0 + n, :].astype(BF16)
            hg = _dot(x, wg_s[...])
            he = (hg * jax.nn.sigmoid(hg)) * _dot(x, wu_s[...])
            y = _dot(he.astype(BF16), wd_s[...])
            row = r0 + lax.broadcasted_iota(jnp.int32, (n, 1), 0)
            ys_ref[r0:r0 + n, :] = jnp.where((row >= lo) & (row < hi), y, ys_ref[r0:r0 + n, :])

        half = TMX // 2
        lower = hi <= half
        upper = lo >= half
        pl.when(lower)(lambda: visit(0, half))
        pl.when(upper)(lambda: visit(half, half))
        pl.when(jnp.logical_not(jnp.logical_or(lower, upper)))(lambda: visit(0, TMX))


def _moe(plan, xs, w_gate, w_up, w_down):
    n_steps = plan[0].shape[0]
    tile = lambda s, t, *_: (t[s], 0)
    any_spec = pl.BlockSpec(memory_space=pl.ANY)
    return pl.pallas_call(
        _moe_kernel,
        grid_spec=pltpu.PrefetchScalarGridSpec(
            num_scalar_prefetch=len(plan),
            grid=(n_steps,),
            in_specs=[pl.BlockSpec((TMX, D_MODEL), tile), any_spec, any_spec, any_spec],
            out_specs=pl.BlockSpec((TMX, D_MODEL), tile),
            scratch_shapes=[pltpu.VMEM((2, D_MODEL, EXPERT_FF), F32), pltpu.VMEM((2, D_MODEL, EXPERT_FF), F32),
                            pltpu.VMEM((2, EXPERT_FF, D_MODEL), F32),
                            pltpu.VMEM((D_MODEL, EXPERT_FF), BF16), pltpu.VMEM((D_MODEL, EXPERT_FF), BF16),
                            pltpu.VMEM((EXPERT_FF, D_MODEL), BF16), pltpu.SemaphoreType.DMA((2,))]),
        out_shape=jax.ShapeDtypeStruct(xs.shape, F32),
        compiler_params=_cparams(("arbitrary",)),
        name="moe_experts",
    )(*plan, xs, w_gate, w_up, w_down)


def _combine_kernel(dest_ref, ys_hbm, x_ref, w_ref, g2_ref, n_ref, o_ref, y_buf, sem, *, n_tok, final):
    i = pl.program_id(0)

    def row_copy(k, r, d):
        return pltpu.make_async_copy(ys_hbm.at[pl.ds(d, 1)], y_buf.at[k, pl.ds(r, 1)], sem)

    def start(r, c):
        for k in range(2):
            row_copy(k, r, dest_ref[k * n_tok + i * TM + r]).start(priority=k)
        return c

    lax.fori_loop(0, TM, start, 0, unroll=8)

    def wait(r, c):
        for k in range(2):
            row_copy(k, r, 0).wait()
        return c

    lax.fori_loop(0, TM, wait, 0, unroll=8)
    w = w_ref[...]
    x = x_ref[...] + g2_ref[0] * (w[:, 4:5] * y_buf[0] + w[:, 5:6] * y_buf[1])
    o_ref[...] = _rms(x, n_ref[...]) if final else x


def _combine(dest, ys, x, winfo, mod, norm, tile_fn, n_tiles, final):
    n_tok = n_tiles * TM
    row = lambda i, d: (i, 0)
    return pl.pallas_call(
        functools.partial(_combine_kernel, n_tok=n_tok, final=final),
        grid_spec=pltpu.PrefetchScalarGridSpec(
            num_scalar_prefetch=1,
            grid=(n_tiles,),
            in_specs=[pl.BlockSpec(memory_space=pl.ANY), pl.BlockSpec((TM, D_MODEL), row),
                      pl.BlockSpec((TM, SUBLANES), row), _mod_spec(tile_fn, 5),
                      pl.BlockSpec((1, D_MODEL), lambda i, d: (0, 0))],
            out_specs=pl.BlockSpec((TM, D_MODEL), row),
            scratch_shapes=[pltpu.VMEM((2, TM, D_MODEL), F32), pltpu.SemaphoreType.DMA(())]),
        out_shape=jax.ShapeDtypeStruct((n_tok, D_MODEL), F32),
        compiler_params=_cparams(("arbitrary",)),
        name="moe_combine",
    )(dest, ys, x, winfo, mod, norm)


def _moe_block(x, h, info, counts, mod, norm, w_gate, w_up, w_down, tile_fn, n_tiles, final):
    dest, *plan = _dispatch_plan(info, counts, n_tiles * TM)
    ys = _moe(plan, _dispatch(dest, h), w_gate, w_up, w_down)
    return _combine(dest, ys, x, info.T, mod, norm, tile_fn, n_tiles, final)


def _rope_tables(dim):
    half, quarter = dim // 2, dim // 4
    t = jnp.arange(SEQ)
    pos = jnp.stack([t // GRID_W, t % GRID_W], axis=-1).astype(F32)
    inv_freq = ROPE_THETA ** (-jnp.arange(0, half, 2, dtype=F32) / half)
    ang = pos[:, :, None] * inv_freq
    cos = jnp.cos(ang)
    sin = jnp.sin(ang)
    cos = jnp.concatenate([cos[:, 0], cos[:, 0], cos[:, 1], cos[:, 1]], axis=-1)
    sin = jnp.concatenate([-sin[:, 0], sin[:, 0], -sin[:, 1], sin[:, 1]], axis=-1)
    pad = LANES - dim
    cos = jnp.pad(cos, ((0, 0), (0, pad)))
    sin = jnp.pad(sin, ((0, 0), (0, pad)))
    ctx_cos = jnp.pad(jnp.ones((CTX_LEN, dim), F32), ((0, 0), (0, pad)))
    return jnp.concatenate([ctx_cos, cos], axis=0), jnp.concatenate([jnp.zeros((CTX_LEN, LANES), F32), sin], axis=0)


def kernel(x, c, ctx, c_ctx, router_w, router_b, final_norm, l0_mod_w, l0_mod_b, l0_norm_mix, l0_norm_ffn, l0_w_in, l0_q_norm, l0_w_uq, l0_kv_norm, l0_w_ukv, l0_conv_w, l0_conv_b, l0_gate_a_w, l0_gate_a_b, l0_gate_x_w, l0_gate_x_b, l0_lru_lambda, l0_w_out, l0_exp_gate, l0_exp_up, l0_exp_down, l1_mod_w, l1_mod_b, l1_norm_mix, l1_norm_ffn, l1_w_in, l1_sink, l1_rpb, l1_w_out, l1_exp_gate, l1_exp_up, l1_exp_down):
    row = lambda v: v.reshape(1, -1)
    xs = jnp.concatenate([ctx, x], axis=1).reshape(TOK, D_MODEL)

    cs = jnp.concatenate([c, c_ctx[None], jnp.zeros((SUBLANES - BATCH - 1, D_MODEL), F32)], axis=0)
    mod0 = _modulation(cs, l0_mod_w, l0_mod_b)
    mod1 = _modulation(cs, l1_mod_w, l1_mod_b)

    perm = jnp.arange(N_EXPERTS).reshape(N_GROUPS, EXPERTS_PER_GROUP).T.reshape(-1)
    rwt = router_w.T[perm]
    rwh = rwt.astype(BF16)
    rwl = (rwt - rwh.astype(F32)).astype(BF16)
    rb = router_b[perm].reshape(N_EXPERTS, 1).astype(F32)
    tri = jnp.triu(jnp.ones((OUT_SUB, OUT_SUB), F32), k=1).astype(BF16)
    router = (rwh, rwl, rb, tri)

    split = MLA_Q_RANK + MLA_KV_RANK + MLA_ROPE
    w_in0 = jnp.concatenate([l0_w_in[:, :split], jnp.zeros((D_MODEL, LANES - MLA_ROPE), F32), l0_w_in[:, split:]],
                            axis=1).astype(BF16)
    cqkv, xr, gr = _in0(xs, row(l0_norm_mix), mod0, w_in0)
    wq = jnp.pad(l0_w_uq.reshape(MLA_Q_RANK, MLA_HEADS, MLA_NOPE + MLA_ROPE),
                 ((0, 0), (0, 0), (0, MLA_QK - MLA_NOPE - MLA_ROPE))).reshape(MLA_Q_RANK, MLA_HEADS * MLA_QK)
    wkv = l0_w_ukv.reshape(MLA_KV_RANK, MLA_HEADS, MLA_NOPE + MLA_V)
    wk = wkv[:, :, :MLA_NOPE].reshape(MLA_KV_RANK, MLA_HEADS * MLA_NOPE)
    wv = wkv[:, :, MLA_NOPE:].reshape(MLA_KV_RANK, MLA_HEADS * MLA_V)
    cos0, sin0 = _rope_tables(MLA_ROPE)
    q, k, v = _mla_proj(cqkv, row(l0_q_norm), row(l0_kv_norm), wq.astype(BF16), wk.astype(BF16), wv.astype(BF16),
                        cos0, sin0)
    att = _mla_attn(q, k, v)
    rnn = _rglru(xr, gr, l0_conv_w, l0_conv_b, l0_gate_a_w, l0_gate_a_b, l0_gate_x_w, l0_gate_x_b, l0_lru_lambda)
    n0 = TOK // TM
    xs, h, info, counts = _out_proj(att, rnn, xs, l0_w_out.astype(BF16), mod0, row(l0_norm_ffn), *router,
                                    _comb_tile, n0)
    xs = _moe_block(xs, h, info, counts, mod0, row(final_norm), l0_exp_gate, l0_exp_up, l0_exp_down,
                    _comb_tile, n0, False)

    cos1, sin1 = _rope_tables(HEAD_DIM)
    qw, kw, vw, qn, kn, vn = _in1(xs, row(l1_norm_mix), mod1, l1_w_in.astype(BF16), cos1, sin1)
    win = _win_attn(l1_sink.astype(F32), qw, kw, vw)
    na = _na_attn(qn, kn, vn, _na_bias_table(l1_rpb))
    n1 = BATCH * LAT_TPB
    xl, h, info, counts = _out_proj(win, na, xs, l1_w_out.astype(BF16), mod1, row(l1_norm_ffn), *router,
                                    _lat_tile, n1)
    out = _moe_block(xl, h, info, counts, mod1, row(final_norm), l1_exp_gate, l1_exp_up, l1_exp_down,
                     lambda i: (i, i // LAT_TPB), n1, True)
    return out.reshape(BATCH, SEQ, D_MODEL)
```

```python
import functools

import jax
import jax.numpy as jnp
from jax import lax
from jax.experimental import pallas as pl
from jax.experimental.pallas import tpu as pltpu

F32 = jnp.float32
BF16 = jnp.bfloat16

D_MODEL = 2048
BATCH = 4
SEQ = 2048
GRID_W = 64
CTX_LEN = 256
EPS = 1e-6
NEG_INF = -1e30
ROPE_THETA = 10000.0
N_MOD = 6

MLA_HEADS = 8
MLA_Q_RANK = 512
MLA_KV_RANK = 256
MLA_NOPE = 128
MLA_ROPE = 64
MLA_V = 128

LRU_WIDTH = 1024
LRU_BLOCKS = 8
LRU_C = 8.0

HEAD_DIM = 128
WIN_HEADS = 8
WIN_KV_HEADS = 2
WINDOW = 128
NA_HEADS = 8
NA_ROWS = 8
NA_COLS = 16

N_EXPERTS = 32
N_GROUPS = 8
EXPERTS_PER_GROUP = 4
EXPERT_FF = 512

LANES = 128
SUBLANES = 8
VMEM_LIMIT = 56 * 1024 * 1024

NB = CTX_LEN + SEQ
TOK = BATCH * NB
TM = 256
TPB = NB // TM
LAT_TPB = SEQ // TM
TMX = 256
MLA_QK = 2 * LANES
LOG2E = 1.4426950408889634


def _cparams(sem):
    return pltpu.CompilerParams(dimension_semantics=sem, vmem_limit_bytes=VMEM_LIMIT)


def _resident(shape):
    nd = len(shape)
    return pl.BlockSpec(shape, lambda *_: (0,) * nd, pipeline_mode=pl.Buffered(1))


def _rms(x, g):
    return x * lax.rsqrt(jnp.mean(x * x, axis=-1, keepdims=True) + EPS) * g


def _sigmoid(x):
    return 0.5 * jnp.tanh(0.5 * x) + 0.5


def _dot(a, b):
    return jnp.dot(a, b, preferred_element_type=F32)


def _dot_nt(a, b):
    return lax.dot_general(a, b, (((1,), (1,)), ((), ())), preferred_element_type=F32)


def _swap_blocks(x, blk):
    lane = lax.broadcasted_iota(jnp.int32, x.shape, 1)
    nxt = pltpu.roll(x, LANES - blk, axis=1)
    prv = pltpu.roll(x, blk, axis=1)
    return jnp.where((lane % (2 * blk)) < blk, nxt, prv)


def _rope(x, cos, sin, blk):
    return x * cos + _swap_blocks(x, blk) * sin


def _comb_tile(i):
    b = i // TPB
    return i, jnp.where(i % TPB == 0, BATCH, b)


def _lat_tile(i):
    b = i // LAT_TPB
    return b * TPB + 1 + i % LAT_TPB, b


def _mod_spec(tile_fn, k):
    return pl.BlockSpec((1, 1, D_MODEL), lambda i, *_: (tile_fn(i)[1] * N_MOD + k, 0, 0))


def _mod_kernel(c_ref, w_ref, b_ref, o_ref):
    c = c_ref[...]
    a = (c * jax.nn.sigmoid(c)).astype(BF16)
    o_ref[...] = _dot(a, w_ref[...].astype(BF16)) + b_ref[...]


def _modulation(cs, w, b):
    n = N_MOD * D_MODEL
    tn = 1024
    out = pl.pallas_call(
        _mod_kernel,
        grid=(n // tn,),
        in_specs=[pl.BlockSpec((SUBLANES, D_MODEL), lambda j: (0, 0)),
                  pl.BlockSpec((D_MODEL, tn), lambda j: (0, j)),
                  pl.BlockSpec((1, tn), lambda j: (0, j))],
        out_specs=pl.BlockSpec((SUBLANES, tn), lambda j: (0, j)),
        out_shape=jax.ShapeDtypeStruct((SUBLANES, n), F32),
        compiler_params=_cparams(("arbitrary",)),
        name="modulation",
    )(cs, w, b.reshape(1, n))
    return out.reshape(SUBLANES * N_MOD, 1, D_MODEL)


L0_CQKV = MLA_Q_RANK + MLA_KV_RANK + LANES
L0_IN_PAD = L0_CQKV + 2 * LRU_WIDTH


def _in0_kernel(x_ref, g_ref, sh_ref, sc_ref, w_ref, cqkv_ref, xr_ref, gr_ref):
    h = _rms(x_ref[...], g_ref[...]) * (1.0 + sc_ref[0]) + sh_ref[0]
    y = _dot(h.astype(BF16), w_ref[...])
    cqkv_ref[...] = y[:, :L0_CQKV]
    xr_ref[...] = y[:, L0_CQKV:L0_CQKV + LRU_WIDTH]
    gr_ref[...] = y[:, L0_CQKV + LRU_WIDTH:]


def _in0(x, g, mod, w):
    row = lambda i: (i, 0)
    return pl.pallas_call(
        _in0_kernel,
        grid=(TOK // TM,),
        in_specs=[pl.BlockSpec((TM, D_MODEL), row), _resident((1, D_MODEL)),
                  _mod_spec(_comb_tile, 0), _mod_spec(_comb_tile, 1), _resident((D_MODEL, L0_IN_PAD))],
        out_specs=[pl.BlockSpec((TM, L0_CQKV), row), pl.BlockSpec((TM, LRU_WIDTH), row),
                   pl.BlockSpec((TM, LRU_WIDTH), row)],
        out_shape=[jax.ShapeDtypeStruct((TOK, L0_CQKV), F32), jax.ShapeDtypeStruct((TOK, LRU_WIDTH), F32),
                   jax.ShapeDtypeStruct((TOK, LRU_WIDTH), F32)],
        compiler_params=_cparams(("parallel",)),
        name="l0_in_proj",
    )(x, g, mod, mod, w)


def _mla_proj_kernel(c_ref, qn_ref, kvn_ref, wq_ref, wk_ref, wv_ref, cos_ref, sin_ref, q_ref, k_ref, v_ref):
    c = c_ref[...]
    cos = cos_ref[...]
    sin = sin_ref[...]
    nq = _rms(c[:, :MLA_Q_RANK], qn_ref[...]).astype(BF16)
    q = _dot(nq, wq_ref[...]) * (LOG2E * (MLA_NOPE + MLA_ROPE) ** -0.5)
    nkv = _rms(c[:, MLA_Q_RANK:MLA_Q_RANK + MLA_KV_RANK], kvn_ref[...]).astype(BF16)
    kn = _dot(nkv, wk_ref[...])
    v_ref[...] = _dot(nkv, wv_ref[...]).astype(BF16)
    kr = _rope(c[:, MLA_Q_RANK + MLA_KV_RANK:], cos, sin, MLA_ROPE // 4).astype(BF16)
    for h in range(MLA_HEADS):
        lo = h * MLA_QK
        q_ref[:, lo:lo + LANES] = q[:, lo:lo + LANES].astype(BF16)
        q_ref[:, lo + LANES:lo + MLA_QK] = _rope(q[:, lo + LANES:lo + MLA_QK], cos, sin, MLA_ROPE // 4).astype(BF16)
        k_ref[:, lo:lo + LANES] = kn[:, h * LANES:(h + 1) * LANES].astype(BF16)
        k_ref[:, lo + LANES:lo + MLA_QK] = kr


def _mla_proj(cqkv, qn, kvn, wq, wk, wv, cos, sin):
    row = lambda i: (i, 0)
    pos = lambda i: (i % TPB, 0)
    hq = MLA_HEADS * MLA_QK
    hv = MLA_HEADS * MLA_V
    return pl.pallas_call(
        _mla_proj_kernel,
        grid=(TOK // TM,),
        in_specs=[pl.BlockSpec((TM, L0_CQKV), row), _resident((1, MLA_Q_RANK)), _resident((1, MLA_KV_RANK)),
                  _resident((MLA_Q_RANK, hq)), _resident((MLA_KV_RANK, hv)), _resident((MLA_KV_RANK, hv)),
                  pl.BlockSpec((TM, LANES), pos), pl.BlockSpec((TM, LANES), pos)],
        out_specs=[pl.BlockSpec((TM, hq), row), pl.BlockSpec((TM, hq), row), pl.BlockSpec((TM, hv), row)],
        out_shape=[jax.ShapeDtypeStruct((TOK, hq), BF16), jax.ShapeDtypeStruct((TOK, hq), BF16),
                   jax.ShapeDtypeStruct((TOK, hv), BF16)],
        compiler_params=_cparams(("parallel",)),
        name="mla_proj",
    )(cqkv, qn, kvn, wq, wk, wv, cos, sin)


MLA_TQ = 256


def _softmax_pv(s, v):
    m = jnp.max(s, axis=-1, keepdims=True)
    p = jnp.exp2(s - m)
    l = jnp.sum(p, axis=-1, keepdims=True)
    return _dot(p.astype(BF16), v) / l


def _mla_attn_kernel(q_ref, k_ref, v_ref, o_ref):
    s = _dot_nt(q_ref[0:CTX_LEN, :], k_ref[0:CTX_LEN, :])
    o_ref[0:CTX_LEN, :] = _softmax_pv(s, v_ref[0:CTX_LEN, :]).astype(o_ref.dtype)
    for t in range(SEQ // MLA_TQ):
        r0 = CTX_LEN + t * MLA_TQ
        s = _dot_nt(q_ref[r0:r0 + MLA_TQ, :], k_ref[...])
        o_ref[r0:r0 + MLA_TQ, :] = _softmax_pv(s, v_ref[...]).astype(o_ref.dtype)


def _mla_attn(q, k, v):
    blk = lambda b, h: (b, h)
    return pl.pallas_call(
        _mla_attn_kernel,
        grid=(BATCH, MLA_HEADS),
        in_specs=[pl.BlockSpec((NB, MLA_QK), blk), pl.BlockSpec((NB, MLA_QK), blk), pl.BlockSpec((NB, MLA_V), blk)],
        out_specs=pl.BlockSpec((NB, MLA_V), blk),
        out_shape=jax.ShapeDtypeStruct((TOK, MLA_HEADS * MLA_V), BF16),
        compiler_params=_cparams(("parallel", "parallel")),
        name="mla_attn",
    )(q, k, v)


LRU_BW = LRU_WIDTH // LRU_BLOCKS
CTX_GROUPS = CTX_LEN // SUBLANES
LAT_GROUPS = SEQ // SUBLANES


def _scan_group(a, b, reverse):
    row = lax.broadcasted_iota(jnp.int32, a.shape, 0)
    for d in (1, 2, 4):
        shift = SUBLANES - d if reverse else d
        a_s = pltpu.roll(a, shift, axis=0)
        b_s = pltpu.roll(b, shift, axis=0)
        m = (row < SUBLANES - d) if reverse else (row >= d)
        b = jnp.where(m, a * b_s + b, b)
        a = jnp.where(m, a * a_s, a)
    return a, b


def _rglru_kernel(xr_ref, gr_ref, cw_ref, cb_ref, wa_ref, ba_ref, wx_ref, bx_ref, lam_ref, y_ref,
                  af_ref, bf_ref, ab_ref, bb_ref, hf_ref, hb_ref):
    row8 = lax.broadcasted_iota(jnp.int32, (SUBLANES, LRU_BW), 0)

    def taps(seg):
        n = seg.shape[0]

        def shifted(shift, keep, first):
            r = pltpu.roll(seg, shift % n, axis=0)
            if first:
                return jnp.concatenate([jnp.where(keep, r[:SUBLANES], 0.0), r[SUBLANES:]], axis=0)
            return jnp.concatenate([r[:-SUBLANES], jnp.where(keep, r[-SUBLANES:], 0.0)], axis=0)

        return shifted(2, row8 >= 2, True), shifted(1, row8 >= 1, True), shifted(-1, row8 < SUBLANES - 1, False)

    x = xr_ref[...]
    tc = taps(x[:CTX_LEN])
    tl = taps(x[CTX_LEN:])
    xm2, xm1, xp1 = [jnp.concatenate([a, b], axis=0) for a, b in zip(tc, tl)]
    u = cb_ref[...] + xm2 * cw_ref[0:1, :] + xm1 * cw_ref[1:2, :] + x * cw_ref[2:3, :] + xp1 * cw_ref[3:4, :]
    ub = u.astype(BF16)
    for d, (a_ref, b_ref) in enumerate(((af_ref, bf_ref), (ab_ref, bb_ref))):
        r = _sigmoid(_dot(ub, wa_ref[d, 0].astype(BF16)) + ba_ref[d:d + 1, :])
        ig = _sigmoid(_dot(ub, wx_ref[d, 0].astype(BF16)) + bx_ref[d:d + 1, :])
        z = -lam_ref[d:d + 1, :]
        softplus = jnp.maximum(z, 0.0) + jnp.log(1.0 + jnp.exp(-jnp.abs(z)))
        log_a = -LRU_C * r * softplus
        a = jnp.exp(log_a)
        a_ref[...] = a
        t = 1.0 - a * a
        b_ref[...] = jnp.where(t > 0.0, t * lax.rsqrt(t), 0.0) * (ig * u)

    def step(gf, gb, hf, hb):
        rf = pl.multiple_of(gf * SUBLANES, SUBLANES)
        a, b = _scan_group(af_ref[pl.ds(rf, SUBLANES), :], bf_ref[pl.ds(rf, SUBLANES), :], False)
        h = a * hf + b
        hf_ref[pl.ds(rf, SUBLANES), :] = h
        hf = jnp.broadcast_to(h[SUBLANES - 1:SUBLANES, :], h.shape)
        rb = pl.multiple_of(gb * SUBLANES, SUBLANES)
        a, b = _scan_group(ab_ref[pl.ds(rb, SUBLANES), :], bb_ref[pl.ds(rb, SUBLANES), :], True)
        h = a * hb + b
        hb_ref[pl.ds(rb, SUBLANES), :] = h
        hb = jnp.broadcast_to(h[0:1, :], h.shape)
        return hf, hb

    zero = jnp.zeros((SUBLANES, LRU_BW), F32)
    carry = lax.fori_loop(0, CTX_GROUPS, lambda i, c: step(i, CTX_GROUPS - 1 - i, *c), (zero, zero), unroll=2)
    lax.fori_loop(0, LAT_GROUPS, lambda i, c: step(CTX_GROUPS + i, CTX_GROUPS + LAT_GROUPS - 1 - i, *c), carry, unroll=2)
    y_ref[...] = ((hf_ref[...] + hb_ref[...]) * jax.nn.gelu(gr_ref[...])).astype(y_ref.dtype)


def _rglru(xr, gr, conv_w, conv_b, wa, ba, wx, bx, lam):
    blk = lambda b, n: (b, n)
    col = lambda b, n: (0, n)
    gate = lambda b, n: (0, n, 0, 0)
    seg = pltpu.VMEM((NB, LRU_BW), F32)
    return pl.pallas_call(
        _rglru_kernel,
        grid=(BATCH, LRU_BLOCKS),
        in_specs=[pl.BlockSpec((NB, LRU_BW), blk), pl.BlockSpec((NB, LRU_BW), blk),
                  pl.BlockSpec((4, LRU_BW), col), pl.BlockSpec((1, LRU_BW), col),
                  pl.BlockSpec((2, 1, LRU_BW, LRU_BW), gate), pl.BlockSpec((2, LRU_BW), col),
                  pl.BlockSpec((2, 1, LRU_BW, LRU_BW), gate), pl.BlockSpec((2, LRU_BW), col),
                  pl.BlockSpec((2, LRU_BW), col)],
        out_specs=pl.BlockSpec((NB, LRU_BW), blk),
        out_shape=jax.ShapeDtypeStruct((TOK, LRU_WIDTH), BF16),
        scratch_shapes=[seg, seg, seg, seg, seg, seg],
        compiler_params=_cparams(("parallel", "parallel")),
        name="rglru",
    )(xr, gr, conv_w, conv_b.reshape(1, LRU_WIDTH), wa, ba, wx, bx, lam)


L1_Q = WIN_HEADS * HEAD_DIM
L1_KV = WIN_KV_HEADS * HEAD_DIM
L1_NA = NA_HEADS * HEAD_DIM
L1_IN = L1_Q + 2 * L1_KV + 3 * L1_NA


def _in1_kernel(x_ref, g_ref, sh_ref, sc_ref, w_ref, cos_ref, sin_ref,
                qw_ref, kw_ref, vw_ref, qn_ref, kn_ref, vn_ref):
    h = _rms(x_ref[...], g_ref[...]) * (1.0 + sc_ref[0]) + sh_ref[0]
    y = _dot(h.astype(BF16), w_ref[...])
    cos = cos_ref[...]
    sin = sin_ref[...]
    scale = LOG2E * HEAD_DIM ** -0.5
    for hd in range(WIN_HEADS):
        lo = hd * HEAD_DIM
        qw_ref[:, lo:lo + HEAD_DIM] = _rope(y[:, lo:lo + HEAD_DIM] * scale, cos, sin, HEAD_DIM // 4).astype(BF16)
    for hd in range(WIN_KV_HEADS):
        lo = hd * HEAD_DIM
        kw_ref[:, lo:lo + HEAD_DIM] = _rope(y[:, L1_Q + lo:L1_Q + lo + HEAD_DIM], cos, sin, HEAD_DIM // 4).astype(BF16)
    o = L1_Q + L1_KV
    vw_ref[...] = y[:, o:o + L1_KV].astype(BF16)
    o += L1_KV
    qn_ref[...] = (y[:, o:o + L1_NA] * scale).astype(BF16)
    kn_ref[...] = y[:, o + L1_NA:o + 2 * L1_NA].astype(BF16)
    vn_ref[...] = y[:, o + 2 * L1_NA:].astype(BF16)


def _in1(x, g, mod, w, cos, sin):
    row = lambda i: (i, 0)
    pos = lambda i: (i % TPB, 0)
    widths = (L1_Q, L1_KV, L1_KV, L1_NA, L1_NA, L1_NA)
    return pl.pallas_call(
        _in1_kernel,
        grid=(TOK // TM,),
        in_specs=[pl.BlockSpec((TM, D_MODEL), row), _resident((1, D_MODEL)),
                  _mod_spec(_comb_tile, 0), _mod_spec(_comb_tile, 1), _resident((D_MODEL, L1_IN)),
                  pl.BlockSpec((TM, LANES), pos), pl.BlockSpec((TM, LANES), pos)],
        out_specs=[pl.BlockSpec((TM, n), row) for n in widths],
        out_shape=[jax.ShapeDtypeStruct((TOK, n), BF16) for n in widths],
        compiler_params=_cparams(("parallel",)),
        name="l1_in_proj",
    )(x, g, mod, mod, w, cos, sin)


WIN_TQ = 128
WIN_SPAN = WIN_TQ + 2 * WINDOW
WIN_G = WIN_HEADS // WIN_KV_HEADS


def _win_key_start(n):
    return min(max((n - 1) * WIN_TQ, 0), SEQ - WIN_SPAN)


WIN_OFFSETS = sorted({n * WIN_TQ - _win_key_start(n) for n in range(SEQ // WIN_TQ)})


def _win_mask_bias():
    qoff = jnp.arange(WIN_G * WIN_TQ)[:, None] % WIN_TQ
    koff = jnp.arange(WIN_SPAN)[None, :]
    return jnp.stack([jnp.where(jnp.abs(qoff - koff + d) <= WINDOW, 0.0, NEG_INF) for d in WIN_OFFSETS]).astype(F32)


def _win_kernel(sink_ref, q_ref, k_ref, v_ref, mask_ref, o_ref):
    hk = pl.program_id(1)
    o_ref[0:CTX_LEN, :] = jnp.zeros((CTX_LEN, WIN_G * HEAD_DIM), o_ref.dtype)
    rows = WIN_G * WIN_TQ
    head = lax.broadcasted_iota(jnp.int32, (rows, 1), 0) // WIN_TQ
    sink = jnp.zeros((rows, 1), F32)
    for g in range(WIN_G):
        sink = jnp.where(head == g, sink_ref[hk * WIN_G + g] * LOG2E, sink)

    for n in range(SEQ // WIN_TQ):
        r0 = CTX_LEN + n * WIN_TQ
        start = _win_key_start(n)
        ks = CTX_LEN + start
        q4 = q_ref[r0:r0 + WIN_TQ, :]
        q = jnp.concatenate([q4[:, g * HEAD_DIM:(g + 1) * HEAD_DIM] for g in range(WIN_G)], axis=0)
        s_c = _dot_nt(q, k_ref[0:CTX_LEN, :])
        s_w = _dot_nt(q, k_ref[ks:ks + WIN_SPAN, :]) + mask_ref[WIN_OFFSETS.index(n * WIN_TQ - start)]
        m = jnp.maximum(jnp.maximum(jnp.max(s_c, axis=-1, keepdims=True), jnp.max(s_w, axis=-1, keepdims=True)), sink)
        p_c = jnp.exp2(s_c - m)
        p_w = jnp.exp2(s_w - m)
        l = jnp.sum(p_c, axis=-1, keepdims=True) + jnp.sum(p_w, axis=-1, keepdims=True) + jnp.exp2(sink - m)
        o = (_dot(p_w.astype(BF16), v_ref[ks:ks + WIN_SPAN, :]) + _dot(p_c.astype(BF16), v_ref[0:CTX_LEN, :])) / l
        for g in range(WIN_G):
            o_ref[r0:r0 + WIN_TQ, g * HEAD_DIM:(g + 1) * HEAD_DIM] = o[g * WIN_TQ:(g + 1) * WIN_TQ].astype(o_ref.dtype)


def _win_attn(sink, q, k, v):
    blk = lambda b, h, *_: (b, h)
    mask = _win_mask_bias()
    return pl.pallas_call(
        _win_kernel,
        grid_spec=pltpu.PrefetchScalarGridSpec(
            num_scalar_prefetch=1,
            grid=(BATCH, WIN_KV_HEADS),
            in_specs=[pl.BlockSpec((NB, WIN_G * HEAD_DIM), blk), pl.BlockSpec((NB, HEAD_DIM), blk),
                      pl.BlockSpec((NB, HEAD_DIM), blk), _resident(mask.shape)],
            out_specs=pl.BlockSpec((NB, WIN_G * HEAD_DIM), blk)),
        out_shape=jax.ShapeDtypeStruct((TOK, L1_Q), BF16),
        compiler_params=_cparams(("parallel", "parallel")),
        name="window_attn",
    )(sink, q, k, v, mask)


NA_GRID_ROWS = SEQ // GRID_W
NA_BAND = NA_ROWS * GRID_W


NA_RPI = 4


def _na_kernel(q_ref, k_ref, v_ref, bias_ref, o_ref):
    o_ref[0:CTX_LEN, :] = jnp.zeros((CTX_LEN, HEAD_DIM), o_ref.dtype)

    for i in range(NA_GRID_ROWS // NA_RPI):
        qs = CTX_LEN + i * NA_RPI * GRID_W
        q = q_ref[qs:qs + NA_RPI * GRID_W, :]
        s_c = _dot_nt(q, k_ref[0:CTX_LEN, :])
        starts = []
        s_w = []
        for j in range(NA_RPI):
            r = i * NA_RPI + j
            r0 = min(max(r - NA_ROWS // 2, 0), NA_GRID_ROWS - NA_ROWS)
            ks = CTX_LEN + r0 * GRID_W
            starts.append(ks)
            s_w.append(_dot_nt(q[j * GRID_W:(j + 1) * GRID_W], k_ref[ks:ks + NA_BAND, :])
                       + bias_ref[0, r0 - r + NA_ROWS - 1])
        s_w = jnp.concatenate(s_w, axis=0)
        m = jnp.maximum(jnp.max(s_c, axis=-1, keepdims=True), jnp.max(s_w, axis=-1, keepdims=True))
        p_c = jnp.exp2(s_c - m)
        p_w = jnp.exp2(s_w - m)
        l = jnp.sum(p_c, axis=-1, keepdims=True) + jnp.sum(p_w, axis=-1, keepdims=True)
        p_w = p_w.astype(BF16)
        o_w = jnp.concatenate([_dot(p_w[j * GRID_W:(j + 1) * GRID_W], v_ref[starts[j]:starts[j] + NA_BAND, :])
                               for j in range(NA_RPI)], axis=0)
        o = (o_w + _dot(p_c.astype(BF16), v_ref[0:CTX_LEN, :])) / l
        o_ref[qs:qs + NA_RPI * GRID_W, :] = o.astype(o_ref.dtype)


def _na_attn(q, k, v, bias):
    blk = lambda b, h: (b, h)
    return pl.pallas_call(
        _na_kernel,
        grid=(BATCH, NA_HEADS),
        in_specs=[pl.BlockSpec((NB, HEAD_DIM), blk), pl.BlockSpec((NB, HEAD_DIM), blk),
                  pl.BlockSpec((NB, HEAD_DIM), blk),
                  pl.BlockSpec((1, NA_ROWS, GRID_W, NA_BAND), lambda b, h: (h, 0, 0, 0))],
        out_specs=pl.BlockSpec((NB, HEAD_DIM), blk),
        out_shape=jax.ShapeDtypeStruct((TOK, L1_NA), BF16),
        compiler_params=_cparams(("parallel", "parallel")),
        name="na_attn",
    )(q, k, v, bias)


def _na_bias_table(rpb):
    cols = jnp.arange(GRID_W)
    c0 = jnp.clip(cols - NA_COLS // 2, 0, GRID_W - NA_COLS)
    kc = cols[None, :]
    valid = (kc >= c0[:, None]) & (kc < c0[:, None] + NA_COLS)
    lo = GRID_W - NA_COLS
    ext = jnp.pad(rpb.astype(F32) * LOG2E, ((0, 0), (0, 0), (lo, lo)))
    tbl = jnp.stack([ext[:, :, GRID_W - 1 - c:2 * GRID_W - 1 - c] for c in range(GRID_W)], axis=2)
    tbl = jnp.where(valid[None, None], tbl, NEG_INF)
    tbl = jnp.stack([tbl[:, d:d + NA_ROWS] for d in range(NA_ROWS)], axis=1)
    return jnp.transpose(tbl, (0, 1, 3, 2, 4)).reshape(NA_HEADS, NA_ROWS, GRID_W, NA_BAND)


def _router_logits(h, rwh_ref, rwl_ref):
    hh = h.astype(BF16)
    hl = (h - hh.astype(F32)).astype(BF16)
    return _dot_nt(rwh_ref[...], hh) + (_dot_nt(rwh_ref[...], hl) + _dot_nt(rwl_ref[...], hh))


def _router(logits, rb_ref, tri_ref, carry):
    tm = logits.shape[1]
    scores = jax.nn.sigmoid(logits)
    biased = scores + rb_ref[...]
    nj = EXPERTS_PER_GROUP
    s = [biased[j * N_GROUPS:(j + 1) * N_GROUPS] for j in range(nj)]
    u = [scores[j * N_GROUPS:(j + 1) * N_GROUPS] for j in range(nj)]
    gs = None
    for a in range(nj):
        for b in range(a + 1, nj):
            pair = s[a] + s[b]
            gs = pair if gs is None else jnp.maximum(gs, pair)
    giota = lax.broadcasted_iota(jnp.int32, (N_GROUPS, tm), 0).astype(F32)
    gmax = jnp.max(gs, axis=0, keepdims=True)
    gidx = jnp.min(jnp.where(gs == gmax, giota, float(N_GROUPS)), axis=0, keepdims=True)
    gm = giota == gidx
    v = [jnp.sum(jnp.where(gm, s[j], 0.0), axis=0, keepdims=True) for j in range(nj)]
    w = [jnp.sum(jnp.where(gm, u[j], 0.0), axis=0, keepdims=True) for j in range(nj)]
    sel = []
    for j in range(nj):
        beaten = jnp.zeros((1, tm), F32)
        for i in range(nj):
            if i != j:
                ahead = (v[i] >= v[j]) if i < j else (v[i] > v[j])
                beaten = beaten + jnp.where(ahead, 1.0, 0.0)
        sel.append(beaten < 2.0)
    wsum = sum(jnp.where(sel[j], w[j], 0.0) for j in range(nj))
    first = functools.reduce(jnp.minimum, [jnp.where(sel[j], float(j), float(nj)) for j in range(nj)])
    last = functools.reduce(jnp.maximum, [jnp.where(sel[j], float(j), -1.0) for j in range(nj)])
    gmf = jnp.where(gm, 1.0, 0.0)
    cnt = jnp.concatenate([jnp.where(sel[j], gmf, 0.0) for j in range(nj)], axis=0)
    pos = _dot(cnt.astype(BF16), tri_ref[...]) + carry
    rank = [jnp.sum(cnt[j * N_GROUPS:(j + 1) * N_GROUPS] * pos[j * N_GROUPS:(j + 1) * N_GROUPS], axis=0, keepdims=True)
            for j in range(nj)]
    pick = lambda which, vals: sum(jnp.where(which == float(j), vals[j], 0.0) for j in range(nj))
    gate = [w[j] / wsum for j in range(nj)]
    zero = jnp.zeros((1, tm), F32)
    info = jnp.concatenate(
        [gidx * nj + first, gidx * nj + last, pick(first, rank), pick(last, rank), pick(first, gate), pick(last, gate),
         zero, zero], axis=0)
    return info, carry + jnp.sum(cnt, axis=1, keepdims=True)


OUT_SUB = 128


def _out_kernel(ya_ref, yb_ref, x_ref, wa_ref, wb_ref, g1_ref, n_ref, sh_ref, sc_ref,
                rwh_ref, rwl_ref, rb_ref, tri_ref, xo_ref, h_ref, info_ref, counts_ref, carry_ref, hbuf_ref):
    i = pl.program_id(0)

    @pl.when(i == 0)
    def _():
        carry_ref[...] = jnp.zeros_like(carry_ref)
        hbuf_ref[1] = jnp.zeros((TM, D_MODEL), F32)

    subs = range(0, TM, OUT_SUB)
    logits = [_router_logits(hbuf_ref[(i + 1) % 2, r0:r0 + OUT_SUB, :], rwh_ref, rwl_ref) for r0 in subs]
    y = _dot(ya_ref[...], wa_ref[...]) + _dot(yb_ref[...], wb_ref[...])
    old = carry_ref[:, 0:1]
    carry = old
    for r0, lg in zip(subs, logits):
        info_ref[:, r0:r0 + OUT_SUB], carry = _router(lg, rb_ref, tri_ref, carry)
    counts = jnp.broadcast_to(jnp.where(i > 0, carry, old), carry_ref.shape)
    carry_ref[...] = counts
    counts_ref[...] = counts

    x = x_ref[...] + g1_ref[0] * y
    xo_ref[...] = x
    h = _rms(x, n_ref[...]) * (1.0 + sc_ref[0]) + sh_ref[0]
    h_ref[...] = h
    hbuf_ref[i % 2] = h


def _out_proj(ya, yb, x, w_out, mod, norm, rwh, rwl, rb, tri, tile_fn, n_tiles):
    half = w_out.shape[0] // 2
    this = lambda i: jnp.minimum(i, n_tiles - 1)
    tile = lambda i: tile_fn(this(i))
    src = lambda i: (tile(i)[0], 0)
    dst = lambda i: (this(i), 0)
    n_tok = n_tiles * TM
    return pl.pallas_call(
        _out_kernel,
        grid=(n_tiles + 1,),
        in_specs=[pl.BlockSpec((TM, half), src), pl.BlockSpec((TM, half), src), pl.BlockSpec((TM, D_MODEL), src),
                  _resident((half, D_MODEL)), _resident((half, D_MODEL)),
                  _mod_spec(tile, 2), _resident((1, D_MODEL)), _mod_spec(tile, 3), _mod_spec(tile, 4),
                  _resident((N_EXPERTS, D_MODEL)), _resident((N_EXPERTS, D_MODEL)), _resident((N_EXPERTS, 1)),
                  _resident((OUT_SUB, OUT_SUB))],
        out_specs=[pl.BlockSpec((TM, D_MODEL), dst), pl.BlockSpec((TM, D_MODEL), dst),
                   pl.BlockSpec((SUBLANES, TM), lambda i: (0, jnp.maximum(i - 1, 0))),
                   pl.BlockSpec((N_EXPERTS, LANES), lambda i: (0, 0))],
        out_shape=[jax.ShapeDtypeStruct((n_tok, D_MODEL), F32), jax.ShapeDtypeStruct((n_tok, D_MODEL), F32),
                   jax.ShapeDtypeStruct((SUBLANES, n_tok), F32), jax.ShapeDtypeStruct((N_EXPERTS, LANES), F32)],
        scratch_shapes=[pltpu.VMEM((N_EXPERTS, LANES), F32), pltpu.VMEM((2, TM, D_MODEL), F32)],
        compiler_params=_cparams(("arbitrary",)),
        name="out_proj_router",
    )(ya, yb, x, w_out[:half], w_out[half:], mod, norm, mod, mod, rwh, rwl, rb, tri)


def _lookup(table, idx):
    onehot = idx[..., None] == jnp.arange(table.shape[0], dtype=jnp.int32)
    return jnp.sum(jnp.where(onehot, table, 0), axis=-1)


def _dispatch_plan(info, counts, n_tok):
    n_steps = 2 * n_tok // TMX + N_EXPERTS - 1
    experts = jnp.arange(N_EXPERTS, dtype=jnp.int32)
    cnt = counts[:, 0].astype(jnp.int32).reshape(EXPERTS_PER_GROUP, N_GROUPS).T.reshape(N_EXPERTS)
    end = jnp.cumsum(cnt)
    off = end - cnt
    first_tile = off // TMX
    visits = jnp.where(cnt > 0, (end - 1) // TMX - first_tile + 1, 0)
    visit_end = jnp.cumsum(visits)
    n_valid = visit_end[-1]
    dest = _lookup(off, info[0:2].astype(jnp.int32)) + info[2:4].astype(jnp.int32)
    step = jnp.minimum(jnp.arange(n_steps, dtype=jnp.int32), n_valid - 1)
    e = jnp.sum(visit_end[None, :] <= step[:, None], axis=1).astype(jnp.int32)
    k = step - _lookup(visit_end - visits, e)
    tile = _lookup(first_tile, e) + k
    lo = jnp.clip(_lookup(off, e) - tile * TMX, 0, TMX)
    hi = jnp.clip(_lookup(end, e) - tile * TMX, 0, TMX)
    later = (experts[None, :] > experts[:, None]) & (cnt[None, :] > 0)
    nxt = jnp.min(jnp.where(later, experts[None, :], N_EXPERTS), axis=1)
    nxt = jnp.where(nxt == N_EXPERTS, -1, nxt)
    slot = (jnp.cumsum((cnt > 0).astype(jnp.int32)) - 1) % 2
    i32 = lambda v: v.astype(jnp.int32)
    return (dest.reshape(-1), i32(tile), i32(e), i32(n_valid.reshape(1)), i32(lo), i32(hi), i32(k == 0),
            i32(_lookup(nxt, e)), i32(_lookup(slot, e)))


def _dispatch_kernel(dest_ref, h_ref, xs_hbm, sem, *, n_tok):
    i = pl.program_id(0)

    def row_copy(r, d):
        return pltpu.make_async_copy(h_ref.at[pl.ds(r, 1)], xs_hbm.at[pl.ds(d, 1)], sem)

    def start(r, c):
        for k in range(2):
            row_copy(r, dest_ref[k * n_tok + i * TM + r]).start(priority=k)
        return c

    lax.fori_loop(0, TM, start, 0, unroll=8)

    def wait(r, c):
        for k in range(2):
            row_copy(r, 0).wait()
        return c

    lax.fori_loop(0, TM, wait, 0, unroll=8)


def _dispatch(dest, h):
    n_tok = h.shape[0]
    return pl.pallas_call(
        functools.partial(_dispatch_kernel, n_tok=n_tok),
        grid_spec=pltpu.PrefetchScalarGridSpec(
            num_scalar_prefetch=1,
            grid=(n_tok // TM,),
            in_specs=[pl.BlockSpec((TM, D_MODEL), lambda i, d: (i, 0))],
            out_specs=pl.BlockSpec(memory_space=pl.ANY),
            scratch_shapes=[pltpu.SemaphoreType.DMA(())]),
        out_shape=jax.ShapeDtypeStruct((2 * n_tok, D_MODEL), F32),
        compiler_params=_cparams(("arbitrary",)),
        name="moe_dispatch",
    )(dest, h)


def _moe_kernel(tile_ref, te_ref, nv_ref, lo_ref, hi_ref, first_ref, nxt_ref, slot_ref,
                xs_ref, wg_hbm, wu_hbm, wd_hbm, ys_ref, wg_f, wu_f, wd_f, wg_s, wu_s, wd_s, sem):
    s = pl.program_id(0)

    def fetch(e, slot):
        return [pltpu.make_async_copy(src.at[e], dst.at[slot], sem.at[slot])
                for src, dst in ((wg_hbm, wg_f), (wu_hbm, wu_f), (wd_hbm, wd_f))]

    @pl.when(s < nv_ref[0])
    def _():
        @pl.when(first_ref[s] == 1)
        def _():
            slot = slot_ref[s]

            @pl.when(s == 0)
            def _():
                for c in fetch(te_ref[0], slot):
                    c.start()

            for c in fetch(te_ref[s], slot):
                c.wait()

            @pl.when(nxt_ref[s] >= 0)
            def _():
                for c in fetch(nxt_ref[s], 1 - slot):
                    c.start()

            wg_s[...] = wg_f[slot].astype(BF16)
            wu_s[...] = wu_f[slot].astype(BF16)
            wd_s[...] = wd_f[slot].astype(BF16)

        lo = lo_ref[s]
        hi = hi_ref[s]

        @pl.when(lo == 0)
        def _():
            ys_ref[...] = jnp.zeros_like(ys_ref)

        def visit(r0, n):
            x = xs_ref[r0:r0 + n, :].astype(BF16)
            hg = _dot(x, wg_s[...])
            he = (hg * jax.nn.sigmoid(hg)) * _dot(x, wu_s[...])
            y = _dot(he.astype(BF16), wd_s[...])
            row = r0 + lax.broadcasted_iota(jnp.int32, (n, 1), 0)
            ys_ref[r0:r0 + n, :] = jnp.where((row >= lo) & (row < hi), y, ys_ref[r0:r0 + n, :])

        half = TMX // 2
        lower = hi <= half
        upper = lo >= half
        pl.when(lower)(lambda: visit(0, half))
        pl.when(upper)(lambda: visit(half, half))
        pl.when(jnp.logical_not(jnp.logical_or(lower, upper)))(lambda: visit(0, TMX))


def _moe(plan, xs, w_gate, w_up, w_down):
    n_steps = plan[0].shape[0]
    tile = lambda s, t, *_: (t[s], 0)
    any_spec = pl.BlockSpec(memory_space=pl.ANY)
    return pl.pallas_call(
        _moe_kernel,
        grid_spec=pltpu.PrefetchScalarGridSpec(
            num_scalar_prefetch=len(plan),
            grid=(n_steps,),
            in_specs=[pl.BlockSpec((TMX, D_MODEL), tile), any_spec, any_spec, any_spec],
            out_specs=pl.BlockSpec((TMX, D_MODEL), tile),
            scratch_shapes=[pltpu.VMEM((2, D_MODEL, EXPERT_FF), F32), pltpu.VMEM((2, D_MODEL, EXPERT_FF), F32),
                            pltpu.VMEM((2, EXPERT_FF, D_MODEL), F32),
                            pltpu.VMEM((D_MODEL, EXPERT_FF), BF16), pltpu.VMEM((D_MODEL, EXPERT_FF), BF16),
                            pltpu.VMEM((EXPERT_FF, D_MODEL), BF16), pltpu.SemaphoreType.DMA((2,))]),
        out_shape=jax.ShapeDtypeStruct(xs.shape, F32),
        compiler_params=_cparams(("arbitrary",)),
        name="moe_experts",
    )(*plan, xs, w_gate, w_up, w_down)


def _combine_kernel(dest_ref, ys_hbm, x_ref, w_ref, g2_ref, n_ref, o_ref, y_buf, sem, *, n_tok, final):
    i = pl.program_id(0)

    def row_copy(k, r, d):
        return pltpu.make_async_copy(ys_hbm.at[pl.ds(d, 1)], y_buf.at[k, pl.ds(r, 1)], sem)

    def start(r, c):
        for k in range(2):
            row_copy(k, r, dest_ref[k * n_tok + i * TM + r]).start(priority=k)
        return c

    lax.fori_loop(0, TM, start, 0, unroll=8)

    def wait(r, c):
        for k in range(2):
            row_copy(k, r, 0).wait()
        return c

    lax.fori_loop(0, TM, wait, 0, unroll=8)
    w = w_ref[...]
    x = x_ref[...] + g2_ref[0] * (w[:, 4:5] * y_buf[0] + w[:, 5:6] * y_buf[1])
    o_ref[...] = _rms(x, n_ref[...]) if final else x


def _combine(dest, ys, x, winfo, mod, norm, tile_fn, n_tiles, final):
    n_tok = n_tiles * TM
    row = lambda i, d: (i, 0)
    return pl.pallas_call(
        functools.partial(_combine_kernel, n_tok=n_tok, final=final),
        grid_spec=pltpu.PrefetchScalarGridSpec(
            num_scalar_prefetch=1,
            grid=(n_tiles,),
            in_specs=[pl.BlockSpec(memory_space=pl.ANY), pl.BlockSpec((TM, D_MODEL), row),
                      pl.BlockSpec((TM, SUBLANES), row), _mod_spec(tile_fn, 5),
                      pl.BlockSpec((1, D_MODEL), lambda i, d: (0, 0))],
            out_specs=pl.BlockSpec((TM, D_MODEL), row),
            scratch_shapes=[pltpu.VMEM((2, TM, D_MODEL), F32), pltpu.SemaphoreType.DMA(())]),
        out_shape=jax.ShapeDtypeStruct((n_tok, D_MODEL), F32),
        compiler_params=_cparams(("arbitrary",)),
        name="moe_combine",
    )(dest, ys, x, winfo, mod, norm)


def _moe_block(x, h, info, counts, mod, norm, w_gate, w_up, w_down, tile_fn, n_tiles, final):
    dest, *plan = _dispatch_plan(info, counts, n_tiles * TM)
    ys = _moe(plan, _dispatch(dest, h), w_gate, w_up, w_down)
    return _combine(dest, ys, x, info.T, mod, norm, tile_fn, n_tiles, final)


def _rope_tables(dim):
    half, quarter = dim // 2, dim // 4
    t = jnp.arange(SEQ)
    pos = jnp.stack([t // GRID_W, t % GRID_W], axis=-1).astype(F32)
    inv_freq = ROPE_THETA ** (-jnp.arange(0, half, 2, dtype=F32) / half)
    ang = pos[:, :, None] * inv_freq
    cos = jnp.cos(ang)
    sin = jnp.sin(ang)
    cos = jnp.concatenate([cos[:, 0], cos[:, 0], cos[:, 1], cos[:, 1]], axis=-1)
    sin = jnp.concatenate([-sin[:, 0], sin[:, 0], -sin[:, 1], sin[:, 1]], axis=-1)
    pad = LANES - dim
    cos = jnp.pad(cos, ((0, 0), (0, pad)))
    sin = jnp.pad(sin, ((0, 0), (0, pad)))
    ctx_cos = jnp.pad(jnp.ones((CTX_LEN, dim), F32), ((0, 0), (0, pad)))
    return jnp.concatenate([ctx_cos, cos], axis=0), jnp.concatenate([jnp.zeros((CTX_LEN, LANES), F32), sin], axis=0)


def kernel(x, c, ctx, c_ctx, router_w, router_b, final_norm, l0_mod_w, l0_mod_b, l0_norm_mix, l0_norm_ffn, l0_w_in, l0_q_norm, l0_w_uq, l0_kv_norm, l0_w_ukv, l0_conv_w, l0_conv_b, l0_gate_a_w, l0_gate_a_b, l0_gate_x_w, l0_gate_x_b, l0_lru_lambda, l0_w_out, l0_exp_gate, l0_exp_up, l0_exp_down, l1_mod_w, l1_mod_b, l1_norm_mix, l1_norm_ffn, l1_w_in, l1_sink, l1_rpb, l1_w_out, l1_exp_gate, l1_exp_up, l1_exp_down):
    row = lambda v: v.reshape(1, -1)
    xs = jnp.concatenate([ctx, x], axis=1).reshape(TOK, D_MODEL)

    cs = jnp.concatenate([c, c_ctx[None], jnp.zeros((SUBLANES - BATCH - 1, D_MODEL), F32)], axis=0)
    mod0 = _modulation(cs, l0_mod_w, l0_mod_b)
    mod1 = _modulation(cs, l1_mod_w, l1_mod_b)

    perm = jnp.arange(N_EXPERTS).reshape(N_GROUPS, EXPERTS_PER_GROUP).T.reshape(-1)
    rwt = router_w.T[perm]
    rwh = rwt.astype(BF16)
    rwl = (rwt - rwh.astype(F32)).astype(BF16)
    rb = router_b[perm].reshape(N_EXPERTS, 1).astype(F32)
    tri = jnp.triu(jnp.ones((OUT_SUB, OUT_SUB), F32), k=1).astype(BF16)
    router = (rwh, rwl, rb, tri)

    split = MLA_Q_RANK + MLA_KV_RANK + MLA_ROPE
    w_in0 = jnp.concatenate([l0_w_in[:, :split], jnp.zeros((D_MODEL, LANES - MLA_ROPE), F32), l0_w_in[:, split:]],
                            axis=1).astype(BF16)
    cqkv, xr, gr = _in0(xs, row(l0_norm_mix), mod0, w_in0)
    wq = jnp.pad(l0_w_uq.reshape(MLA_Q_RANK, MLA_HEADS, MLA_NOPE + MLA_ROPE),
                 ((0, 0), (0, 0), (0, MLA_QK - MLA_NOPE - MLA_ROPE))).reshape(MLA_Q_RANK, MLA_HEADS * MLA_QK)
    wkv = l0_w_ukv.reshape(MLA_KV_RANK, MLA_HEADS, MLA_NOPE + MLA_V)
    wk = wkv[:, :, :MLA_NOPE].reshape(MLA_KV_RANK, MLA_HEADS * MLA_NOPE)
    wv = wkv[:, :, MLA_NOPE:].reshape(MLA_KV_RANK, MLA_HEADS * MLA_V)
    cos0, sin0 = _rope_tables(MLA_ROPE)
    q, k, v = _mla_proj(cqkv, row(l0_q_norm), row(l0_kv_norm), wq.astype(BF16), wk.astype(BF16), wv.astype(BF16),
                        cos0, sin0)
    att = _mla_attn(q, k, v)
    rnn = _rglru(xr, gr, l0_conv_w, l0_conv_b, l0_gate_a_w, l0_gate_a_b, l0_gate_x_w, l0_gate_x_b, l0_lru_lambda)
    n0 = TOK // TM
    xs, h, info, counts = _out_proj(att, rnn, xs, l0_w_out.astype(BF16), mod0, row(l0_norm_ffn), *router,
                                    _comb_tile, n0)
    xs = _moe_block(xs, h, info, counts, mod0, row(final_norm), l0_exp_gate, l0_exp_up, l0_exp_down,
                    _comb_tile, n0, False)

    cos1, sin1 = _rope_tables(HEAD_DIM)
    qw, kw, vw, qn, kn, vn = _in1(xs, row(l1_norm_mix), mod1, l1_w_in.astype(BF16), cos1, sin1)
    win = _win_attn(l1_sink.astype(F32), qw, kw, vw)
    na = _na_attn(qn, kn, vn, _na_bias_table(l1_rpb))
    n1 = BATCH * LAT_TPB
    xl, h, info, counts = _out_proj(win, na, xs, l1_w_out.astype(BF16), mod1, row(l1_norm_ffn), *router,
                                    _lat_tile, n1)
    out = _moe_block(xl, h, info, counts, mod1, row(final_norm), l1_exp_gate, l1_exp_up, l1_exp_down,
                     lambda i: (i, i // LAT_TPB), n1, True)
    return out.reshape(BATCH, SEQ, D_MODEL)
```

```python
import functools

import jax
import jax.numpy as jnp
from jax import lax
from jax.experimental import pallas as pl
from jax.experimental.pallas import tpu as pltpu

F32 = jnp.float32
BF16 = jnp.bfloat16

D_MODEL = 2048
BATCH = 4
SEQ = 2048
GRID_W = 64
CTX_LEN = 256
EPS = 1e-6
NEG_INF = -1e30
ROPE_THETA = 10000.0
N_MOD = 6

MLA_HEADS = 8
MLA_Q_RANK = 512
MLA_KV_RANK = 256
MLA_NOPE = 128
MLA_ROPE = 64
MLA_V = 128

LRU_WIDTH = 1024
LRU_BLOCKS = 8
LRU_C = 8.0

HEAD_DIM = 128
WIN_HEADS = 8
WIN_KV_HEADS = 2
WINDOW = 128
NA_HEADS = 8
NA_ROWS = 8
NA_COLS = 16

N_EXPERTS = 32
N_GROUPS = 8
EXPERTS_PER_GROUP = 4
EXPERT_FF = 512

LANES = 128
SUBLANES = 8
VMEM_LIMIT = 56 * 1024 * 1024

NB = CTX_LEN + SEQ
TOK = BATCH * NB
TM = 256
TPB = NB // TM
LAT_TPB = SEQ // TM
TMX = 256
MLA_QK = 2 * LANES
LOG2E = 1.4426950408889634


def _cparams(sem):
    return pltpu.CompilerParams(dimension_semantics=sem, vmem_limit_bytes=VMEM_LIMIT)


def _resident(shape):
    nd = len(shape)
    return pl.BlockSpec(shape, lambda *_: (0,) * nd, pipeline_mode=pl.Buffered(1))


def _rms(x, g):
    return x * lax.rsqrt(jnp.mean(x * x, axis=-1, keepdims=True) + EPS) * g


def _sigmoid(x):
    return 0.5 * jnp.tanh(0.5 * x) + 0.5


def _dot(a, b):
    return jnp.dot(a, b, preferred_element_type=F32)


def _dot_nt(a, b):
    return lax.dot_general(a, b, (((1,), (1,)), ((), ())), preferred_element_type=F32)


def _swap_blocks(x, blk):
    lane = lax.broadcasted_iota(jnp.int32, x.shape, 1)
    nxt = pltpu.roll(x, LANES - blk, axis=1)
    prv = pltpu.roll(x, blk, axis=1)
    return jnp.where((lane % (2 * blk)) < blk, nxt, prv)


def _rope(x, cos, sin, blk):
    return x * cos + _swap_blocks(x, blk) * sin


def _comb_tile(i):
    b = i // TPB
    return i, jnp.where(i % TPB == 0, BATCH, b)


def _lat_tile(i):
    b = i // LAT_TPB
    return b * TPB + 1 + i % LAT_TPB, b


def _mod_spec(tile_fn, k):
    return pl.BlockSpec((1, 1, D_MODEL), lambda i, *_: (tile_fn(i)[1] * N_MOD + k, 0, 0))


def _mod_kernel(c_ref, w_ref, b_ref, o_ref):
    c = c_ref[...]
    a = (c * jax.nn.sigmoid(c)).astype(BF16)
    o_ref[...] = _dot(a, w_ref[...].astype(BF16)) + b_ref[...]


def _modulation(cs, w, b):
    n = N_MOD * D_MODEL
    tn = 1024
    out = pl.pallas_call(
        _mod_kernel,
        grid=(n // tn,),
        in_specs=[pl.BlockSpec((SUBLANES, D_MODEL), lambda j: (0, 0)),
                  pl.BlockSpec((D_MODEL, tn), lambda j: (0, j)),
                  pl.BlockSpec((1, tn), lambda j: (0, j))],
        out_specs=pl.BlockSpec((SUBLANES, tn), lambda j: (0, j)),
        out_shape=jax.ShapeDtypeStruct((SUBLANES, n), F32),
        compiler_params=_cparams(("arbitrary",)),
        name="modulation",
    )(cs, w, b.reshape(1, n))
    return out.reshape(SUBLANES * N_MOD, 1, D_MODEL)


L0_CQKV = MLA_Q_RANK + MLA_KV_RANK + LANES
L0_IN_PAD = L0_CQKV + 2 * LRU_WIDTH


def _in0_kernel(x_ref, g_ref, sh_ref, sc_ref, w_ref, cqkv_ref, xr_ref, gr_ref):
    h = _rms(x_ref[...], g_ref[...]) * (1.0 + sc_ref[0]) + sh_ref[0]
    y = _dot(h.astype(BF16), w_ref[...])
    cqkv_ref[...] = y[:, :L0_CQKV]
    xr_ref[...] = y[:, L0_CQKV:L0_CQKV + LRU_WIDTH]
    gr_ref[...] = y[:, L0_CQKV + LRU_WIDTH:]


def _in0(x, g, mod, w):
    row = lambda i: (i, 0)
    return pl.pallas_call(
        _in0_kernel,
        grid=(TOK // TM,),
        in_specs=[pl.BlockSpec((TM, D_MODEL), row), _resident((1, D_MODEL)),
                  _mod_spec(_comb_tile, 0), _mod_spec(_comb_tile, 1), _resident((D_MODEL, L0_IN_PAD))],
        out_specs=[pl.BlockSpec((TM, L0_CQKV), row), pl.BlockSpec((TM, LRU_WIDTH), row),
                   pl.BlockSpec((TM, LRU_WIDTH), row)],
        out_shape=[jax.ShapeDtypeStruct((TOK, L0_CQKV), F32), jax.ShapeDtypeStruct((TOK, LRU_WIDTH), F32),
                   jax.ShapeDtypeStruct((TOK, LRU_WIDTH), F32)],
        compiler_params=_cparams(("parallel",)),
        name="l0_in_proj",
    )(x, g, mod, mod, w)


def _mla_proj_kernel(c_ref, qn_ref, kvn_ref, wq_ref, wk_ref, wv_ref, cos_ref, sin_ref, q_ref, k_ref, v_ref):
    c = c_ref[...]
    cos = cos_ref[...]
    sin = sin_ref[...]
    nq = _rms(c[:, :MLA_Q_RANK], qn_ref[...]).astype(BF16)
    q = _dot(nq, wq_ref[...]) * (LOG2E * (MLA_NOPE + MLA_ROPE) ** -0.5)
    nkv = _rms(c[:, MLA_Q_RANK:MLA_Q_RANK + MLA_KV_RANK], kvn_ref[...]).astype(BF16)
    kn = _dot(nkv, wk_ref[...])
    v_ref[...] = _dot(nkv, wv_ref[...]).astype(BF16)
    kr = _rope(c[:, MLA_Q_RANK + MLA_KV_RANK:], cos, sin, MLA_ROPE // 4).astype(BF16)
    for h in range(MLA_HEADS):
        lo = h * MLA_QK
        q_ref[:, lo:lo + LANES] = q[:, lo:lo + LANES].astype(BF16)
        q_ref[:, lo + LANES:lo + MLA_QK] = _rope(q[:, lo + LANES:lo + MLA_QK], cos, sin, MLA_ROPE // 4).astype(BF16)
        k_ref[:, lo:lo + LANES] = kn[:, h * LANES:(h + 1) * LANES].astype(BF16)
        k_ref[:, lo + LANES:lo + MLA_QK] = kr


def _mla_proj(cqkv, qn, kvn, wq, wk, wv, cos, sin):
    row = lambda i: (i, 0)
    pos = lambda i: (i % TPB, 0)
    hq = MLA_HEADS * MLA_QK
    hv = MLA_HEADS * MLA_V
    return pl.pallas_call(
        _mla_proj_kernel,
        grid=(TOK // TM,),
        in_specs=[pl.BlockSpec((TM, L0_CQKV), row), _resident((1, MLA_Q_RANK)), _resident((1, MLA_KV_RANK)),
                  _resident((MLA_Q_RANK, hq)), _resident((MLA_KV_RANK, hv)), _resident((MLA_KV_RANK, hv)),
                  pl.BlockSpec((TM, LANES), pos), pl.BlockSpec((TM, LANES), pos)],
        out_specs=[pl.BlockSpec((TM, hq), row), pl.BlockSpec((TM, hq), row), pl.BlockSpec((TM, hv), row)],
        out_shape=[jax.ShapeDtypeStruct((TOK, hq), BF16), jax.ShapeDtypeStruct((TOK, hq), BF16),
                   jax.ShapeDtypeStruct((TOK, hv), BF16)],
        compiler_params=_cparams(("parallel",)),
        name="mla_proj",
    )(cqkv, qn, kvn, wq, wk, wv, cos, sin)


MLA_TQ = 256


def _softmax_pv(s, v):
    m = jnp.max(s, axis=-1, keepdims=True)
    p = jnp.exp2(s - m)
    l = jnp.sum(p, axis=-1, keepdims=True)
    return _dot(p.astype(BF16), v) / l


def _mla_attn_kernel(q_ref, k_ref, v_ref, o_ref):
    s = _dot_nt(q_ref[0:CTX_LEN, :], k_ref[0:CTX_LEN, :])
    o_ref[0:CTX_LEN, :] = _softmax_pv(s, v_ref[0:CTX_LEN, :]).astype(o_ref.dtype)
    for t in range(SEQ // MLA_TQ):
        r0 = CTX_LEN + t * MLA_TQ
        s = _dot_nt(q_ref[r0:r0 + MLA_TQ, :], k_ref[...])
        o_ref[r0:r0 + MLA_TQ, :] = _softmax_pv(s, v_ref[...]).astype(o_ref.dtype)


def _mla_attn(q, k, v):
    blk = lambda b, h: (b, h)
    return pl.pallas_call(
        _mla_attn_kernel,
        grid=(BATCH, MLA_HEADS),
        in_specs=[pl.BlockSpec((NB, MLA_QK), blk), pl.BlockSpec((NB, MLA_QK), blk), pl.BlockSpec((NB, MLA_V), blk)],
        out_specs=pl.BlockSpec((NB, MLA_V), blk),
        out_shape=jax.ShapeDtypeStruct((TOK, MLA_HEADS * MLA_V), BF16),
        compiler_params=_cparams(("parallel", "parallel")),
        name="mla_attn",
    )(q, k, v)


LRU_BW = LRU_WIDTH // LRU_BLOCKS
CTX_GROUPS = CTX_LEN // SUBLANES
LAT_GROUPS = SEQ // SUBLANES


def _scan_group(a, b, reverse):
    row = lax.broadcasted_iota(jnp.int32, a.shape, 0)
    for d in (1, 2, 4):
        shift = SUBLANES - d if reverse else d
        a_s = pltpu.roll(a, shift, axis=0)
        b_s = pltpu.roll(b, shift, axis=0)
        m = (row < SUBLANES - d) if reverse else (row >= d)
        b = jnp.where(m, a * b_s + b, b)
        a = jnp.where(m, a * a_s, a)
    return a, b


def _rglru_kernel(xr_ref, gr_ref, cw_ref, cb_ref, wa_ref, ba_ref, wx_ref, bx_ref, lam_ref, y_ref,
                  af_ref, bf_ref, ab_ref, bb_ref, hf_ref, hb_ref):
    row8 = lax.broadcasted_iota(jnp.int32, (SUBLANES, LRU_BW), 0)

    def taps(seg):
        n = seg.shape[0]

        def shifted(shift, keep, first):
            r = pltpu.roll(seg, shift % n, axis=0)
            if first:
                return jnp.concatenate([jnp.where(keep, r[:SUBLANES], 0.0), r[SUBLANES:]], axis=0)
            return jnp.concatenate([r[:-SUBLANES], jnp.where(keep, r[-SUBLANES:], 0.0)], axis=0)

        return shifted(2, row8 >= 2, True), shifted(1, row8 >= 1, True), shifted(-1, row8 < SUBLANES - 1, False)

    x = xr_ref[...]
    tc = taps(x[:CTX_LEN])
    tl = taps(x[CTX_LEN:])
    xm2, xm1, xp1 = [jnp.concatenate([a, b], axis=0) for a, b in zip(tc, tl)]
    u = cb_ref[...] + xm2 * cw_ref[0:1, :] + xm1 * cw_ref[1:2, :] + x * cw_ref[2:3, :] + xp1 * cw_ref[3:4, :]
    ub = u.astype(BF16)
    for d, (a_ref, b_ref) in enumerate(((af_ref, bf_ref), (ab_ref, bb_ref))):
        r = _sigmoid(_dot(ub, wa_ref[d, 0].astype(BF16)) + ba_ref[d:d + 1, :])
        ig = _sigmoid(_dot(ub, wx_ref[d, 0].astype(BF16)) + bx_ref[d:d + 1, :])
        z = -lam_ref[d:d + 1, :]
        softplus = jnp.maximum(z, 0.0) + jnp.log(1.0 + jnp.exp(-jnp.abs(z)))
        log_a = -LRU_C * r * softplus
        a = jnp.exp(log_a)
        a_ref[...] = a
        t = 1.0 - a * a
        b_ref[...] = jnp.where(t > 0.0, t * lax.rsqrt(t), 0.0) * (ig * u)

    def step(gf, gb, hf, hb):
        rf = pl.multiple_of(gf * SUBLANES, SUBLANES)
        a, b = _scan_group(af_ref[pl.ds(rf, SUBLANES), :], bf_ref[pl.ds(rf, SUBLANES), :], False)
        h = a * hf + b
        hf_ref[pl.ds(rf, SUBLANES), :] = h
        hf = jnp.broadcast_to(h[SUBLANES - 1:SUBLANES, :], h.shape)
        rb = pl.multiple_of(gb * SUBLANES, SUBLANES)
        a, b = _scan_group(ab_ref[pl.ds(rb, SUBLANES), :], bb_ref[pl.ds(rb, SUBLANES), :], True)
        h = a * hb + b
        hb_ref[pl.ds(rb, SUBLANES), :] = h
        hb = jnp.broadcast_to(h[0:1, :], h.shape)
        return hf, hb

    zero = jnp.zeros((SUBLANES, LRU_BW), F32)
    carry = lax.fori_loop(0, CTX_GROUPS, lambda i, c: step(i, CTX_GROUPS - 1 - i, *c), (zero, zero), unroll=2)
    lax.fori_loop(0, LAT_GROUPS, lambda i, c: step(CTX_GROUPS + i, CTX_GROUPS + LAT_GROUPS - 1 - i, *c), carry, unroll=2)
    y_ref[...] = ((hf_ref[...] + hb_ref[...]) * jax.nn.gelu(gr_ref[...])).astype(y_ref.dtype)


def _rglru(xr, gr, conv_w, conv_b, wa, ba, wx, bx, lam):
    blk = lambda b, n: (b, n)
    col = lambda b, n: (0, n)
    gate = lambda b, n: (0, n, 0, 0)
    seg = pltpu.VMEM((NB, LRU_BW), F32)
    return pl.pallas_call(
        _rglru_kernel,
        grid=(BATCH, LRU_BLOCKS),
        in_specs=[pl.BlockSpec((NB, LRU_BW), blk), pl.BlockSpec((NB, LRU_BW), blk),
                  pl.BlockSpec((4, LRU_BW), col), pl.BlockSpec((1, LRU_BW), col),
                  pl.BlockSpec((2, 1, LRU_BW, LRU_BW), gate), pl.BlockSpec((2, LRU_BW), col),
                  pl.BlockSpec((2, 1, LRU_BW, LRU_BW), gate), pl.BlockSpec((2, LRU_BW), col),
                  pl.BlockSpec((2, LRU_BW), col)],
        out_specs=pl.BlockSpec((NB, LRU_BW), blk),
        out_shape=jax.ShapeDtypeStruct((TOK, LRU_WIDTH), BF16),
        scratch_shapes=[seg, seg, seg, seg, seg, seg],
        compiler_params=_cparams(("parallel", "parallel")),
        name="rglru",
    )(xr, gr, conv_w, conv_b.reshape(1, LRU_WIDTH), wa, ba, wx, bx, lam)


def _gather_rows(ys_hbm, y_buf, sem, dest_ref, base, n_tok):
    def row_copy(k, r, d):
        return pltpu.make_async_copy(ys_hbm.at[pl.ds(d, 1)], y_buf.at[k, pl.ds(r, 1)], sem)

    def start():
        def body(r, c):
            for k in range(2):
                row_copy(k, r, dest_ref[k * n_tok + base + r]).start(priority=k)
            return c

        lax.fori_loop(0, TM, body, 0, unroll=8)

    def wait():
        def body(r, c):
            for k in range(2):
                row_copy(k, r, 0).wait()
            return c

        lax.fori_loop(0, TM, body, 0, unroll=8)

    return start, wait


L1_Q = WIN_HEADS * HEAD_DIM
L1_KV = WIN_KV_HEADS * HEAD_DIM
L1_NA = NA_HEADS * HEAD_DIM
L1_IN = L1_Q + 2 * L1_KV + 3 * L1_NA


def _in1_kernel(dest_ref, ys_hbm, x_ref, gate_ref, g2_ref, g_ref, sh_ref, sc_ref, w_ref, cos_ref, sin_ref,
                xo_ref, qw_ref, kw_ref, vw_ref, qn_ref, kn_ref, vn_ref, y_buf, sem):
    i = pl.program_id(0)
    slot = i % 2
    rows = lambda tile, sl: _gather_rows(ys_hbm, y_buf.at[sl], sem.at[sl], dest_ref, tile * TM, TOK)

    @pl.when(i == 0)
    def _():
        rows(0, 0)[0]()

    rows(i, slot)[1]()

    @pl.when(i + 1 < pl.num_programs(0))
    def _():
        rows(i + 1, 1 - slot)[0]()

    gate = gate_ref[...]
    x = x_ref[...] + g2_ref[0] * (gate[:, 4:5] * y_buf[slot, 0] + gate[:, 5:6] * y_buf[slot, 1])
    xo_ref[...] = x
    h = _rms(x, g_ref[...]) * (1.0 + sc_ref[0]) + sh_ref[0]
    y = _dot(h.astype(BF16), w_ref[...])
    cos = cos_ref[...]
    sin = sin_ref[...]
    scale = LOG2E * HEAD_DIM ** -0.5
    for hd in range(WIN_HEADS):
        lo = hd * HEAD_DIM
        qw_ref[:, lo:lo + HEAD_DIM] = _rope(y[:, lo:lo + HEAD_DIM] * scale, cos, sin, HEAD_DIM // 4).astype(BF16)
    for hd in range(WIN_KV_HEADS):
        lo = hd * HEAD_DIM
        kw_ref[:, lo:lo + HEAD_DIM] = _rope(y[:, L1_Q + lo:L1_Q + lo + HEAD_DIM], cos, sin, HEAD_DIM // 4).astype(BF16)
    o = L1_Q + L1_KV
    vw_ref[...] = y[:, o:o + L1_KV].astype(BF16)
    o += L1_KV
    qn_ref[...] = (y[:, o:o + L1_NA] * scale).astype(BF16)
    kn_ref[...] = y[:, o + L1_NA:o + 2 * L1_NA].astype(BF16)
    vn_ref[...] = y[:, o + 2 * L1_NA:].astype(BF16)


def _in1(dest, ys, x, gates, mod_prev, g, mod, w, cos, sin):
    row = lambda i, d: (i, 0)
    pos = lambda i, d: (i % TPB, 0)
    widths = (L1_Q, L1_KV, L1_KV, L1_NA, L1_NA, L1_NA)
    return pl.pallas_call(
        _in1_kernel,
        grid_spec=pltpu.PrefetchScalarGridSpec(
            num_scalar_prefetch=1,
            grid=(TOK // TM,),
            in_specs=[pl.BlockSpec(memory_space=pl.ANY), pl.BlockSpec((TM, D_MODEL), row),
                      pl.BlockSpec((TM, SUBLANES), row), _mod_spec(_comb_tile, 5), _resident((1, D_MODEL)),
                      _mod_spec(_comb_tile, 0), _mod_spec(_comb_tile, 1), _resident((D_MODEL, L1_IN)),
                      pl.BlockSpec((TM, LANES), pos), pl.BlockSpec((TM, LANES), pos)],
            out_specs=[pl.BlockSpec((TM, D_MODEL), row)] + [pl.BlockSpec((TM, n), row) for n in widths],
            scratch_shapes=[pltpu.VMEM((2, 2, TM, D_MODEL), F32), pltpu.SemaphoreType.DMA((2,))]),
        out_shape=[jax.ShapeDtypeStruct((TOK, D_MODEL), F32)] + [jax.ShapeDtypeStruct((TOK, n), BF16) for n in widths],
        compiler_params=_cparams(("arbitrary",)),
        name="l1_in_proj",
    )(dest, ys, x, gates, mod_prev, g, mod, mod, w, cos, sin)


WIN_TQ = 128
WIN_SPAN = WIN_TQ + 2 * WINDOW
WIN_G = WIN_HEADS // WIN_KV_HEADS


def _win_key_start(n):
    return min(max((n - 1) * WIN_TQ, 0), SEQ - WIN_SPAN)


WIN_OFFSETS = sorted({n * WIN_TQ - _win_key_start(n) for n in range(SEQ // WIN_TQ)})


def _win_mask_bias():
    qoff = jnp.arange(WIN_G * WIN_TQ)[:, None] % WIN_TQ
    koff = jnp.arange(WIN_SPAN)[None, :]
    return jnp.stack([jnp.where(jnp.abs(qoff - koff + d) <= WINDOW, 0.0, NEG_INF) for d in WIN_OFFSETS]).astype(F32)


def _win_kernel(sink_ref, q_ref, k_ref, v_ref, mask_ref, o_ref):
    hk = pl.program_id(1)
    o_ref[0:CTX_LEN, :] = jnp.zeros((CTX_LEN, WIN_G * HEAD_DIM), o_ref.dtype)
    rows = WIN_G * WIN_TQ
    head = lax.broadcasted_iota(jnp.int32, (rows, 1), 0) // WIN_TQ
    sink = jnp.zeros((rows, 1), F32)
    for g in range(WIN_G):
        sink = jnp.where(head == g, sink_ref[hk * WIN_G + g] * LOG2E, sink)

    for n in range(SEQ // WIN_TQ):
        r0 = CTX_LEN + n * WIN_TQ
        start = _win_key_start(n)
        ks = CTX_LEN + start
        q4 = q_ref[r0:r0 + WIN_TQ, :]
        q = jnp.concatenate([q4[:, g * HEAD_DIM:(g + 1) * HEAD_DIM] for g in range(WIN_G)], axis=0)
        s_c = _dot_nt(q, k_ref[0:CTX_LEN, :])
        s_w = _dot_nt(q, k_ref[ks:ks + WIN_SPAN, :]) + mask_ref[WIN_OFFSETS.index(n * WIN_TQ - start)]
        m = jnp.maximum(jnp.maximum(jnp.max(s_c, axis=-1, keepdims=True), jnp.max(s_w, axis=-1, keepdims=True)), sink)
        p_c = jnp.exp2(s_c - m)
        p_w = jnp.exp2(s_w - m)
        l = jnp.sum(p_c, axis=-1, keepdims=True) + jnp.sum(p_w, axis=-1, keepdims=True) + jnp.exp2(sink - m)
        o = (_dot(p_w.astype(BF16), v_ref[ks:ks + WIN_SPAN, :]) + _dot(p_c.astype(BF16), v_ref[0:CTX_LEN, :])) / l
        for g in range(WIN_G):
            o_ref[r0:r0 + WIN_TQ, g * HEAD_DIM:(g + 1) * HEAD_DIM] = o[g * WIN_TQ:(g + 1) * WIN_TQ].astype(o_ref.dtype)


def _win_attn(sink, q, k, v):
    blk = lambda b, h, *_: (b, h)
    mask = _win_mask_bias()
    return pl.pallas_call(
        _win_kernel,
        grid_spec=pltpu.PrefetchScalarGridSpec(
            num_scalar_prefetch=1,
            grid=(BATCH, WIN_KV_HEADS),
            in_specs=[pl.BlockSpec((NB, WIN_G * HEAD_DIM), blk), pl.BlockSpec((NB, HEAD_DIM), blk),
                      pl.BlockSpec((NB, HEAD_DIM), blk), _resident(mask.shape)],
            out_specs=pl.BlockSpec((NB, WIN_G * HEAD_DIM), blk)),
        out_shape=jax.ShapeDtypeStruct((TOK, L1_Q), BF16),
        compiler_params=_cparams(("parallel", "parallel")),
        name="window_attn",
    )(sink, q, k, v, mask)


NA_GRID_ROWS = SEQ // GRID_W
NA_BAND = NA_ROWS * GRID_W
NA_RPI = 4


def _na_kernel(q_ref, k_ref, v_ref, bias_ref, o_ref):
    o_ref[0:CTX_LEN, :] = jnp.zeros((CTX_LEN, HEAD_DIM), o_ref.dtype)

    for i in range(NA_GRID_ROWS // NA_RPI):
        qs = CTX_LEN + i * NA_RPI * GRID_W
        q = q_ref[qs:qs + NA_RPI * GRID_W, :]
        s_c = _dot_nt(q, k_ref[0:CTX_LEN, :])
        starts = []
        s_w = []
        for j in range(NA_RPI):
            r = i * NA_RPI + j
            r0 = min(max(r - NA_ROWS // 2, 0), NA_GRID_ROWS - NA_ROWS)
            ks = CTX_LEN + r0 * GRID_W
            starts.append(ks)
            s_w.append(_dot_nt(q[j * GRID_W:(j + 1) * GRID_W], k_ref[ks:ks + NA_BAND, :])
                       + jnp.concatenate([bias_ref[0, r0 - r + NA_ROWS - 1 + 2 * p] for p in range(NA_ROWS // 2)],
                                         axis=1))
        s_w = jnp.concatenate(s_w, axis=0)
        m = jnp.maximum(jnp.max(s_c, axis=-1, keepdims=True), jnp.max(s_w, axis=-1, keepdims=True))
        p_c = jnp.exp2(s_c - m)
        p_w = jnp.exp2(s_w - m)
        l = jnp.sum(p_c, axis=-1, keepdims=True) + jnp.sum(p_w, axis=-1, keepdims=True)
        p_w = p_w.astype(BF16)
        o_w = jnp.concatenate([_dot(p_w[j * GRID_W:(j + 1) * GRID_W], v_ref[starts[j]:starts[j] + NA_BAND, :])
                               for j in range(NA_RPI)], axis=0)
        o = (o_w + _dot(p_c.astype(BF16), v_ref[0:CTX_LEN, :])) / l
        o_ref[qs:qs + NA_RPI * GRID_W, :] = o.astype(o_ref.dtype)


def _na_attn(q, k, v, bias):
    blk = lambda b, h: (b, h)
    return pl.pallas_call(
        _na_kernel,
        grid=(BATCH, NA_HEADS),
        in_specs=[pl.BlockSpec((NB, HEAD_DIM), blk), pl.BlockSpec((NB, HEAD_DIM), blk),
                  pl.BlockSpec((NB, HEAD_DIM), blk),
                  pl.BlockSpec((1, 2 * NA_ROWS - 2, GRID_W, 2 * GRID_W), lambda b, h: (h, 0, 0, 0))],
        out_specs=pl.BlockSpec((NB, HEAD_DIM), blk),
        out_shape=jax.ShapeDtypeStruct((TOK, L1_NA), BF16),
        compiler_params=_cparams(("parallel", "parallel")),
        name="na_attn",
    )(q, k, v, bias)


def _na_bias_table(rpb):
    cols = jnp.arange(GRID_W)
    c0 = jnp.clip(cols - NA_COLS // 2, 0, GRID_W - NA_COLS)
    kc = cols[None, :]
    valid = (kc >= c0[:, None]) & (kc < c0[:, None] + NA_COLS)
    lo = GRID_W - NA_COLS
    ext = jnp.pad(rpb.astype(F32) * LOG2E, ((0, 0), (0, 0), (lo, lo)))
    tbl = jnp.stack([ext[:, :, GRID_W - 1 - c:2 * GRID_W - 1 - c] for c in range(GRID_W)], axis=2)
    tbl = jnp.where(valid[None, None], tbl, NEG_INF)
    return jnp.concatenate([tbl[:, :-1], tbl[:, 1:]], axis=-1)


def _router_logits(h, rwh_ref, rwl_ref):
    hh = h.astype(BF16)
    hl = (h - hh.astype(F32)).astype(BF16)
    return _dot_nt(rwh_ref[...], hh) + (_dot_nt(rwh_ref[...], hl) + _dot_nt(rwl_ref[...], hh))


def _router(logits, rb_ref, tri_ref, carry):
    tm = logits.shape[1]
    scores = jax.nn.sigmoid(logits)
    biased = scores + rb_ref[...]
    nj = EXPERTS_PER_GROUP
    s = [biased[j * N_GROUPS:(j + 1) * N_GROUPS] for j in range(nj)]
    u = [scores[j * N_GROUPS:(j + 1) * N_GROUPS] for j in range(nj)]
    gs = None
    for a in range(nj):
        for b in range(a + 1, nj):
            pair = s[a] + s[b]
            gs = pair if gs is None else jnp.maximum(gs, pair)
    giota = lax.broadcasted_iota(jnp.int32, (N_GROUPS, tm), 0).astype(F32)
    gmax = jnp.max(gs, axis=0, keepdims=True)
    gidx = jnp.min(jnp.where(gs == gmax, giota, float(N_GROUPS)), axis=0, keepdims=True)
    gm = giota == gidx
    v = [jnp.sum(jnp.where(gm, s[j], 0.0), axis=0, keepdims=True) for j in range(nj)]
    w = [jnp.sum(jnp.where(gm, u[j], 0.0), axis=0, keepdims=True) for j in range(nj)]
    sel = []
    for j in range(nj):
        beaten = jnp.zeros((1, tm), F32)
        for i in range(nj):
            if i != j:
                ahead = (v[i] >= v[j]) if i < j else (v[i] > v[j])
                beaten = beaten + jnp.where(ahead, 1.0, 0.0)
        sel.append(beaten < 2.0)
    wsum = sum(jnp.where(sel[j], w[j], 0.0) for j in range(nj))
    first = functools.reduce(jnp.minimum, [jnp.where(sel[j], float(j), float(nj)) for j in range(nj)])
    last = functools.reduce(jnp.maximum, [jnp.where(sel[j], float(j), -1.0) for j in range(nj)])
    gmf = jnp.where(gm, 1.0, 0.0)
    cnt = jnp.concatenate([jnp.where(sel[j], gmf, 0.0) for j in range(nj)], axis=0)
    pos = _dot(cnt.astype(BF16), tri_ref[...]) + carry
    rank = [jnp.sum(cnt[j * N_GROUPS:(j + 1) * N_GROUPS] * pos[j * N_GROUPS:(j + 1) * N_GROUPS], axis=0, keepdims=True)
            for j in range(nj)]
    pick = lambda which, vals: sum(jnp.where(which == float(j), vals[j], 0.0) for j in range(nj))
    gate = [w[j] / wsum for j in range(nj)]
    zero = jnp.zeros((1, tm), F32)
    info = jnp.concatenate(
        [gidx * nj + first, gidx * nj + last, pick(first, rank), pick(last, rank), pick(first, gate), pick(last, gate),
         zero, zero], axis=0)
    return info, carry + jnp.sum(cnt, axis=1, keepdims=True)


OUT_SUB = 128


def _out_kernel(ya_ref, yb_ref, x_ref, w_ref, g1_ref, n_ref, sh_ref, sc_ref,
                rwh_ref, rwl_ref, rb_ref, tri_ref, xo_ref, h_ref, info_ref, counts_ref, carry_ref, hbuf_ref):
    i = pl.program_id(0)

    @pl.when(i == 0)
    def _():
        carry_ref[...] = jnp.zeros_like(carry_ref)
        hbuf_ref[1] = jnp.zeros((TM, D_MODEL), F32)

    subs = range(0, TM, OUT_SUB)
    logits = [_router_logits(hbuf_ref[(i + 1) % 2, r0:r0 + OUT_SUB, :], rwh_ref, rwl_ref) for r0 in subs]
    half = w_ref.shape[0] // 2
    y = _dot(ya_ref[...], w_ref[0:half, :]) + _dot(yb_ref[...], w_ref[half:, :])
    old = carry_ref[:, 0:1]
    carry = old
    for r0, lg in zip(subs, logits):
        info_ref[:, r0:r0 + OUT_SUB], carry = _router(lg, rb_ref, tri_ref, carry)
    counts = jnp.broadcast_to(jnp.where(i > 0, carry, old), carry_ref.shape)
    carry_ref[...] = counts
    counts_ref[...] = counts

    x = x_ref[...] + g1_ref[0] * y
    xo_ref[...] = x
    h = _rms(x, n_ref[...]) * (1.0 + sc_ref[0]) + sh_ref[0]
    h_ref[...] = h
    hbuf_ref[i % 2] = h


def _out_proj(ya, yb, x, w_out, mod, norm, rwh, rwl, rb, tri, tile_fn, n_tiles):
    half = w_out.shape[0] // 2
    this = lambda i: jnp.minimum(i, n_tiles - 1)
    tile = lambda i: tile_fn(this(i))
    src = lambda i: (tile(i)[0], 0)
    dst = lambda i: (this(i), 0)
    n_tok = n_tiles * TM
    return pl.pallas_call(
        _out_kernel,
        grid=(n_tiles + 1,),
        in_specs=[pl.BlockSpec((TM, half), src), pl.BlockSpec((TM, half), src), pl.BlockSpec((TM, D_MODEL), src),
                  _resident(w_out.shape),
                  _mod_spec(tile, 2), _resident((1, D_MODEL)), _mod_spec(tile, 3), _mod_spec(tile, 4),
                  _resident((N_EXPERTS, D_MODEL)), _resident((N_EXPERTS, D_MODEL)), _resident((N_EXPERTS, 1)),
                  _resident((OUT_SUB, OUT_SUB))],
        out_specs=[pl.BlockSpec((TM, D_MODEL), dst), pl.BlockSpec((TM, D_MODEL), dst),
                   pl.BlockSpec((SUBLANES, TM), lambda i: (0, jnp.maximum(i - 1, 0))),
                   pl.BlockSpec((N_EXPERTS, LANES), lambda i: (0, 0))],
        out_shape=[jax.ShapeDtypeStruct((n_tok, D_MODEL), F32), jax.ShapeDtypeStruct((n_tok, D_MODEL), F32),
                   jax.ShapeDtypeStruct((SUBLANES, n_tok), F32), jax.ShapeDtypeStruct((N_EXPERTS, LANES), F32)],
        scratch_shapes=[pltpu.VMEM((N_EXPERTS, LANES), F32), pltpu.VMEM((2, TM, D_MODEL), F32)],
        compiler_params=_cparams(("arbitrary",)),
        name="out_proj_router",
    )(ya, yb, x, w_out, mod, norm, mod, mod, rwh, rwl, rb, tri)


def _lookup(table, idx):
    onehot = idx[..., None] == jnp.arange(table.shape[0], dtype=jnp.int32)
    return jnp.sum(jnp.where(onehot, table, 0), axis=-1)


def _dispatch_plan(info, counts, n_tok):
    n_steps = 2 * n_tok // TMX + N_EXPERTS - 1
    experts = jnp.arange(N_EXPERTS, dtype=jnp.int32)
    cnt = counts[:, 0].astype(jnp.int32).reshape(EXPERTS_PER_GROUP, N_GROUPS).T.reshape(N_EXPERTS)
    end = jnp.cumsum(cnt)
    off = end - cnt
    first_tile = off // TMX
    visits = jnp.where(cnt > 0, (end - 1) // TMX - first_tile + 1, 0)
    visit_end = jnp.cumsum(visits)
    n_valid = visit_end[-1]
    dest = _lookup(off, info[0:2].astype(jnp.int32)) + info[2:4].astype(jnp.int32)
    step = jnp.minimum(jnp.arange(n_steps, dtype=jnp.int32), n_valid - 1)
    e = jnp.sum(visit_end[None, :] <= step[:, None], axis=1).astype(jnp.int32)
    k = step - _lookup(visit_end - visits, e)
    tile = _lookup(first_tile, e) + k
    lo = jnp.clip(_lookup(off, e) - tile * TMX, 0, TMX)
    hi = jnp.clip(_lookup(end, e) - tile * TMX, 0, TMX)
    later = (experts[None, :] > experts[:, None]) & (cnt[None, :] > 0)
    nxt = jnp.min(jnp.where(later, experts[None, :], N_EXPERTS), axis=1)
    nxt = jnp.where(nxt == N_EXPERTS, -1, nxt)
    slot = (jnp.cumsum((cnt > 0).astype(jnp.int32)) - 1) % 2
    i32 = lambda v: v.astype(jnp.int32)
    return (dest.reshape(-1), i32(tile), i32(e), i32(n_valid.reshape(1)), i32(lo), i32(hi), i32(k == 0),
            i32(_lookup(nxt, e)), i32(_lookup(slot, e)))


def _dispatch_kernel(dest_ref, h_ref, xs_hbm, sem, *, n_tok):
    i = pl.program_id(0)

    def row_copy(r, d):
        return pltpu.make_async_copy(h_ref.at[pl.ds(r, 1)], xs_hbm.at[pl.ds(d, 1)], sem)

    def start(r, c):
        for k in range(2):
            row_copy(r, dest_ref[k * n_tok + i * TM + r]).start(priority=k)
        return c

    lax.fori_loop(0, TM, start, 0, unroll=8)

    def wait(r, c):
        for k in range(2):
            row_copy(r, 0).wait()
        return c

    lax.fori_loop(0, TM, wait, 0, unroll=8)


def _dispatch(dest, h):
    n_tok = h.shape[0]
    return pl.pallas_call(
        functools.partial(_dispatch_kernel, n_tok=n_tok),
        grid_spec=pltpu.PrefetchScalarGridSpec(
            num_scalar_prefetch=1,
            grid=(n_tok // TM,),
            in_specs=[pl.BlockSpec((TM, D_MODEL), lambda i, d: (i, 0))],
            out_specs=pl.BlockSpec(memory_space=pl.ANY),
            scratch_shapes=[pltpu.SemaphoreType.DMA(())]),
        out_shape=jax.ShapeDtypeStruct((2 * n_tok, D_MODEL), F32),
        compiler_params=_cparams(("arbitrary",)),
        name="moe_dispatch",
    )(dest, h)


def _moe_kernel(tile_ref, te_ref, nv_ref, lo_ref, hi_ref, first_ref, nxt_ref, slot_ref,
                xs_ref, wg_hbm, wu_hbm, wd_hbm, ys_ref, wg_f, wu_f, wd_f, wg_s, wu_s, wd_s, sem):
    s = pl.program_id(0)

    def fetch(e, slot):
        return [pltpu.make_async_copy(src.at[e], dst.at[slot], sem.at[slot])
                for src, dst in ((wg_hbm, wg_f), (wu_hbm, wu_f), (wd_hbm, wd_f))]

    @pl.when(s < nv_ref[0])
    def _():
        @pl.when(first_ref[s] == 1)
        def _():
            slot = slot_ref[s]

            @pl.when(s == 0)
            def _():
                for c in fetch(te_ref[0], slot):
                    c.start()

            for c in fetch(te_ref[s], slot):
                c.wait()

            @pl.when(nxt_ref[s] >= 0)
            def _():
                for c in fetch(nxt_ref[s], 1 - slot):
                    c.start()

            wg_s[...] = wg_f[slot].astype(BF16)
            wu_s[...] = wu_f[slot].astype(BF16)
            wd_s[...] = wd_f[slot].astype(BF16)

        lo = lo_ref[s]
        hi = hi_ref[s]

        @pl.when(lo == 0)
        def _():
            ys_ref[...] = jnp.zeros_like(ys_ref)

        def visit(r0, n):
            x = xs_ref[r0:r0 + n, :].astype(BF16)
            hg = _dot(x, wg_s[...])
            he = (hg * jax.nn.sigmoid(hg)) * _dot(x, wu_s[...])
            y = _dot(he.astype(BF16), wd_s[...])
            row = r0 + lax.broadcasted_iota(jnp.int32, (n, 1), 0)
            ys_ref[r0:r0 + n, :] = jnp.where((row >= lo) & (row < hi), y, ys_ref[r0:r0 + n, :])

        half = TMX // 2
        lower = hi <= half
        upper = lo >= half
        pl.when(lower)(lambda: visit(0, half))
        pl.when(upper)(lambda: visit(half, half))
        pl.when(jnp.logical_not(jnp.logical_or(lower, upper)))(lambda: visit(0, TMX))


def _moe(plan, xs, w_gate, w_up, w_down):
    n_steps = plan[0].shape[0]
    tile = lambda s, t, *_: (t[s], 0)
    any_spec = pl.BlockSpec(memory_space=pl.ANY)
    return pl.pallas_call(
        _moe_kernel,
        grid_spec=pltpu.PrefetchScalarGridSpec(
            num_scalar_prefetch=len(plan),
            grid=(n_steps,),
            in_specs=[pl.BlockSpec((TMX, D_MODEL), tile), any_spec, any_spec, any_spec],
            out_specs=pl.BlockSpec((TMX, D_MODEL), tile),
            scratch_shapes=[pltpu.VMEM((2, D_MODEL, EXPERT_FF), F32), pltpu.VMEM((2, D_MODEL, EXPERT_FF), F32),
                            pltpu.VMEM((2, EXPERT_FF, D_MODEL), F32),
                            pltpu.VMEM((D_MODEL, EXPERT_FF), BF16), pltpu.VMEM((D_MODEL, EXPERT_FF), BF16),
                            pltpu.VMEM((EXPERT_FF, D_MODEL), BF16), pltpu.SemaphoreType.DMA((2,))]),
        out_shape=jax.ShapeDtypeStruct(xs.shape, F32),
        compiler_params=_cparams(("arbitrary",)),
        name="moe_experts",
    )(*plan, xs, w_gate, w_up, w_down)


def _combine_kernel(dest_ref, ys_hbm, x_ref, gate_ref, g2_ref, n_ref, o_ref, y_buf, sem, *, n_tok):
    start, wait = _gather_rows(ys_hbm, y_buf, sem, dest_ref, pl.program_id(0) * TM, n_tok)
    start()
    wait()
    gate = gate_ref[...]
    x = x_ref[...] + g2_ref[0] * (gate[:, 4:5] * y_buf[0] + gate[:, 5:6] * y_buf[1])
    o_ref[...] = _rms(x, n_ref[...])


def _combine(dest, ys, x, gates, mod, norm, tile_fn, n_tiles):
    n_tok = n_tiles * TM
    row = lambda i, d: (i, 0)
    return pl.pallas_call(
        functools.partial(_combine_kernel, n_tok=n_tok),
        grid_spec=pltpu.PrefetchScalarGridSpec(
            num_scalar_prefetch=1,
            grid=(n_tiles,),
            in_specs=[pl.BlockSpec(memory_space=pl.ANY), pl.BlockSpec((TM, D_MODEL), row),
                      pl.BlockSpec((TM, SUBLANES), row), _mod_spec(tile_fn, 5),
                      pl.BlockSpec((1, D_MODEL), lambda i, d: (0, 0))],
            out_specs=pl.BlockSpec((TM, D_MODEL), row),
            scratch_shapes=[pltpu.VMEM((2, TM, D_MODEL), F32), pltpu.SemaphoreType.DMA(())]),
        out_shape=jax.ShapeDtypeStruct((n_tok, D_MODEL), F32),
        compiler_params=_cparams(("arbitrary",)),
        name="moe_combine",
    )(dest, ys, x, gates, mod, norm)


def _experts(h, info, counts, w_gate, w_up, w_down):
    dest, *plan = _dispatch_plan(info, counts, h.shape[0])
    return dest, _moe(plan, _dispatch(dest, h), w_gate, w_up, w_down)


def _rope_tables(dim):
    half = dim // 2
    inv_freq = ROPE_THETA ** (-jnp.arange(0, half, 2, dtype=F32) / half)
    ang_r = jnp.arange(SEQ // GRID_W, dtype=F32)[:, None] * inv_freq
    ang_c = jnp.arange(GRID_W, dtype=F32)[:, None] * inv_freq
    by_row = lambda v: jnp.repeat(v, GRID_W, axis=0)
    by_col = lambda v: jnp.tile(v, (SEQ // GRID_W, 1))
    cos_r, sin_r, cos_c, sin_c = by_row(jnp.cos(ang_r)), by_row(jnp.sin(ang_r)), by_col(jnp.cos(ang_c)), by_col(jnp.sin(ang_c))
    cos = jnp.concatenate([cos_r, cos_r, cos_c, cos_c], axis=-1)
    sin = jnp.concatenate([-sin_r, sin_r, -sin_c, sin_c], axis=-1)
    pad = LANES - dim
    cos = jnp.pad(cos, ((0, 0), (0, pad)))
    sin = jnp.pad(sin, ((0, 0), (0, pad)))
    ctx_cos = jnp.pad(jnp.ones((CTX_LEN, dim), F32), ((0, 0), (0, pad)))
    return jnp.concatenate([ctx_cos, cos], axis=0), jnp.concatenate([jnp.zeros((CTX_LEN, LANES), F32), sin], axis=0)


def kernel(x, c, ctx, c_ctx, router_w, router_b, final_norm, l0_mod_w, l0_mod_b, l0_norm_mix, l0_norm_ffn, l0_w_in, l0_q_norm, l0_w_uq, l0_kv_norm, l0_w_ukv, l0_conv_w, l0_conv_b, l0_gate_a_w, l0_gate_a_b, l0_gate_x_w, l0_gate_x_b, l0_lru_lambda, l0_w_out, l0_exp_gate, l0_exp_up, l0_exp_down, l1_mod_w, l1_mod_b, l1_norm_mix, l1_norm_ffn, l1_w_in, l1_sink, l1_rpb, l1_w_out, l1_exp_gate, l1_exp_up, l1_exp_down):
    row = lambda v: v.reshape(1, -1)
    xs = jnp.concatenate([ctx, x], axis=1).reshape(TOK, D_MODEL)

    cs = jnp.concatenate([c, c_ctx[None], jnp.zeros((SUBLANES - BATCH - 1, D_MODEL), F32)], axis=0)
    mod0 = _modulation(cs, l0_mod_w, l0_mod_b)
    mod1 = _modulation(cs, l1_mod_w, l1_mod_b)

    perm = jnp.arange(N_EXPERTS).reshape(N_GROUPS, EXPERTS_PER_GROUP).T.reshape(-1)
    rwt = router_w.T[perm]
    rwh = rwt.astype(BF16)
    rwl = (rwt - rwh.astype(F32)).astype(BF16)
    rb = router_b[perm].reshape(N_EXPERTS, 1).astype(F32)
    tri = jnp.triu(jnp.ones((OUT_SUB, OUT_SUB), F32), k=1).astype(BF16)
    router = (rwh, rwl, rb, tri)

    split = MLA_Q_RANK + MLA_KV_RANK + MLA_ROPE
    w_in0 = jnp.concatenate([l0_w_in[:, :split], jnp.zeros((D_MODEL, LANES - MLA_ROPE), F32), l0_w_in[:, split:]],
                            axis=1).astype(BF16)
    cqkv, xr, gr = _in0(xs, row(l0_norm_mix), mod0, w_in0)
    wq = jnp.pad(l0_w_uq.reshape(MLA_Q_RANK, MLA_HEADS, MLA_NOPE + MLA_ROPE),
                 ((0, 0), (0, 0), (0, MLA_QK - MLA_NOPE - MLA_ROPE))).reshape(MLA_Q_RANK, MLA_HEADS * MLA_QK)
    wkv = l0_w_ukv.reshape(MLA_KV_RANK, MLA_HEADS, MLA_NOPE + MLA_V)
    wk = wkv[:, :, :MLA_NOPE].reshape(MLA_KV_RANK, MLA_HEADS * MLA_NOPE)
    wv = wkv[:, :, MLA_NOPE:].reshape(MLA_KV_RANK, MLA_HEADS * MLA_V)
    cos0, sin0 = _rope_tables(MLA_ROPE)
    q, k, v = _mla_proj(cqkv, row(l0_q_norm), row(l0_kv_norm), wq.astype(BF16), wk.astype(BF16), wv.astype(BF16),
                        cos0, sin0)
    att = _mla_attn(q, k, v)
    rnn = _rglru(xr, gr, l0_conv_w, l0_conv_b, l0_gate_a_w, l0_gate_a_b, l0_gate_x_w, l0_gate_x_b, l0_lru_lambda)
    n0 = TOK // TM
    xs, h, info, counts = _out_proj(att, rnn, xs, l0_w_out.astype(BF16), mod0, row(l0_norm_ffn), *router,
                                    _comb_tile, n0)
    dest, ys = _experts(h, info, counts, l0_exp_gate, l0_exp_up, l0_exp_down)

    cos1, sin1 = _rope_tables(HEAD_DIM)
    xs, qw, kw, vw, qn, kn, vn = _in1(dest, ys, xs, info.T, mod0, row(l1_norm_mix), mod1, l1_w_in.astype(BF16),
                                      cos1, sin1)
    win = _win_attn(l1_sink.astype(F32), qw, kw, vw)
    na = _na_attn(qn, kn, vn, _na_bias_table(l1_rpb))
    n1 = BATCH * LAT_TPB
    xl, h, info, counts = _out_proj(win, na, xs, l1_w_out.astype(BF16), mod1, row(l1_norm_ffn), *router,
                                    _lat_tile, n1)
    dest, ys = _experts(h, info, counts, l1_exp_gate, l1_exp_up, l1_exp_down)
    out = _combine(dest, ys, xl, info.T, mod1, row(final_norm), lambda i: (i, i // LAT_TPB), n1)
    return out.reshape(BATCH, SEQ, D_MODEL)
```

```python
import functools

import jax
import jax.numpy as jnp
from jax import lax
from jax.experimental import pallas as pl
from jax.experimental.pallas import tpu as pltpu

F32 = jnp.float32
BF16 = jnp.bfloat16

D_MODEL = 2048
BATCH = 4
SEQ = 2048
GRID_W = 64
CTX_LEN = 256
EPS = 1e-6
NEG_INF = -1e30
ROPE_THETA = 10000.0
N_MOD = 6

MLA_HEADS = 8
MLA_Q_RANK = 512
MLA_KV_RANK = 256
MLA_NOPE = 128
MLA_ROPE = 64
MLA_V = 128

LRU_WIDTH = 1024
LRU_BLOCKS = 8
LRU_C = 8.0

HEAD_DIM = 128
WIN_HEADS = 8
WIN_KV_HEADS = 2
WINDOW = 128
NA_HEADS = 8
NA_ROWS = 8
NA_COLS = 16

N_EXPERTS = 32
N_GROUPS = 8
EXPERTS_PER_GROUP = 4
EXPERT_FF = 512

LANES = 128
SUBLANES = 8
VMEM_LIMIT = 56 * 1024 * 1024

NB = CTX_LEN + SEQ
TOK = BATCH * NB
TM = 256
TPB = NB // TM
LAT_TPB = SEQ // TM
TMX = 256
MLA_QK = 2 * LANES
LOG2E = 1.4426950408889634


def _cparams(sem):
    return pltpu.CompilerParams(dimension_semantics=sem, vmem_limit_bytes=VMEM_LIMIT)


def _resident(shape):
    nd = len(shape)
    return pl.BlockSpec(shape, lambda *_: (0,) * nd, pipeline_mode=pl.Buffered(1))


def _rms(x, g):
    return x * lax.rsqrt(jnp.mean(x * x, axis=-1, keepdims=True) + EPS) * g


def _sigmoid(x):
    return 0.5 * jnp.tanh(0.5 * x) + 0.5


def _dot(a, b):
    return jnp.dot(a, b, preferred_element_type=F32)


def _dot_nt(a, b):
    return lax.dot_general(a, b, (((1,), (1,)), ((), ())), preferred_element_type=F32)


def _swap_blocks(x, blk):
    lane = lax.broadcasted_iota(jnp.int32, x.shape, 1)
    nxt = pltpu.roll(x, LANES - blk, axis=1)
    prv = pltpu.roll(x, blk, axis=1)
    return jnp.where((lane % (2 * blk)) < blk, nxt, prv)


def _rope(x, cos, sin, blk):
    return x * cos + _swap_blocks(x, blk) * sin


def _comb_tile(i):
    b = i // TPB
    return i, jnp.where(i % TPB == 0, BATCH, b)


def _lat_tile(i):
    b = i // LAT_TPB
    return b * TPB + 1 + i % LAT_TPB, b


def _ctx_spec(tile_fn):
    return pl.BlockSpec((TM, D_MODEL), lambda i, *_: (tile_fn(i) // TPB, 0))


def _lat_spec(tile_fn):
    return pl.BlockSpec((TM, D_MODEL),
                        lambda i, *_: (tile_fn(i) // TPB * LAT_TPB + jnp.maximum(tile_fn(i) % TPB - 1, 0), 0))


def _pick_tile(tile, ctx_ref, lat_ref):
    return jnp.where(tile % TPB == 0, ctx_ref[...], lat_ref[...])


def _mod_spec(tile_fn, k):
    return pl.BlockSpec((1, 1, D_MODEL), lambda i, *_: (tile_fn(i)[1] * N_MOD + k, 0, 0))


def _mod_kernel(c_ref, w_ref, b_ref, o_ref):
    c = c_ref[...]
    a = (c * jax.nn.sigmoid(c)).astype(BF16)
    o_ref[...] = _dot(a, w_ref[...].astype(BF16)) + b_ref[...]


def _modulation(cs, w, b):
    n = N_MOD * D_MODEL
    tn = 1024
    out = pl.pallas_call(
        _mod_kernel,
        grid=(n // tn,),
        in_specs=[pl.BlockSpec((SUBLANES, D_MODEL), lambda j: (0, 0)),
                  pl.BlockSpec((D_MODEL, tn), lambda j: (0, j)),
                  pl.BlockSpec((1, tn), lambda j: (0, j))],
        out_specs=pl.BlockSpec((SUBLANES, tn), lambda j: (0, j)),
        out_shape=jax.ShapeDtypeStruct((SUBLANES, n), F32),
        compiler_params=_cparams(("arbitrary",)),
        name="modulation",
    )(cs, w, b.reshape(1, n))
    return out.reshape(SUBLANES * N_MOD, 1, D_MODEL)


L0_CQKV = MLA_Q_RANK + MLA_KV_RANK + LANES
L0_IN_PAD = L0_CQKV + 2 * LRU_WIDTH


def _in0_kernel(xc_ref, xl_ref, g_ref, sh_ref, sc_ref, w_ref, cqkv_ref, xr_ref, gr_ref):
    x = _pick_tile(pl.program_id(0), xc_ref, xl_ref)
    h = _rms(x, g_ref[...]) * (1.0 + sc_ref[0]) + sh_ref[0]
    y = _dot(h.astype(BF16), w_ref[...])
    cqkv_ref[...] = y[:, :L0_CQKV]
    xr_ref[...] = y[:, L0_CQKV:L0_CQKV + LRU_WIDTH]
    gr_ref[...] = y[:, L0_CQKV + LRU_WIDTH:]


def _in0(x_ctx, x_lat, g, mod, w):
    row = lambda i: (i, 0)
    tile = lambda i: i
    return pl.pallas_call(
        _in0_kernel,
        grid=(TOK // TM,),
        in_specs=[_ctx_spec(tile), _lat_spec(tile), _resident((1, D_MODEL)),
                  _mod_spec(_comb_tile, 0), _mod_spec(_comb_tile, 1), _resident((D_MODEL, L0_IN_PAD))],
        out_specs=[pl.BlockSpec((TM, L0_CQKV), row), pl.BlockSpec((TM, LRU_WIDTH), row),
                   pl.BlockSpec((TM, LRU_WIDTH), row)],
        out_shape=[jax.ShapeDtypeStruct((TOK, L0_CQKV), F32), jax.ShapeDtypeStruct((TOK, LRU_WIDTH), F32),
                   jax.ShapeDtypeStruct((TOK, LRU_WIDTH), F32)],
        compiler_params=_cparams(("parallel",)),
        name="l0_in_proj",
    )(x_ctx, x_lat, g, mod, mod, w)


def _mla_proj_kernel(c_ref, qn_ref, kvn_ref, wq_ref, wk_ref, wv_ref, cos_ref, sin_ref, q_ref, k_ref, v_ref):
    c = c_ref[...]
    cos = cos_ref[...]
    sin = sin_ref[...]
    nq = _rms(c[:, :MLA_Q_RANK], qn_ref[...]).astype(BF16)
    q = _dot(nq, wq_ref[...]) * (LOG2E * (MLA_NOPE + MLA_ROPE) ** -0.5)
    nkv = _rms(c[:, MLA_Q_RANK:MLA_Q_RANK + MLA_KV_RANK], kvn_ref[...]).astype(BF16)
    kn = _dot(nkv, wk_ref[...])
    v_ref[...] = _dot(nkv, wv_ref[...]).astype(BF16)
    kr = _rope(c[:, MLA_Q_RANK + MLA_KV_RANK:], cos, sin, MLA_ROPE // 4).astype(BF16)
    for h in range(MLA_HEADS):
        lo = h * MLA_QK
        q_ref[:, lo:lo + LANES] = q[:, lo:lo + LANES].astype(BF16)
        q_ref[:, lo + LANES:lo + MLA_QK] = _rope(q[:, lo + LANES:lo + MLA_QK], cos, sin, MLA_ROPE // 4).astype(BF16)
        k_ref[:, lo:lo + LANES] = kn[:, h * LANES:(h + 1) * LANES].astype(BF16)
        k_ref[:, lo + LANES:lo + MLA_QK] = kr


def _mla_proj(cqkv, qn, kvn, wq, wk, wv, cos, sin):
    row = lambda i: (i, 0)
    pos = lambda i: (i % TPB, 0)
    hq = MLA_HEADS * MLA_QK
    hv = MLA_HEADS * MLA_V
    return pl.pallas_call(
        _mla_proj_kernel,
        grid=(TOK // TM,),
        in_specs=[pl.BlockSpec((TM, L0_CQKV), row), _resident((1, MLA_Q_RANK)), _resident((1, MLA_KV_RANK)),
                  _resident((MLA_Q_RANK, hq)), _resident((MLA_KV_RANK, hv)), _resident((MLA_KV_RANK, hv)),
                  pl.BlockSpec((TM, LANES), pos), pl.BlockSpec((TM, LANES), pos)],
        out_specs=[pl.BlockSpec((TM, hq), row), pl.BlockSpec((TM, hq), row), pl.BlockSpec((TM, hv), row)],
        out_shape=[jax.ShapeDtypeStruct((TOK, hq), BF16), jax.ShapeDtypeStruct((TOK, hq), BF16),
                   jax.ShapeDtypeStruct((TOK, hv), BF16)],
        compiler_params=_cparams(("parallel",)),
        name="mla_proj",
    )(cqkv, qn, kvn, wq, wk, wv, cos, sin)


MLA_TQ = 256


def _softmax_pv(s, v):
    m = jnp.max(s, axis=-1, keepdims=True)
    p = jnp.exp2(s - m)
    l = jnp.sum(p, axis=-1, keepdims=True)
    return _dot(p.astype(BF16), v) / l


def _mla_attn_kernel(q_ref, k_ref, v_ref, o_ref):
    s = _dot_nt(q_ref[0:CTX_LEN, :], k_ref[0:CTX_LEN, :])
    o_ref[0:CTX_LEN, :] = _softmax_pv(s, v_ref[0:CTX_LEN, :]).astype(o_ref.dtype)
    for t in range(SEQ // MLA_TQ):
        r0 = CTX_LEN + t * MLA_TQ
        s = _dot_nt(q_ref[r0:r0 + MLA_TQ, :], k_ref[...])
        o_ref[r0:r0 + MLA_TQ, :] = _softmax_pv(s, v_ref[...]).astype(o_ref.dtype)


def _mla_attn(q, k, v):
    blk = lambda b, h: (b, h)
    return pl.pallas_call(
        _mla_attn_kernel,
        grid=(BATCH, MLA_HEADS),
        in_specs=[pl.BlockSpec((NB, MLA_QK), blk), pl.BlockSpec((NB, MLA_QK), blk), pl.BlockSpec((NB, MLA_V), blk)],
        out_specs=pl.BlockSpec((NB, MLA_V), blk),
        out_shape=jax.ShapeDtypeStruct((TOK, MLA_HEADS * MLA_V), BF16),
        compiler_params=_cparams(("parallel", "parallel")),
        name="mla_attn",
    )(q, k, v)


LRU_BW = LRU_WIDTH // LRU_BLOCKS
CTX_GROUPS = CTX_LEN // SUBLANES
LAT_GROUPS = SEQ // SUBLANES


def _scan_group(a, b, reverse):
    row = lax.broadcasted_iota(jnp.int32, a.shape, 0)
    for d in (1, 2, 4):
        shift = SUBLANES - d if reverse else d
        a_s = pltpu.roll(a, shift, axis=0)
        b_s = pltpu.roll(b, shift, axis=0)
        m = (row < SUBLANES - d) if reverse else (row >= d)
        b = jnp.where(m, a * b_s + b, b)
        a = jnp.where(m, a * a_s, a)
    return a, b


def _rglru_kernel(xr_ref, gr_ref, cw_ref, cb_ref, wa_ref, ba_ref, wx_ref, bx_ref, lam_ref, y_ref,
                  af_ref, bf_ref, ab_ref, bb_ref, hf_ref, hb_ref):
    row8 = lax.broadcasted_iota(jnp.int32, (SUBLANES, LRU_BW), 0)

    def taps(seg):
        n = seg.shape[0]

        def shifted(shift, keep, first):
            r = pltpu.roll(seg, shift % n, axis=0)
            if first:
                return jnp.concatenate([jnp.where(keep, r[:SUBLANES], 0.0), r[SUBLANES:]], axis=0)
            return jnp.concatenate([r[:-SUBLANES], jnp.where(keep, r[-SUBLANES:], 0.0)], axis=0)

        return shifted(2, row8 >= 2, True), shifted(1, row8 >= 1, True), shifted(-1, row8 < SUBLANES - 1, False)

    x = xr_ref[...]
    tc = taps(x[:CTX_LEN])
    tl = taps(x[CTX_LEN:])
    xm2, xm1, xp1 = [jnp.concatenate([a, b], axis=0) for a, b in zip(tc, tl)]
    u = cb_ref[...] + xm2 * cw_ref[0:1, :] + xm1 * cw_ref[1:2, :] + x * cw_ref[2:3, :] + xp1 * cw_ref[3:4, :]
    ub = u.astype(BF16)
    for d, (a_ref, b_ref) in enumerate(((af_ref, bf_ref), (ab_ref, bb_ref))):
        r = _sigmoid(_dot(ub, wa_ref[d, 0].astype(BF16)) + ba_ref[d:d + 1, :])
        ig = _sigmoid(_dot(ub, wx_ref[d, 0].astype(BF16)) + bx_ref[d:d + 1, :])
        z = -lam_ref[d:d + 1, :]
        softplus = jnp.maximum(z, 0.0) + jnp.log(1.0 + jnp.exp(-jnp.abs(z)))
        log_a = -LRU_C * r * softplus
        a = jnp.exp(log_a)
        a_ref[...] = a
        t = 1.0 - a * a
        b_ref[...] = jnp.where(t > 0.0, t * lax.rsqrt(t), 0.0) * (ig * u)

    def step(gf, gb, hf, hb):
        rf = pl.multiple_of(gf * SUBLANES, SUBLANES)
        a, b = _scan_group(af_ref[pl.ds(rf, SUBLANES), :], bf_ref[pl.ds(rf, SUBLANES), :], False)
        h = a * hf + b
        hf_ref[pl.ds(rf, SUBLANES), :] = h
        hf = jnp.broadcast_to(h[SUBLANES - 1:SUBLANES, :], h.shape)
        rb = pl.multiple_of(gb * SUBLANES, SUBLANES)
        a, b = _scan_group(ab_ref[pl.ds(rb, SUBLANES), :], bb_ref[pl.ds(rb, SUBLANES), :], True)
        h = a * hb + b
        hb_ref[pl.ds(rb, SUBLANES), :] = h
        hb = jnp.broadcast_to(h[0:1, :], h.shape)
        return hf, hb

    zero = jnp.zeros((SUBLANES, LRU_BW), F32)
    carry = lax.fori_loop(0, CTX_GROUPS, lambda i, c: step(i, CTX_GROUPS - 1 - i, *c), (zero, zero), unroll=2)
    lax.fori_loop(0, LAT_GROUPS, lambda i, c: step(CTX_GROUPS + i, CTX_GROUPS + LAT_GROUPS - 1 - i, *c), carry, unroll=2)
    y_ref[...] = ((hf_ref[...] + hb_ref[...]) * jax.nn.gelu(gr_ref[...])).astype(y_ref.dtype)


def _rglru(xr, gr, conv_w, conv_b, wa, ba, wx, bx, lam):
    blk = lambda b, n: (b, n)
    col = lambda b, n: (0, n)
    gate = lambda b, n: (0, n, 0, 0)
    seg = pltpu.VMEM((NB, LRU_BW), F32)
    return pl.pallas_call(
        _rglru_kernel,
        grid=(BATCH, LRU_BLOCKS),
        in_specs=[pl.BlockSpec((NB, LRU_BW), blk), pl.BlockSpec((NB, LRU_BW), blk),
                  pl.BlockSpec((4, LRU_BW), col), pl.BlockSpec((1, LRU_BW), col),
                  pl.BlockSpec((2, 1, LRU_BW, LRU_BW), gate), pl.BlockSpec((2, LRU_BW), col),
                  pl.BlockSpec((2, 1, LRU_BW, LRU_BW), gate), pl.BlockSpec((2, LRU_BW), col),
                  pl.BlockSpec((2, LRU_BW), col)],
        out_specs=pl.BlockSpec((NB, LRU_BW), blk),
        out_shape=jax.ShapeDtypeStruct((TOK, LRU_WIDTH), BF16),
        scratch_shapes=[seg, seg, seg, seg, seg, seg],
        compiler_params=_cparams(("parallel", "parallel")),
        name="rglru",
    )(xr, gr, conv_w, conv_b.reshape(1, LRU_WIDTH), wa, ba, wx, bx, lam)


def _gather_rows(ys_hbm, y_buf, sem, dest_ref, base, n_tok):
    def row_copy(k, r, d):
        return pltpu.make_async_copy(ys_hbm.at[pl.ds(d, 1)], y_buf.at[k, pl.ds(r, 1)], sem)

    def start():
        def body(r, c):
            for k in range(2):
                row_copy(k, r, dest_ref[k * n_tok + base + r]).start(priority=k)
            return c

        lax.fori_loop(0, TM, body, 0, unroll=8)

    def wait():
        def body(r, c):
            for k in range(2):
                row_copy(k, r, 0).wait()
            return c

        lax.fori_loop(0, TM, body, 0, unroll=8)

    return start, wait


L1_Q = WIN_HEADS * HEAD_DIM
L1_KV = WIN_KV_HEADS * HEAD_DIM
L1_NA = NA_HEADS * HEAD_DIM
L1_IN = L1_Q + 2 * L1_KV + 3 * L1_NA


def _in1_kernel(dest_ref, ys_hbm, x_ref, gate_ref, g2_ref, g_ref, sh_ref, sc_ref, w_ref, cos_ref, sin_ref,
                xo_ref, qw_ref, kw_ref, vw_ref, qn_ref, kn_ref, vn_ref, y_buf, sem):
    i = pl.program_id(0)
    slot = i % 2
    rows = lambda tile, sl: _gather_rows(ys_hbm, y_buf.at[sl], sem.at[sl], dest_ref, tile * TM, TOK)

    @pl.when(i == 0)
    def _():
        rows(0, 0)[0]()

    rows(i, slot)[1]()

    @pl.when(i + 1 < pl.num_programs(0))
    def _():
        rows(i + 1, 1 - slot)[0]()

    gate = gate_ref[...]
    x = x_ref[...] + g2_ref[0] * (gate[:, 4:5] * y_buf[slot, 0] + gate[:, 5:6] * y_buf[slot, 1])
    xo_ref[...] = x
    h = _rms(x, g_ref[...]) * (1.0 + sc_ref[0]) + sh_ref[0]
    y = _dot(h.astype(BF16), w_ref[...])
    cos = cos_ref[...]
    sin = sin_ref[...]
    scale = LOG2E * HEAD_DIM ** -0.5
    for hd in range(WIN_HEADS):
        lo = hd * HEAD_DIM
        qw_ref[:, lo:lo + HEAD_DIM] = _rope(y[:, lo:lo + HEAD_DIM] * scale, cos, sin, HEAD_DIM // 4).astype(BF16)
    for hd in range(WIN_KV_HEADS):
        lo = hd * HEAD_DIM
        kw_ref[:, lo:lo + HEAD_DIM] = _rope(y[:, L1_Q + lo:L1_Q + lo + HEAD_DIM], cos, sin, HEAD_DIM // 4).astype(BF16)
    o = L1_Q + L1_KV
    vw_ref[...] = y[:, o:o + L1_KV].astype(BF16)
    o += L1_KV
    qn_ref[...] = (y[:, o:o + L1_NA] * scale).astype(BF16)
    kn_ref[...] = y[:, o + L1_NA:o + 2 * L1_NA].astype(BF16)
    vn_ref[...] = y[:, o + 2 * L1_NA:].astype(BF16)


def _in1(dest, ys, x, gates, mod_prev, g, mod, w, cos, sin):
    row = lambda i, d: (i, 0)
    pos = lambda i, d: (i % TPB, 0)
    widths = (L1_Q, L1_KV, L1_KV, L1_NA, L1_NA, L1_NA)
    return pl.pallas_call(
        _in1_kernel,
        grid_spec=pltpu.PrefetchScalarGridSpec(
            num_scalar_prefetch=1,
            grid=(TOK // TM,),
            in_specs=[pl.BlockSpec(memory_space=pl.ANY), pl.BlockSpec((TM, D_MODEL), row),
                      pl.BlockSpec((TM, SUBLANES), row), _mod_spec(_comb_tile, 5), _resident((1, D_MODEL)),
                      _mod_spec(_comb_tile, 0), _mod_spec(_comb_tile, 1), _resident((D_MODEL, L1_IN)),
                      pl.BlockSpec((TM, LANES), pos), pl.BlockSpec((TM, LANES), pos)],
            out_specs=[pl.BlockSpec((TM, D_MODEL), row)] + [pl.BlockSpec((TM, n), row) for n in widths],
            scratch_shapes=[pltpu.VMEM((2, 2, TM, D_MODEL), F32), pltpu.SemaphoreType.DMA((2,))]),
        out_shape=[jax.ShapeDtypeStruct((TOK, D_MODEL), F32)] + [jax.ShapeDtypeStruct((TOK, n), BF16) for n in widths],
        compiler_params=_cparams(("arbitrary",)),
        name="l1_in_proj",
    )(dest, ys, x, gates, mod_prev, g, mod, mod, w, cos, sin)


WIN_TQ = 128
WIN_SPAN = WIN_TQ + 2 * WINDOW
WIN_G = WIN_HEADS // WIN_KV_HEADS


def _win_key_start(n):
    return min(max((n - 1) * WIN_TQ, 0), SEQ - WIN_SPAN)


WIN_OFFSETS = sorted({n * WIN_TQ - _win_key_start(n) for n in range(SEQ // WIN_TQ)})


def _win_mask_bias():
    qoff = jnp.arange(WIN_G * WIN_TQ)[:, None] % WIN_TQ
    koff = jnp.arange(WIN_SPAN)[None, :]
    return jnp.stack([jnp.where(jnp.abs(qoff - koff + d) <= WINDOW, 0.0, NEG_INF) for d in WIN_OFFSETS]).astype(F32)


def _win_kernel(sink_ref, q_ref, k_ref, v_ref, mask_ref, o_ref):
    hk = pl.program_id(1)
    o_ref[0:CTX_LEN, :] = jnp.zeros((CTX_LEN, WIN_G * HEAD_DIM), o_ref.dtype)
    rows = WIN_G * WIN_TQ
    head = lax.broadcasted_iota(jnp.int32, (rows, 1), 0) // WIN_TQ
    sink = jnp.zeros((rows, 1), F32)
    for g in range(WIN_G):
        sink = jnp.where(head == g, sink_ref[hk * WIN_G + g] * LOG2E, sink)

    for n in range(SEQ // WIN_TQ):
        r0 = CTX_LEN + n * WIN_TQ
        start = _win_key_start(n)
        ks = CTX_LEN + start
        q4 = q_ref[r0:r0 + WIN_TQ, :]
        q = jnp.concatenate([q4[:, g * HEAD_DIM:(g + 1) * HEAD_DIM] for g in range(WIN_G)], axis=0)
        s_c = _dot_nt(q, k_ref[0:CTX_LEN, :])
        s_w = _dot_nt(q, k_ref[ks:ks + WIN_SPAN, :]) + mask_ref[WIN_OFFSETS.index(n * WIN_TQ - start)]
        m = jnp.maximum(jnp.maximum(jnp.max(s_c, axis=-1, keepdims=True), jnp.max(s_w, axis=-1, keepdims=True)), sink)
        p_c = jnp.exp2(s_c - m)
        p_w = jnp.exp2(s_w - m)
        l = jnp.sum(p_c, axis=-1, keepdims=True) + jnp.sum(p_w, axis=-1, keepdims=True) + jnp.exp2(sink - m)
        o = (_dot(p_w.astype(BF16), v_ref[ks:ks + WIN_SPAN, :]) + _dot(p_c.astype(BF16), v_ref[0:CTX_LEN, :])) / l
        for g in range(WIN_G):
            o_ref[r0:r0 + WIN_TQ, g * HEAD_DIM:(g + 1) * HEAD_DIM] = o[g * WIN_TQ:(g + 1) * WIN_TQ].astype(o_ref.dtype)


def _win_attn(sink, q, k, v):
    blk = lambda b, h, *_: (b, h)
    mask = _win_mask_bias()
    return pl.pallas_call(
        _win_kernel,
        grid_spec=pltpu.PrefetchScalarGridSpec(
            num_scalar_prefetch=1,
            grid=(BATCH, WIN_KV_HEADS),
            in_specs=[pl.BlockSpec((NB, WIN_G * HEAD_DIM), blk), pl.BlockSpec((NB, HEAD_DIM), blk),
                      pl.BlockSpec((NB, HEAD_DIM), blk), _resident(mask.shape)],
            out_specs=pl.BlockSpec((NB, WIN_G * HEAD_DIM), blk)),
        out_shape=jax.ShapeDtypeStruct((TOK, L1_Q), BF16),
        compiler_params=_cparams(("parallel", "parallel")),
        name="window_attn",
    )(sink, q, k, v, mask)


NA_GRID_ROWS = SEQ // GRID_W
NA_BAND = NA_ROWS * GRID_W
NA_RPI = 4


def _na_kernel(q_ref, k_ref, v_ref, bias_ref, o_ref):
    o_ref[0:CTX_LEN, :] = jnp.zeros((CTX_LEN, HEAD_DIM), o_ref.dtype)

    for i in range(NA_GRID_ROWS // NA_RPI):
        qs = CTX_LEN + i * NA_RPI * GRID_W
        q = q_ref[qs:qs + NA_RPI * GRID_W, :]
        s_c = _dot_nt(q, k_ref[0:CTX_LEN, :])
        starts = []
        s_w = []
        for j in range(NA_RPI):
            r = i * NA_RPI + j
            r0 = min(max(r - NA_ROWS // 2, 0), NA_GRID_ROWS - NA_ROWS)
            ks = CTX_LEN + r0 * GRID_W
            starts.append(ks)
            s_w.append(_dot_nt(q[j * GRID_W:(j + 1) * GRID_W], k_ref[ks:ks + NA_BAND, :])
                       + jnp.concatenate([bias_ref[0, r0 - r + NA_ROWS - 1 + 2 * p] for p in range(NA_ROWS // 2)],
                                         axis=1))
        s_w = jnp.concatenate(s_w, axis=0)
        m = jnp.maximum(jnp.max(s_c, axis=-1, keepdims=True), jnp.max(s_w, axis=-1, keepdims=True))
        p_c = jnp.exp2(s_c - m)
        p_w = jnp.exp2(s_w - m)
        l = jnp.sum(p_c, axis=-1, keepdims=True) + jnp.sum(p_w, axis=-1, keepdims=True)
        p_w = p_w.astype(BF16)
        o_w = jnp.concatenate([_dot(p_w[j * GRID_W:(j + 1) * GRID_W], v_ref[starts[j]:starts[j] + NA_BAND, :])
                               for j in range(NA_RPI)], axis=0)
        o = (o_w + _dot(p_c.astype(BF16), v_ref[0:CTX_LEN, :])) / l
        o_ref[qs:qs + NA_RPI * GRID_W, :] = o.astype(o_ref.dtype)


def _na_attn(q, k, v, bias):
    blk = lambda b, h: (b, h)
    return pl.pallas_call(
        _na_kernel,
        grid=(BATCH, NA_HEADS),
        in_specs=[pl.BlockSpec((NB, HEAD_DIM), blk), pl.BlockSpec((NB, HEAD_DIM), blk),
                  pl.BlockSpec((NB, HEAD_DIM), blk),
                  pl.BlockSpec((1, 2 * NA_ROWS - 2, GRID_W, 2 * GRID_W), lambda b, h: (h, 0, 0, 0))],
        out_specs=pl.BlockSpec((NB, HEAD_DIM), blk),
        out_shape=jax.ShapeDtypeStruct((TOK, L1_NA), BF16),
        compiler_params=_cparams(("parallel", "parallel")),
        name="na_attn",
    )(q, k, v, bias)


def _na_bias_table(rpb):
    cols = jnp.arange(GRID_W)
    c0 = jnp.clip(cols - NA_COLS // 2, 0, GRID_W - NA_COLS)
    kc = cols[None, :]
    valid = (kc >= c0[:, None]) & (kc < c0[:, None] + NA_COLS)
    pick = (jnp.arange(2 * NA_COLS - 1)[:, None, None] == (kc - cols[:, None] + NA_COLS - 1)[None]).astype(F32)
    tbl = jnp.einsum("hdo,ock->hdck", rpb.astype(F32) * LOG2E, pick, precision=lax.Precision.HIGHEST)
    tbl = jnp.where(valid[None, None], tbl, NEG_INF)
    return jnp.concatenate([tbl[:, :-1], tbl[:, 1:]], axis=-1)


def _router_logits(h, rwh_ref, rwl_ref):
    hh = h.astype(BF16)
    hl = (h - hh.astype(F32)).astype(BF16)
    return _dot_nt(rwh_ref[...], hh) + (_dot_nt(rwh_ref[...], hl) + _dot_nt(rwl_ref[...], hh))


def _router(logits, rb_ref, tri_ref, carry):
    tm = logits.shape[1]
    scores = jax.nn.sigmoid(logits)
    biased = scores + rb_ref[...]
    nj = EXPERTS_PER_GROUP
    s = [biased[j * N_GROUPS:(j + 1) * N_GROUPS] for j in range(nj)]
    u = [scores[j * N_GROUPS:(j + 1) * N_GROUPS] for j in range(nj)]
    gs = None
    for a in range(nj):
        for b in range(a + 1, nj):
            pair = s[a] + s[b]
            gs = pair if gs is None else jnp.maximum(gs, pair)
    giota = lax.broadcasted_iota(jnp.int32, (N_GROUPS, tm), 0).astype(F32)
    gmax = jnp.max(gs, axis=0, keepdims=True)
    gidx = jnp.min(jnp.where(gs == gmax, giota, float(N_GROUPS)), axis=0, keepdims=True)
    gm = giota == gidx
    v = [jnp.sum(jnp.where(gm, s[j], 0.0), axis=0, keepdims=True) for j in range(nj)]
    w = [jnp.sum(jnp.where(gm, u[j], 0.0), axis=0, keepdims=True) for j in range(nj)]
    sel = []
    for j in range(nj):
        beaten = jnp.zeros((1, tm), F32)
        for i in range(nj):
            if i != j:
                ahead = (v[i] >= v[j]) if i < j else (v[i] > v[j])
                beaten = beaten + jnp.where(ahead, 1.0, 0.0)
        sel.append(beaten < 2.0)
    wsum = sum(jnp.where(sel[j], w[j], 0.0) for j in range(nj))
    first = functools.reduce(jnp.minimum, [jnp.where(sel[j], float(j), float(nj)) for j in range(nj)])
    last = functools.reduce(jnp.maximum, [jnp.where(sel[j], float(j), -1.0) for j in range(nj)])
    gmf = jnp.where(gm, 1.0, 0.0)
    cnt = jnp.concatenate([jnp.where(sel[j], gmf, 0.0) for j in range(nj)], axis=0)
    pos = _dot(cnt.astype(BF16), tri_ref[...]) + carry
    rank = [jnp.sum(cnt[j * N_GROUPS:(j + 1) * N_GROUPS] * pos[j * N_GROUPS:(j + 1) * N_GROUPS], axis=0, keepdims=True)
            for j in range(nj)]
    pick = lambda which, vals: sum(jnp.where(which == float(j), vals[j], 0.0) for j in range(nj))
    gate = [w[j] / wsum for j in range(nj)]
    zero = jnp.zeros((1, tm), F32)
    info = jnp.concatenate(
        [gidx * nj + first, gidx * nj + last, pick(first, rank), pick(last, rank), pick(first, gate), pick(last, gate),
         zero, zero], axis=0)
    return info, carry + jnp.sum(cnt, axis=1, keepdims=True)


OUT_SUB = 128


def _out_kernel(ya_ref, yb_ref, *refs, n_tiles, split_x):
    i = pl.program_id(0)
    if split_x:
        xc_ref, xl_ref, *refs = refs
        x_in = lambda: _pick_tile(jnp.minimum(i, n_tiles - 1), xc_ref, xl_ref)
    else:
        x_ref, *refs = refs
        x_in = lambda: x_ref[...]
    (w_ref, g1_ref, n_ref, sh_ref, sc_ref, rwh_ref, rwl_ref, rb_ref, tri_ref,
     xo_ref, h_ref, info_ref, counts_ref, carry_ref, hbuf_ref) = refs

    @pl.when(i == 0)
    def _():
        carry_ref[...] = jnp.zeros_like(carry_ref)
        hbuf_ref[1] = jnp.zeros((TM, D_MODEL), F32)

    subs = range(0, TM, OUT_SUB)
    logits = [_router_logits(hbuf_ref[(i + 1) % 2, r0:r0 + OUT_SUB, :], rwh_ref, rwl_ref) for r0 in subs]
    half = w_ref.shape[0] // 2
    y = _dot(ya_ref[...], w_ref[0:half, :]) + _dot(yb_ref[...], w_ref[half:, :])
    old = carry_ref[:, 0:1]
    carry = old
    for r0, lg in zip(subs, logits):
        info_ref[:, r0:r0 + OUT_SUB], carry = _router(lg, rb_ref, tri_ref, carry)
    counts = jnp.broadcast_to(jnp.where(i > 0, carry, old), carry_ref.shape)
    carry_ref[...] = counts
    counts_ref[...] = counts

    x = x_in() + g1_ref[0] * y
    xo_ref[...] = x
    h = _rms(x, n_ref[...]) * (1.0 + sc_ref[0]) + sh_ref[0]
    h_ref[...] = h
    hbuf_ref[i % 2] = h


def _out_proj(ya, yb, x, w_out, mod, norm, rwh, rwl, rb, tri, tile_fn, n_tiles):
    split_x = isinstance(x, tuple)
    half = w_out.shape[0] // 2
    this = lambda i: jnp.minimum(i, n_tiles - 1)
    tile = lambda i: tile_fn(this(i))
    src = lambda i: (tile(i)[0], 0)
    dst = lambda i: (this(i), 0)
    n_tok = n_tiles * TM
    x_specs = [_ctx_spec(this), _lat_spec(this)] if split_x else [pl.BlockSpec((TM, D_MODEL), src)]
    return pl.pallas_call(
        functools.partial(_out_kernel, n_tiles=n_tiles, split_x=split_x),
        grid=(n_tiles + 1,),
        in_specs=[pl.BlockSpec((TM, half), src), pl.BlockSpec((TM, half), src), *x_specs,
                  _resident(w_out.shape),
                  _mod_spec(tile, 2), _resident((1, D_MODEL)), _mod_spec(tile, 3), _mod_spec(tile, 4),
                  _resident((N_EXPERTS, D_MODEL)), _resident((N_EXPERTS, D_MODEL)), _resident((N_EXPERTS, 1)),
                  _resident((OUT_SUB, OUT_SUB))],
        out_specs=[pl.BlockSpec((TM, D_MODEL), dst), pl.BlockSpec((TM, D_MODEL), dst),
                   pl.BlockSpec((SUBLANES, TM), lambda i: (0, jnp.maximum(i - 1, 0))),
                   pl.BlockSpec((N_EXPERTS, LANES), lambda i: (0, 0))],
        out_shape=[jax.ShapeDtypeStruct((n_tok, D_MODEL), F32), jax.ShapeDtypeStruct((n_tok, D_MODEL), F32),
                   jax.ShapeDtypeStruct((SUBLANES, n_tok), F32), jax.ShapeDtypeStruct((N_EXPERTS, LANES), F32)],
        scratch_shapes=[pltpu.VMEM((N_EXPERTS, LANES), F32), pltpu.VMEM((2, TM, D_MODEL), F32)],
        compiler_params=_cparams(("arbitrary",)),
        name="out_proj_router",
    )(ya, yb, *(x if split_x else (x,)), w_out, mod, norm, mod, mod, rwh, rwl, rb, tri)


def _lookup(table, idx):
    onehot = idx[..., None] == jnp.arange(table.shape[0], dtype=jnp.int32)
    return jnp.sum(jnp.where(onehot, table, 0), axis=-1)


def _dispatch_plan(info, counts, n_tok):
    n_steps = 2 * n_tok // TMX + N_EXPERTS - 1
    experts = jnp.arange(N_EXPERTS, dtype=jnp.int32)
    cnt = counts[:, 0].astype(jnp.int32).reshape(EXPERTS_PER_GROUP, N_GROUPS).T.reshape(N_EXPERTS)
    end = jnp.cumsum(cnt)
    off = end - cnt
    first_tile = off // TMX
    visits = jnp.where(cnt > 0, (end - 1) // TMX - first_tile + 1, 0)
    visit_end = jnp.cumsum(visits)
    n_valid = visit_end[-1]
    dest = _lookup(off, info[0:2].astype(jnp.int32)) + info[2:4].astype(jnp.int32)
    step = jnp.minimum(jnp.arange(n_steps, dtype=jnp.int32), n_valid - 1)
    e = jnp.sum(visit_end[None, :] <= step[:, None], axis=1).astype(jnp.int32)
    k = step - _lookup(visit_end - visits, e)
    tile = _lookup(first_tile, e) + k
    lo = jnp.clip(_lookup(off, e) - tile * TMX, 0, TMX)
    hi = jnp.clip(_lookup(end, e) - tile * TMX, 0, TMX)
    later = (experts[None, :] > experts[:, None]) & (cnt[None, :] > 0)
    nxt = jnp.min(jnp.where(later, experts[None, :], N_EXPERTS), axis=1)
    nxt = jnp.where(nxt == N_EXPERTS, -1, nxt)
    slot = (jnp.cumsum((cnt > 0).astype(jnp.int32)) - 1) % 2
    i32 = lambda v: v.astype(jnp.int32)
    return (dest.reshape(-1), i32(tile), i32(e), i32(n_valid.reshape(1)), i32(lo), i32(hi), i32(k == 0),
            i32(_lookup(nxt, e)), i32(_lookup(slot, e)))


def _dispatch_kernel(dest_ref, h_ref, xs_hbm, sem, *, n_tok):
    i = pl.program_id(0)

    def row_copy(r, d):
        return pltpu.make_async_copy(h_ref.at[pl.ds(r, 1)], xs_hbm.at[pl.ds(d, 1)], sem)

    def start(r, c):
        for k in range(2):
            row_copy(r, dest_ref[k * n_tok + i * TM + r]).start(priority=k)
        return c

    lax.fori_loop(0, TM, start, 0, unroll=8)

    def wait(r, c):
        for k in range(2):
            row_copy(r, 0).wait()
        return c

    lax.fori_loop(0, TM, wait, 0, unroll=8)


def _dispatch(dest, h):
    n_tok = h.shape[0]
    return pl.pallas_call(
        functools.partial(_dispatch_kernel, n_tok=n_tok),
        grid_spec=pltpu.PrefetchScalarGridSpec(
            num_scalar_prefetch=1,
            grid=(n_tok // TM,),
            in_specs=[pl.BlockSpec((TM, D_MODEL), lambda i, d: (i, 0))],
            out_specs=pl.BlockSpec(memory_space=pl.ANY),
            scratch_shapes=[pltpu.SemaphoreType.DMA(())]),
        out_shape=jax.ShapeDtypeStruct((2 * n_tok, D_MODEL), F32),
        compiler_params=_cparams(("arbitrary",)),
        name="moe_dispatch",
    )(dest, h)


def _moe_kernel(tile_ref, te_ref, nv_ref, lo_ref, hi_ref, first_ref, nxt_ref, slot_ref,
                xs_ref, wg_hbm, wu_hbm, wd_hbm, ys_ref, wg_f, wu_f, wd_f, wg_s, wu_s, wd_s, sem):
    s = pl.program_id(0)

    def fetch(e, slot):
        return [pltpu.make_async_copy(src.at[e], dst.at[slot], sem.at[slot])
                for src, dst in ((wg_hbm, wg_f), (wu_hbm, wu_f), (wd_hbm, wd_f))]

    @pl.when(s < nv_ref[0])
    def _():
        @pl.when(first_ref[s] == 1)
        def _():
            slot = slot_ref[s]

            @pl.when(s == 0)
            def _():
                for c in fetch(te_ref[0], slot):
                    c.start()

            for c in fetch(te_ref[s], slot):
                c.wait()

            @pl.when(nxt_ref[s] >= 0)
            def _():
                for c in fetch(nxt_ref[s], 1 - slot):
                    c.start()

            wg_s[...] = wg_f[slot].astype(BF16)
            wu_s[...] = wu_f[slot].astype(BF16)
            wd_s[...] = wd_f[slot].astype(BF16)

        lo = lo_ref[s]
        hi = hi_ref[s]

        @pl.when(lo == 0)
        def _():
            ys_ref[...] = jnp.zeros_like(ys_ref)

        def visit(r0, n):
            x = xs_ref[r0:r0 + n, :].astype(BF16)
            hg = _dot(x, wg_s[...])
            he = (hg * jax.nn.sigmoid(hg)) * _dot(x, wu_s[...])
            y = _dot(he.astype(BF16), wd_s[...])
            row = r0 + lax.broadcasted_iota(jnp.int32, (n, 1), 0)
            ys_ref[r0:r0 + n, :] = jnp.where((row >= lo) & (row < hi), y, ys_ref[r0:r0 + n, :])

        half = TMX // 2
        lower = hi <= half
        upper = lo >= half
        pl.when(lower)(lambda: visit(0, half))
        pl.when(upper)(lambda: visit(half, half))
        pl.when(jnp.logical_not(jnp.logical_or(lower, upper)))(lambda: visit(0, TMX))


def _moe(plan, xs, w_gate, w_up, w_down):
    n_steps = plan[0].shape[0]
    tile = lambda s, t, *_: (t[s], 0)
    any_spec = pl.BlockSpec(memory_space=pl.ANY)
    return pl.pallas_call(
        _moe_kernel,
        grid_spec=pltpu.PrefetchScalarGridSpec(
            num_scalar_prefetch=len(plan),
            grid=(n_steps,),
            in_specs=[pl.BlockSpec((TMX, D_MODEL), tile), any_spec, any_spec, any_spec],
            out_specs=pl.BlockSpec((TMX, D_MODEL), tile),
            scratch_shapes=[pltpu.VMEM((2, D_MODEL, EXPERT_FF), F32), pltpu.VMEM((2, D_MODEL, EXPERT_FF), F32),
                            pltpu.VMEM((2, EXPERT_FF, D_MODEL), F32),
                            pltpu.VMEM((D_MODEL, EXPERT_FF), BF16), pltpu.VMEM((D_MODEL, EXPERT_FF), BF16),
                            pltpu.VMEM((EXPERT_FF, D_MODEL), BF16), pltpu.SemaphoreType.DMA((2,))]),
        out_shape=jax.ShapeDtypeStruct(xs.shape, F32),
        compiler_params=_cparams(("arbitrary",)),
        name="moe_experts",
    )(*plan, xs, w_gate, w_up, w_down)


def _combine_kernel(dest_ref, ys_hbm, x_ref, gate_ref, g2_ref, n_ref, o_ref, y_buf, sem, *, n_tok):
    start, wait = _gather_rows(ys_hbm, y_buf, sem, dest_ref, pl.program_id(0) * TM, n_tok)
    start()
    wait()
    gate = gate_ref[...]
    x = x_ref[...] + g2_ref[0] * (gate[:, 4:5] * y_buf[0] + gate[:, 5:6] * y_buf[1])
    o_ref[...] = _rms(x, n_ref[...])


def _combine(dest, ys, x, gates, mod, norm, tile_fn, n_tiles):
    n_tok = n_tiles * TM
    row = lambda i, d: (i, 0)
    return pl.pallas_call(
        functools.partial(_combine_kernel, n_tok=n_tok),
        grid_spec=pltpu.PrefetchScalarGridSpec(
            num_scalar_prefetch=1,
            grid=(n_tiles,),
            in_specs=[pl.BlockSpec(memory_space=pl.ANY), pl.BlockSpec((TM, D_MODEL), row),
                      pl.BlockSpec((TM, SUBLANES), row), _mod_spec(tile_fn, 5),
                      pl.BlockSpec((1, D_MODEL), lambda i, d: (0, 0))],
            out_specs=pl.BlockSpec((TM, D_MODEL), row),
            scratch_shapes=[pltpu.VMEM((2, TM, D_MODEL), F32), pltpu.SemaphoreType.DMA(())]),
        out_shape=jax.ShapeDtypeStruct((n_tok, D_MODEL), F32),
        compiler_params=_cparams(("arbitrary",)),
        name="moe_combine",
    )(dest, ys, x, gates, mod, norm)


def _experts(h, info, counts, w_gate, w_up, w_down):
    dest, *plan = _dispatch_plan(info, counts, h.shape[0])
    return dest, _moe(plan, _dispatch(dest, h), w_gate, w_up, w_down)


def _rope_tables(dim):
    half = dim // 2
    inv_freq = ROPE_THETA ** (-jnp.arange(0, half, 2, dtype=F32) / half)
    ang_r = jnp.arange(SEQ // GRID_W, dtype=F32)[:, None] * inv_freq
    ang_c = jnp.arange(GRID_W, dtype=F32)[:, None] * inv_freq
    by_row = lambda v: jnp.repeat(v, GRID_W, axis=0)
    by_col = lambda v: jnp.tile(v, (SEQ // GRID_W, 1))
    cos_r, sin_r, cos_c, sin_c = by_row(jnp.cos(ang_r)), by_row(jnp.sin(ang_r)), by_col(jnp.cos(ang_c)), by_col(jnp.sin(ang_c))
    cos = jnp.concatenate([cos_r, cos_r, cos_c, cos_c], axis=-1)
    sin = jnp.concatenate([-sin_r, sin_r, -sin_c, sin_c], axis=-1)
    pad = LANES - dim
    cos = jnp.pad(cos, ((0, 0), (0, pad)))
    sin = jnp.pad(sin, ((0, 0), (0, pad)))
    ctx_cos = jnp.pad(jnp.ones((CTX_LEN, dim), F32), ((0, 0), (0, pad)))
    return jnp.concatenate([ctx_cos, cos], axis=0), jnp.concatenate([jnp.zeros((CTX_LEN, LANES), F32), sin], axis=0)


def kernel(x, c, ctx, c_ctx, router_w, router_b, final_norm, l0_mod_w, l0_mod_b, l0_norm_mix, l0_norm_ffn, l0_w_in, l0_q_norm, l0_w_uq, l0_kv_norm, l0_w_ukv, l0_conv_w, l0_conv_b, l0_gate_a_w, l0_gate_a_b, l0_gate_x_w, l0_gate_x_b, l0_lru_lambda, l0_w_out, l0_exp_gate, l0_exp_up, l0_exp_down, l1_mod_w, l1_mod_b, l1_norm_mix, l1_norm_ffn, l1_w_in, l1_sink, l1_rpb, l1_w_out, l1_exp_gate, l1_exp_up, l1_exp_down):
    row = lambda v: v.reshape(1, -1)
    x_in = (ctx.reshape(BATCH * CTX_LEN, D_MODEL), x.reshape(BATCH * SEQ, D_MODEL))

    cs = jnp.concatenate([c, c_ctx[None], jnp.zeros((SUBLANES - BATCH - 1, D_MODEL), F32)], axis=0)
    mod0 = _modulation(cs, l0_mod_w, l0_mod_b)
    mod1 = _modulation(cs, l1_mod_w, l1_mod_b)

    perm = jnp.arange(N_EXPERTS).reshape(N_GROUPS, EXPERTS_PER_GROUP).T.reshape(-1)
    rwt = router_w.T[perm]
    rwh = rwt.astype(BF16)
    rwl = (rwt - rwh.astype(F32)).astype(BF16)
    rb = router_b[perm].reshape(N_EXPERTS, 1).astype(F32)
    tri = jnp.triu(jnp.ones((OUT_SUB, OUT_SUB), F32), k=1).astype(BF16)
    router = (rwh, rwl, rb, tri)

    split = MLA_Q_RANK + MLA_KV_RANK + MLA_ROPE
    w_in0 = jnp.concatenate([l0_w_in[:, :split], jnp.zeros((D_MODEL, LANES - MLA_ROPE), F32), l0_w_in[:, split:]],
                            axis=1).astype(BF16)
    cqkv, xr, gr = _in0(*x_in, row(l0_norm_mix), mod0, w_in0)
    wq = jnp.pad(l0_w_uq.reshape(MLA_Q_RANK, MLA_HEADS, MLA_NOPE + MLA_ROPE),
                 ((0, 0), (0, 0), (0, MLA_QK - MLA_NOPE - MLA_ROPE))).reshape(MLA_Q_RANK, MLA_HEADS * MLA_QK)
    wkv = l0_w_ukv.reshape(MLA_KV_RANK, MLA_HEADS, MLA_NOPE + MLA_V)
    wk = wkv[:, :, :MLA_NOPE].reshape(MLA_KV_RANK, MLA_HEADS * MLA_NOPE)
    wv = wkv[:, :, MLA_NOPE:].reshape(MLA_KV_RANK, MLA_HEADS * MLA_V)
    cos0, sin0 = _rope_tables(MLA_ROPE)
    q, k, v = _mla_proj(cqkv, row(l0_q_norm), row(l0_kv_norm), wq.astype(BF16), wk.astype(BF16), wv.astype(BF16),
                        cos0, sin0)
    att = _mla_attn(q, k, v)
    rnn = _rglru(xr, gr, l0_conv_w, l0_conv_b, l0_gate_a_w, l0_gate_a_b, l0_gate_x_w, l0_gate_x_b, l0_lru_lambda)
    n0 = TOK // TM
    xs, h, info, counts = _out_proj(att, rnn, x_in, l0_w_out.astype(BF16), mod0, row(l0_norm_ffn), *router,
                                    _comb_tile, n0)
    dest, ys = _experts(h, info, counts, l0_exp_gate, l0_exp_up, l0_exp_down)

    cos1, sin1 = _rope_tables(HEAD_DIM)
    xs, qw, kw, vw, qn, kn, vn = _in1(dest, ys, xs, info.T, mod0, row(l1_norm_mix), mod1, l1_w_in.astype(BF16),
                                      cos1, sin1)
    win = _win_attn(l1_sink.astype(F32), qw, kw, vw)
    na = _na_attn(qn, kn, vn, _na_bias_table(l1_rpb))
    n1 = BATCH * LAT_TPB
    xl, h, info, counts = _out_proj(win, na, xs, l1_w_out.astype(BF16), mod1, row(l1_norm_ffn), *router,
                                    _lat_tile, n1)
    dest, ys = _experts(h, info, counts, l1_exp_gate, l1_exp_up, l1_exp_down)
    out = _combine(dest, ys, xl, info.T, mod1, row(final_norm), lambda i: (i, i // LAT_TPB), n1)
    return out.reshape(BATCH, SEQ, D_MODEL)
```

```python
import functools

import jax
import jax.numpy as jnp
from jax import lax
from jax.experimental import pallas as pl
from jax.experimental.pallas import tpu as pltpu

F32 = jnp.float32
BF16 = jnp.bfloat16

D_MODEL = 2048
BATCH = 4
SEQ = 2048
GRID_W = 64
CTX_LEN = 256
EPS = 1e-6
NEG_INF = -1e30
ROPE_THETA = 10000.0
N_MOD = 6

MLA_HEADS = 8
MLA_Q_RANK = 512
MLA_KV_RANK = 256
MLA_NOPE = 128
MLA_ROPE = 64
MLA_V = 128

LRU_WIDTH = 1024
LRU_BLOCKS = 8
LRU_C = 8.0

HEAD_DIM = 128
WIN_HEADS = 8
WIN_KV_HEADS = 2
WINDOW = 128
NA_HEADS = 8
NA_ROWS = 8
NA_COLS = 16

N_EXPERTS = 32
N_GROUPS = 8
EXPERTS_PER_GROUP = 4
EXPERT_FF = 512

LANES = 128
SUBLANES = 8
VMEM_LIMIT = 56 * 1024 * 1024

NB = CTX_LEN + SEQ
TOK = BATCH * NB
TM = 256
TPB = NB // TM
LAT_TPB = SEQ // TM
TMX = 256
MLA_QK = 2 * LANES
LOG2E = 1.4426950408889634


def _cparams(sem):
    return pltpu.CompilerParams(dimension_semantics=sem, vmem_limit_bytes=VMEM_LIMIT)


def _resident(shape):
    nd = len(shape)
    return pl.BlockSpec(shape, lambda *_: (0,) * nd, pipeline_mode=pl.Buffered(1))


def _rms(x, g):
    return x * lax.rsqrt(jnp.mean(x * x, axis=-1, keepdims=True) + EPS) * g


def _sigmoid(x):
    return 0.5 * jnp.tanh(0.5 * x) + 0.5


def _dot(a, b):
    return jnp.dot(a, b, preferred_element_type=F32)


def _dot_nt(a, b):
    return lax.dot_general(a, b, (((1,), (1,)), ((), ())), preferred_element_type=F32)


def _swap_blocks(x, blk):
    lane = lax.broadcasted_iota(jnp.int32, x.shape, 1)
    nxt = pltpu.roll(x, LANES - blk, axis=1)
    prv = pltpu.roll(x, blk, axis=1)
    return jnp.where((lane % (2 * blk)) < blk, nxt, prv)


def _rope(x, cos, sin, blk):
    return x * cos + _swap_blocks(x, blk) * sin


def _comb_tile(i):
    b = i // TPB
    return i, jnp.where(i % TPB == 0, BATCH, b)


def _lat_tile(i):
    b = i // LAT_TPB
    return b * TPB + 1 + i % LAT_TPB, b


def _ctx_spec(tile_fn):
    return pl.BlockSpec((TM, D_MODEL), lambda i, *_: (tile_fn(i) // TPB, 0))


def _lat_spec(tile_fn):
    return pl.BlockSpec((TM, D_MODEL),
                        lambda i, *_: (tile_fn(i) // TPB * LAT_TPB + jnp.maximum(tile_fn(i) % TPB - 1, 0), 0))


def _pick_tile(tile, ctx_ref, lat_ref):
    return jnp.where(tile % TPB == 0, ctx_ref[...], lat_ref[...])


def _mod_spec(tile_fn, k):
    return pl.BlockSpec((1, 1, D_MODEL), lambda i, *_: (tile_fn(i)[1] * N_MOD + k, 0, 0))


def _mod_kernel(c_ref, w_ref, b_ref, o_ref):
    c = c_ref[...]
    a = (c * jax.nn.sigmoid(c)).astype(BF16)
    o_ref[...] = _dot(a, w_ref[...].astype(BF16)) + b_ref[...]


def _modulation(cs, w, b):
    n = N_MOD * D_MODEL
    tn = 1024
    out = pl.pallas_call(
        _mod_kernel,
        grid=(n // tn,),
        in_specs=[pl.BlockSpec((SUBLANES, D_MODEL), lambda j: (0, 0)),
                  pl.BlockSpec((D_MODEL, tn), lambda j: (0, j)),
                  pl.BlockSpec((1, tn), lambda j: (0, j))],
        out_specs=pl.BlockSpec((SUBLANES, tn), lambda j: (0, j)),
        out_shape=jax.ShapeDtypeStruct((SUBLANES, n), F32),
        compiler_params=_cparams(("arbitrary",)),
        name="modulation",
    )(cs, w, b.reshape(1, n))
    return out.reshape(SUBLANES * N_MOD, 1, D_MODEL)


L0_CQKV = MLA_Q_RANK + MLA_KV_RANK + LANES
L0_IN_PAD = L0_CQKV + 2 * LRU_WIDTH


def _in0_kernel(xc_ref, xl_ref, g_ref, sh_ref, sc_ref, w_ref, cqkv_ref, xr_ref, gr_ref):
    x = _pick_tile(pl.program_id(0), xc_ref, xl_ref)
    h = _rms(x, g_ref[...]) * (1.0 + sc_ref[0]) + sh_ref[0]
    y = _dot(h.astype(BF16), w_ref[...])
    cqkv_ref[...] = y[:, :L0_CQKV]
    xr_ref[...] = y[:, L0_CQKV:L0_CQKV + LRU_WIDTH]
    gr_ref[...] = y[:, L0_CQKV + LRU_WIDTH:]


def _in0(x_ctx, x_lat, g, mod, w):
    row = lambda i: (i, 0)
    tile = lambda i: i
    return pl.pallas_call(
        _in0_kernel,
        grid=(TOK // TM,),
        in_specs=[_ctx_spec(tile), _lat_spec(tile), _resident((1, D_MODEL)),
                  _mod_spec(_comb_tile, 0), _mod_spec(_comb_tile, 1), _resident((D_MODEL, L0_IN_PAD))],
        out_specs=[pl.BlockSpec((TM, L0_CQKV), row), pl.BlockSpec((TM, LRU_WIDTH), row),
                   pl.BlockSpec((TM, LRU_WIDTH), row)],
        out_shape=[jax.ShapeDtypeStruct((TOK, L0_CQKV), F32), jax.ShapeDtypeStruct((TOK, LRU_WIDTH), F32),
                   jax.ShapeDtypeStruct((TOK, LRU_WIDTH), F32)],
        compiler_params=_cparams(("parallel",)),
        name="l0_in_proj",
    )(x_ctx, x_lat, g, mod, mod, w)


def _mla_proj_kernel(c_ref, qn_ref, kvn_ref, wq_ref, wk_ref, wv_ref, cos_ref, sin_ref, q_ref, k_ref, v_ref):
    c = c_ref[...]
    cos = cos_ref[...]
    sin = sin_ref[...]
    nq = _rms(c[:, :MLA_Q_RANK], qn_ref[...]).astype(BF16)
    q = _dot(nq, wq_ref[...]) * (LOG2E * (MLA_NOPE + MLA_ROPE) ** -0.5)
    nkv = _rms(c[:, MLA_Q_RANK:MLA_Q_RANK + MLA_KV_RANK], kvn_ref[...]).astype(BF16)
    kn = _dot(nkv, wk_ref[...])
    v_ref[...] = _dot(nkv, wv_ref[...]).astype(BF16)
    kr = _rope(c[:, MLA_Q_RANK + MLA_KV_RANK:], cos, sin, MLA_ROPE // 4).astype(BF16)
    for h in range(MLA_HEADS):
        lo = h * MLA_QK
        q_ref[:, lo:lo + LANES] = q[:, lo:lo + LANES].astype(BF16)
        q_ref[:, lo + LANES:lo + MLA_QK] = _rope(q[:, lo + LANES:lo + MLA_QK], cos, sin, MLA_ROPE // 4).astype(BF16)
        k_ref[:, lo:lo + LANES] = kn[:, h * LANES:(h + 1) * LANES].astype(BF16)
        k_ref[:, lo + LANES:lo + MLA_QK] = kr


def _mla_proj(cqkv, qn, kvn, wq, wk, wv, cos, sin):
    row = lambda i: (i, 0)
    pos = lambda i: (i % TPB, 0)
    hq = MLA_HEADS * MLA_QK
    hv = MLA_HEADS * MLA_V
    return pl.pallas_call(
        _mla_proj_kernel,
        grid=(TOK // TM,),
        in_specs=[pl.BlockSpec((TM, L0_CQKV), row), _resident((1, MLA_Q_RANK)), _resident((1, MLA_KV_RANK)),
                  _resident((MLA_Q_RANK, hq)), _resident((MLA_KV_RANK, hv)), _resident((MLA_KV_RANK, hv)),
                  pl.BlockSpec((TM, LANES), pos), pl.BlockSpec((TM, LANES), pos)],
        out_specs=[pl.BlockSpec((TM, hq), row), pl.BlockSpec((TM, hq), row), pl.BlockSpec((TM, hv), row)],
        out_shape=[jax.ShapeDtypeStruct((TOK, hq), BF16), jax.ShapeDtypeStruct((TOK, hq), BF16),
                   jax.ShapeDtypeStruct((TOK, hv), BF16)],
        compiler_params=_cparams(("parallel",)),
        name="mla_proj",
    )(cqkv, qn, kvn, wq, wk, wv, cos, sin)


MLA_TQ = 256


def _softmax_pv(s, v):
    m = jnp.max(s, axis=-1, keepdims=True)
    p = jnp.exp2(s - m)
    l = jnp.sum(p, axis=-1, keepdims=True)
    return _dot(p.astype(BF16), v) / l


def _mla_attn_kernel(q_ref, k_ref, v_ref, o_ref):
    s = _dot_nt(q_ref[0:CTX_LEN, :], k_ref[0:CTX_LEN, :])
    o_ref[0:CTX_LEN, :] = _softmax_pv(s, v_ref[0:CTX_LEN, :]).astype(o_ref.dtype)
    for t in range(SEQ // MLA_TQ):
        r0 = CTX_LEN + t * MLA_TQ
        s = _dot_nt(q_ref[r0:r0 + MLA_TQ, :], k_ref[...])
        o_ref[r0:r0 + MLA_TQ, :] = _softmax_pv(s, v_ref[...]).astype(o_ref.dtype)


def _mla_attn(q, k, v):
    blk = lambda b, h: (b, h)
    return pl.pallas_call(
        _mla_attn_kernel,
        grid=(BATCH, MLA_HEADS),
        in_specs=[pl.BlockSpec((NB, MLA_QK), blk), pl.BlockSpec((NB, MLA_QK), blk), pl.BlockSpec((NB, MLA_V), blk)],
        out_specs=pl.BlockSpec((NB, MLA_V), blk),
        out_shape=jax.ShapeDtypeStruct((TOK, MLA_HEADS * MLA_V), BF16),
        compiler_params=_cparams(("parallel", "parallel")),
        name="mla_attn",
    )(q, k, v)


LRU_BW = LRU_WIDTH // LRU_BLOCKS
CTX_GROUPS = CTX_LEN // SUBLANES
LAT_GROUPS = SEQ // SUBLANES


def _scan_group(a, b, reverse):
    row = lax.broadcasted_iota(jnp.int32, a.shape, 0)
    for d in (1, 2, 4):
        shift = SUBLANES - d if reverse else d
        a_s = pltpu.roll(a, shift, axis=0)
        b_s = pltpu.roll(b, shift, axis=0)
        m = (row < SUBLANES - d) if reverse else (row >= d)
        b = jnp.where(m, a * b_s + b, b)
        a = jnp.where(m, a * a_s, a)
    return a, b


def _rglru_kernel(xr_ref, gr_ref, cw_ref, cb_ref, wa_ref, ba_ref, wx_ref, bx_ref, lam_ref, y_ref,
                  af_ref, bf_ref, ab_ref, bb_ref, hf_ref, hb_ref):
    row8 = lax.broadcasted_iota(jnp.int32, (SUBLANES, LRU_BW), 0)

    def taps(seg):
        n = seg.shape[0]

        def shifted(shift, keep, first):
            r = pltpu.roll(seg, shift % n, axis=0)
            if first:
                return jnp.concatenate([jnp.where(keep, r[:SUBLANES], 0.0), r[SUBLANES:]], axis=0)
            return jnp.concatenate([r[:-SUBLANES], jnp.where(keep, r[-SUBLANES:], 0.0)], axis=0)

        return shifted(2, row8 >= 2, True), shifted(1, row8 >= 1, True), shifted(-1, row8 < SUBLANES - 1, False)

    x = xr_ref[...]
    tc = taps(x[:CTX_LEN])
    tl = taps(x[CTX_LEN:])
    xm2, xm1, xp1 = [jnp.concatenate([a, b], axis=0) for a, b in zip(tc, tl)]
    u = cb_ref[...] + xm2 * cw_ref[0:1, :] + xm1 * cw_ref[1:2, :] + x * cw_ref[2:3, :] + xp1 * cw_ref[3:4, :]
    ub = u.astype(BF16)
    for d, (a_ref, b_ref) in enumerate(((af_ref, bf_ref), (ab_ref, bb_ref))):
        r = _sigmoid(_dot(ub, wa_ref[d, 0].astype(BF16)) + ba_ref[d:d + 1, :])
        ig = _sigmoid(_dot(ub, wx_ref[d, 0].astype(BF16)) + bx_ref[d:d + 1, :])
        z = -lam_ref[d:d + 1, :]
        softplus = jnp.maximum(z, 0.0) + jnp.log(1.0 + jnp.exp(-jnp.abs(z)))
        log_a = -LRU_C * r * softplus
        a = jnp.exp(log_a)
        a_ref[...] = a
        t = 1.0 - a * a
        b_ref[...] = jnp.where(t > 0.0, t * lax.rsqrt(t), 0.0) * (ig * u)

    def step(gf, gb, hf, hb):
        rf = pl.multiple_of(gf * SUBLANES, SUBLANES)
        a, b = _scan_group(af_ref[pl.ds(rf, SUBLANES), :], bf_ref[pl.ds(rf, SUBLANES), :], False)
        h = a * hf + b
        hf_ref[pl.ds(rf, SUBLANES), :] = h
        hf = jnp.broadcast_to(h[SUBLANES - 1:SUBLANES, :], h.shape)
        rb = pl.multiple_of(gb * SUBLANES, SUBLANES)
        a, b = _scan_group(ab_ref[pl.ds(rb, SUBLANES), :], bb_ref[pl.ds(rb, SUBLANES), :], True)
        h = a * hb + b
        hb_ref[pl.ds(rb, SUBLANES), :] = h
        hb = jnp.broadcast_to(h[0:1, :], h.shape)
        return hf, hb

    zero = jnp.zeros((SUBLANES, LRU_BW), F32)
    carry = lax.fori_loop(0, CTX_GROUPS, lambda i, c: step(i, CTX_GROUPS - 1 - i, *c), (zero, zero), unroll=2)
    lax.fori_loop(0, LAT_GROUPS, lambda i, c: step(CTX_GROUPS + i, CTX_GROUPS + LAT_GROUPS - 1 - i, *c), carry, unroll=2)
    y_ref[...] = ((hf_ref[...] + hb_ref[...]) * jax.nn.gelu(gr_ref[...])).astype(y_ref.dtype)


def _rglru(xr, gr, conv_w, conv_b, wa, ba, wx, bx, lam):
    blk = lambda b, n: (b, n)
    col = lambda b, n: (0, n)
    gate = lambda b, n: (0, n, 0, 0)
    seg = pltpu.VMEM((NB, LRU_BW), F32)
    return pl.pallas_call(
        _rglru_kernel,
        grid=(BATCH, LRU_BLOCKS),
        in_specs=[pl.BlockSpec((NB, LRU_BW), blk), pl.BlockSpec((NB, LRU_BW), blk),
                  pl.BlockSpec((4, LRU_BW), col), pl.BlockSpec((1, LRU_BW), col),
                  pl.BlockSpec((2, 1, LRU_BW, LRU_BW), gate), pl.BlockSpec((2, LRU_BW), col),
                  pl.BlockSpec((2, 1, LRU_BW, LRU_BW), gate), pl.BlockSpec((2, LRU_BW), col),
                  pl.BlockSpec((2, LRU_BW), col)],
        out_specs=pl.BlockSpec((NB, LRU_BW), blk),
        out_shape=jax.ShapeDtypeStruct((TOK, LRU_WIDTH), BF16),
        scratch_shapes=[seg, seg, seg, seg, seg, seg],
        compiler_params=_cparams(("parallel", "parallel")),
        name="rglru",
    )(xr, gr, conv_w, conv_b.reshape(1, LRU_WIDTH), wa, ba, wx, bx, lam)


def _gather_rows(ys_hbm, y_buf, sem, dest_ref, base, n_tok):
    def row_copy(k, r, d):
        return pltpu.make_async_copy(ys_hbm.at[pl.ds(d, 1)], y_buf.at[k, pl.ds(r, 1)], sem)

    def start(straight_line=False):
        def body(r, c):
            for k in range(2):
                row_copy(k, r, dest_ref[k * n_tok + base + r]).start(priority=k)
            return c

        if straight_line:
            for r in range(TM):
                body(r, 0)
        else:
            lax.fori_loop(0, TM, body, 0, unroll=8)

    def wait():
        def body(r, c):
            for k in range(2):
                row_copy(k, r, 0).wait()
            return c

        lax.fori_loop(0, TM, body, 0, unroll=8)

    return start, wait


L1_Q = WIN_HEADS * HEAD_DIM
L1_KV = WIN_KV_HEADS * HEAD_DIM
L1_NA = NA_HEADS * HEAD_DIM
L1_IN = L1_Q + 2 * L1_KV + 3 * L1_NA


def _in1_kernel(dest_ref, ys_hbm, x_ref, gate_ref, g2_ref, g_ref, sh_ref, sc_ref, w_ref, cos_ref, sin_ref,
                xo_ref, qw_ref, kw_ref, vw_ref, qn_ref, kn_ref, vn_ref, y_buf, sem):
    i = pl.program_id(0)
    last = pl.num_programs(0) - 1
    slot = i % 2
    rows = lambda tile, sl: _gather_rows(ys_hbm, y_buf.at[sl], sem.at[sl], dest_ref, tile * TM, TOK)

    @pl.when(i == 0)
    def _():
        rows(0, 0)[0]()

    rows(i, slot)[1]()
    rows(jnp.minimum(i + 1, last), 1 - slot)[0](straight_line=True)

    gate = gate_ref[...]
    x = x_ref[...] + g2_ref[0] * (gate[:, 4:5] * y_buf[slot, 0] + gate[:, 5:6] * y_buf[slot, 1])
    xo_ref[...] = x
    h = _rms(x, g_ref[...]) * (1.0 + sc_ref[0]) + sh_ref[0]
    y = _dot(h.astype(BF16), w_ref[...])
    cos = cos_ref[...]
    sin = sin_ref[...]
    scale = LOG2E * HEAD_DIM ** -0.5
    for hd in range(WIN_HEADS):
        lo = hd * HEAD_DIM
        qw_ref[:, lo:lo + HEAD_DIM] = _rope(y[:, lo:lo + HEAD_DIM] * scale, cos, sin, HEAD_DIM // 4).astype(BF16)
    for hd in range(WIN_KV_HEADS):
        lo = hd * HEAD_DIM
        kw_ref[:, lo:lo + HEAD_DIM] = _rope(y[:, L1_Q + lo:L1_Q + lo + HEAD_DIM], cos, sin, HEAD_DIM // 4).astype(BF16)
    o = L1_Q + L1_KV
    vw_ref[...] = y[:, o:o + L1_KV].astype(BF16)
    o += L1_KV
    qn_ref[...] = (y[:, o:o + L1_NA] * scale).astype(BF16)
    kn_ref[...] = y[:, o + L1_NA:o + 2 * L1_NA].astype(BF16)
    vn_ref[...] = y[:, o + 2 * L1_NA:].astype(BF16)

    @pl.when(i == last)
    def _():
        rows(last, 1 - slot)[1]()


def _in1(dest, ys, x, gates, mod_prev, g, mod, w, cos, sin):
    row = lambda i, d: (i, 0)
    pos = lambda i, d: (i % TPB, 0)
    widths = (L1_Q, L1_KV, L1_KV, L1_NA, L1_NA, L1_NA)
    return pl.pallas_call(
        _in1_kernel,
        grid_spec=pltpu.PrefetchScalarGridSpec(
            num_scalar_prefetch=1,
            grid=(TOK // TM,),
            in_specs=[pl.BlockSpec(memory_space=pl.ANY), pl.BlockSpec((TM, D_MODEL), row),
                      pl.BlockSpec((TM, SUBLANES), row), _mod_spec(_comb_tile, 5), _resident((1, D_MODEL)),
                      _mod_spec(_comb_tile, 0), _mod_spec(_comb_tile, 1), _resident((D_MODEL, L1_IN)),
                      pl.BlockSpec((TM, LANES), pos), pl.BlockSpec((TM, LANES), pos)],
            out_specs=[pl.BlockSpec((TM, D_MODEL), row)] + [pl.BlockSpec((TM, n), row) for n in widths],
            scratch_shapes=[pltpu.VMEM((2, 2, TM, D_MODEL), F32), pltpu.SemaphoreType.DMA((2,))]),
        out_shape=[jax.ShapeDtypeStruct((TOK, D_MODEL), F32)] + [jax.ShapeDtypeStruct((TOK, n), BF16) for n in widths],
        compiler_params=_cparams(("arbitrary",)),
        name="l1_in_proj",
    )(dest, ys, x, gates, mod_prev, g, mod, mod, w, cos, sin)


WIN_TQ = 128
WIN_SPAN = WIN_TQ + 2 * WINDOW
WIN_G = WIN_HEADS // WIN_KV_HEADS


def _win_key_start(n):
    return min(max((n - 1) * WIN_TQ, 0), SEQ - WIN_SPAN)


WIN_OFFSETS = sorted({n * WIN_TQ - _win_key_start(n) for n in range(SEQ // WIN_TQ)})


def _win_mask_bias():
    qoff = jnp.arange(WIN_G * WIN_TQ)[:, None] % WIN_TQ
    koff = jnp.arange(WIN_SPAN)[None, :]
    return jnp.stack([jnp.where(jnp.abs(qoff - koff + d) <= WINDOW, 0.0, NEG_INF) for d in WIN_OFFSETS]).astype(F32)


def _win_kernel(sink_ref, q_ref, k_ref, v_ref, mask_ref, o_ref):
    hk = pl.program_id(1)
    o_ref[0:CTX_LEN, :] = jnp.zeros((CTX_LEN, WIN_G * HEAD_DIM), o_ref.dtype)
    rows = WIN_G * WIN_TQ
    head = lax.broadcasted_iota(jnp.int32, (rows, 1), 0) // WIN_TQ
    sink = jnp.zeros((rows, 1), F32)
    for g in range(WIN_G):
        sink = jnp.where(head == g, sink_ref[hk * WIN_G + g] * LOG2E, sink)

    for n in range(SEQ // WIN_TQ):
        r0 = CTX_LEN + n * WIN_TQ
        start = _win_key_start(n)
        ks = CTX_LEN + start
        q4 = q_ref[r0:r0 + WIN_TQ, :]
        q = jnp.concatenate([q4[:, g * HEAD_DIM:(g + 1) * HEAD_DIM] for g in range(WIN_G)], axis=0)
        s_c = _dot_nt(q, k_ref[0:CTX_LEN, :])
        s_w = _dot_nt(q, k_ref[ks:ks + WIN_SPAN, :]) + mask_ref[WIN_OFFSETS.index(n * WIN_TQ - start)]
        m = jnp.maximum(jnp.maximum(jnp.max(s_c, axis=-1, keepdims=True), jnp.max(s_w, axis=-1, keepdims=True)), sink)
        p_c = jnp.exp2(s_c - m)
        p_w = jnp.exp2(s_w - m)
        l = jnp.sum(p_c, axis=-1, keepdims=True) + jnp.sum(p_w, axis=-1, keepdims=True) + jnp.exp2(sink - m)
        o = (_dot(p_w.astype(BF16), v_ref[ks:ks + WIN_SPAN, :]) + _dot(p_c.astype(BF16), v_ref[0:CTX_LEN, :])) / l
        for g in range(WIN_G):
            o_ref[r0:r0 + WIN_TQ, g * HEAD_DIM:(g + 1) * HEAD_DIM] = o[g * WIN_TQ:(g + 1) * WIN_TQ].astype(o_ref.dtype)


def _win_attn(sink, q, k, v):
    blk = lambda b, h, *_: (b, h)
    mask = _win_mask_bias()
    return pl.pallas_call(
        _win_kernel,
        grid_spec=pltpu.PrefetchScalarGridSpec(
            num_scalar_prefetch=1,
            grid=(BATCH, WIN_KV_HEADS),
            in_specs=[pl.BlockSpec((NB, WIN_G * HEAD_DIM), blk), pl.BlockSpec((NB, HEAD_DIM), blk),
                      pl.BlockSpec((NB, HEAD_DIM), blk), _resident(mask.shape)],
            out_specs=pl.BlockSpec((NB, WIN_G * HEAD_DIM), blk)),
        out_shape=jax.ShapeDtypeStruct((TOK, L1_Q), BF16),
        compiler_params=_cparams(("parallel", "parallel")),
        name="window_attn",
    )(sink, q, k, v, mask)


NA_GRID_ROWS = SEQ // GRID_W
NA_BAND = NA_ROWS * GRID_W
NA_RPI = 4


def _na_kernel(q_ref, k_ref, v_ref, bias_ref, o_ref):
    o_ref[0:CTX_LEN, :] = jnp.zeros((CTX_LEN, HEAD_DIM), o_ref.dtype)

    for i in range(NA_GRID_ROWS // NA_RPI):
        qs = CTX_LEN + i * NA_RPI * GRID_W
        q = q_ref[qs:qs + NA_RPI * GRID_W, :]
        s_c = _dot_nt(q, k_ref[0:CTX_LEN, :])
        starts = []
        s_w = []
        for j in range(NA_RPI):
            r = i * NA_RPI + j
            r0 = min(max(r - NA_ROWS // 2, 0), NA_GRID_ROWS - NA_ROWS)
            ks = CTX_LEN + r0 * GRID_W
            starts.append(ks)
            s_w.append(_dot_nt(q[j * GRID_W:(j + 1) * GRID_W], k_ref[ks:ks + NA_BAND, :])
                       + jnp.concatenate([bias_ref[0, r0 - r + NA_ROWS - 1 + 2 * p] for p in range(NA_ROWS // 2)],
                                         axis=1))
        s_w = jnp.concatenate(s_w, axis=0)
        m = jnp.maximum(jnp.max(s_c, axis=-1, keepdims=True), jnp.max(s_w, axis=-1, keepdims=True))
        p_c = jnp.exp2(s_c - m)
        p_w = jnp.exp2(s_w - m)
        l = jnp.sum(p_c, axis=-1, keepdims=True) + jnp.sum(p_w, axis=-1, keepdims=True)
        p_w = p_w.astype(BF16)
        o_w = jnp.concatenate([_dot(p_w[j * GRID_W:(j + 1) * GRID_W], v_ref[starts[j]:starts[j] + NA_BAND, :])
                               for j in range(NA_RPI)], axis=0)
        o = (o_w + _dot(p_c.astype(BF16), v_ref[0:CTX_LEN, :])) / l
        o_ref[qs:qs + NA_RPI * GRID_W, :] = o.astype(o_ref.dtype)


def _na_attn(q, k, v, bias):
    blk = lambda b, h: (b, h)
    return pl.pallas_call(
        _na_kernel,
        grid=(BATCH, NA_HEADS),
        in_specs=[pl.BlockSpec((NB, HEAD_DIM), blk), pl.BlockSpec((NB, HEAD_DIM), blk),
                  pl.BlockSpec((NB, HEAD_DIM), blk),
                  pl.BlockSpec((1, 2 * NA_ROWS - 2, GRID_W, 2 * GRID_W), lambda b, h: (h, 0, 0, 0))],
        out_specs=pl.BlockSpec((NB, HEAD_DIM), blk),
        out_shape=jax.ShapeDtypeStruct((TOK, L1_NA), BF16),
        compiler_params=_cparams(("parallel", "parallel")),
        name="na_attn",
    )(q, k, v, bias)


def _na_bias_table(rpb):
    cols = jnp.arange(GRID_W)
    c0 = jnp.clip(cols - NA_COLS // 2, 0, GRID_W - NA_COLS)
    kc = cols[None, :]
    valid = (kc >= c0[:, None]) & (kc < c0[:, None] + NA_COLS)
    pick = (jnp.arange(2 * NA_COLS - 1)[:, None, None] == (kc - cols[:, None] + NA_COLS - 1)[None]).astype(F32)
    tbl = jnp.einsum("hdo,ock->hdck", rpb.astype(F32) * LOG2E, pick, precision=lax.Precision.HIGHEST)
    tbl = jnp.where(valid[None, None], tbl, NEG_INF)
    return jnp.concatenate([tbl[:, :-1], tbl[:, 1:]], axis=-1)


def _router_logits(h, rwh_ref, rwl_ref):
    hh = h.astype(BF16)
    hl = (h - hh.astype(F32)).astype(BF16)
    return _dot_nt(rwh_ref[...], hh) + (_dot_nt(rwh_ref[...], hl) + _dot_nt(rwl_ref[...], hh))


def _router(logits, rb_ref, tri_ref, carry):
    tm = logits.shape[1]
    scores = jax.nn.sigmoid(logits)
    biased = scores + rb_ref[...]
    nj = EXPERTS_PER_GROUP
    s = [biased[j * N_GROUPS:(j + 1) * N_GROUPS] for j in range(nj)]
    u = [scores[j * N_GROUPS:(j + 1) * N_GROUPS] for j in range(nj)]
    gs = None
    for a in range(nj):
        for b in range(a + 1, nj):
            pair = s[a] + s[b]
            gs = pair if gs is None else jnp.maximum(gs, pair)
    giota = lax.broadcasted_iota(jnp.int32, (N_GROUPS, tm), 0).astype(F32)
    gmax = jnp.max(gs, axis=0, keepdims=True)
    gidx = jnp.min(jnp.where(gs == gmax, giota, float(N_GROUPS)), axis=0, keepdims=True)
    gm = giota == gidx
    v = [jnp.sum(jnp.where(gm, s[j], 0.0), axis=0, keepdims=True) for j in range(nj)]
    w = [jnp.sum(jnp.where(gm, u[j], 0.0), axis=0, keepdims=True) for j in range(nj)]
    sel = []
    for j in range(nj):
        beaten = jnp.zeros((1, tm), F32)
        for i in range(nj):
            if i != j:
                ahead = (v[i] >= v[j]) if i < j else (v[i] > v[j])
                beaten = beaten + jnp.where(ahead, 1.0, 0.0)
        sel.append(beaten < 2.0)
    wsum = sum(jnp.where(sel[j], w[j], 0.0) for j in range(nj))
    first = functools.reduce(jnp.minimum, [jnp.where(sel[j], float(j), float(nj)) for j in range(nj)])
    last = functools.reduce(jnp.maximum, [jnp.where(sel[j], float(j), -1.0) for j in range(nj)])
    gmf = jnp.where(gm, 1.0, 0.0)
    cnt = jnp.concatenate([jnp.where(sel[j], gmf, 0.0) for j in range(nj)], axis=0)
    pos = _dot(cnt.astype(BF16), tri_ref[...]) + carry
    rank = [jnp.sum(cnt[j * N_GROUPS:(j + 1) * N_GROUPS] * pos[j * N_GROUPS:(j + 1) * N_GROUPS], axis=0, keepdims=True)
            for j in range(nj)]
    pick = lambda which, vals: sum(jnp.where(which == float(j), vals[j], 0.0) for j in range(nj))
    gate = [w[j] / wsum for j in range(nj)]
    zero = jnp.zeros((1, tm), F32)
    info = jnp.concatenate(
        [gidx * nj + first, gidx * nj + last, pick(first, rank), pick(last, rank), pick(first, gate), pick(last, gate),
         zero, zero], axis=0)
    return info, carry + jnp.sum(cnt, axis=1, keepdims=True)


OUT_SUB = 128


def _out_kernel(ya_ref, yb_ref, *refs, n_tiles, split_x):
    i = pl.program_id(0)
    if split_x:
        xc_ref, xl_ref, *refs = refs
        x_in = lambda: _pick_tile(jnp.minimum(i, n_tiles - 1), xc_ref, xl_ref)
    else:
        x_ref, *refs = refs
        x_in = lambda: x_ref[...]
    (w_ref, g1_ref, n_ref, sh_ref, sc_ref, rwh_ref, rwl_ref, rb_ref, tri_ref,
     xo_ref, h_ref, info_ref, counts_ref, carry_ref, hbuf_ref) = refs

    @pl.when(i == 0)
    def _():
        carry_ref[...] = jnp.zeros_like(carry_ref)
        hbuf_ref[1] = jnp.zeros((TM, D_MODEL), F32)

    subs = range(0, TM, OUT_SUB)
    logits = [_router_logits(hbuf_ref[(i + 1) % 2, r0:r0 + OUT_SUB, :], rwh_ref, rwl_ref) for r0 in subs]
    half = w_ref.shape[0] // 2
    y = _dot(ya_ref[...], w_ref[0:half, :]) + _dot(yb_ref[...], w_ref[half:, :])
    old = carry_ref[:, 0:1]
    carry = old
    for r0, lg in zip(subs, logits):
        info_ref[:, r0:r0 + OUT_SUB], carry = _router(lg, rb_ref, tri_ref, carry)
    counts = jnp.broadcast_to(jnp.where(i > 0, carry, old), carry_ref.shape)
    carry_ref[...] = counts
    counts_ref[...] = counts

    x = x_in() + g1_ref[0] * y
    xo_ref[...] = x
    h = _rms(x, n_ref[...]) * (1.0 + sc_ref[0]) + sh_ref[0]
    h_ref[...] = h
    hbuf_ref[i % 2] = h


def _out_proj(ya, yb, x, w_out, mod, norm, rwh, rwl, rb, tri, tile_fn, n_tiles):
    split_x = isinstance(x, tuple)
    half = w_out.shape[0] // 2
    this = lambda i: jnp.minimum(i, n_tiles - 1)
    tile = lambda i: tile_fn(this(i))
    src = lambda i: (tile(i)[0], 0)
    dst = lambda i: (this(i), 0)
    n_tok = n_tiles * TM
    x_specs = [_ctx_spec(this), _lat_spec(this)] if split_x else [pl.BlockSpec((TM, D_MODEL), src)]
    return pl.pallas_call(
        functools.partial(_out_kernel, n_tiles=n_tiles, split_x=split_x),
        grid=(n_tiles + 1,),
        in_specs=[pl.BlockSpec((TM, half), src), pl.BlockSpec((TM, half), src), *x_specs,
                  _resident(w_out.shape),
                  _mod_spec(tile, 2), _resident((1, D_MODEL)), _mod_spec(tile, 3), _mod_spec(tile, 4),
                  _resident((N_EXPERTS, D_MODEL)), _resident((N_EXPERTS, D_MODEL)), _resident((N_EXPERTS, 1)),
                  _resident((OUT_SUB, OUT_SUB))],
        out_specs=[pl.BlockSpec((TM, D_MODEL), dst), pl.BlockSpec((TM, D_MODEL), dst),
                   pl.BlockSpec((SUBLANES, TM), lambda i: (0, jnp.maximum(i - 1, 0))),
                   pl.BlockSpec((N_EXPERTS, LANES), lambda i: (0, 0))],
        out_shape=[jax.ShapeDtypeStruct((n_tok, D_MODEL), F32), jax.ShapeDtypeStruct((n_tok, D_MODEL), F32),
                   jax.ShapeDtypeStruct((SUBLANES, n_tok), F32), jax.ShapeDtypeStruct((N_EXPERTS, LANES), F32)],
        scratch_shapes=[pltpu.VMEM((N_EXPERTS, LANES), F32), pltpu.VMEM((2, TM, D_MODEL), F32)],
        compiler_params=_cparams(("arbitrary",)),
        name="out_proj_router",
    )(ya, yb, *(x if split_x else (x,)), w_out, mod, norm, mod, mod, rwh, rwl, rb, tri)


def _lookup(table, idx):
    onehot = idx[..., None] == jnp.arange(table.shape[0], dtype=jnp.int32)
    return jnp.sum(jnp.where(onehot, table, 0), axis=-1)


def _dispatch_plan(info, counts, n_tok):
    n_steps = 2 * n_tok // TMX + N_EXPERTS - 1
    experts = jnp.arange(N_EXPERTS, dtype=jnp.int32)
    cnt = counts[:, 0].astype(jnp.int32).reshape(EXPERTS_PER_GROUP, N_GROUPS).T.reshape(N_EXPERTS)
    end = jnp.cumsum(cnt)
    off = end - cnt
    first_tile = off // TMX
    visits = jnp.where(cnt > 0, (end - 1) // TMX - first_tile + 1, 0)
    visit_end = jnp.cumsum(visits)
    n_valid = visit_end[-1]
    dest = _lookup(off, info[0:2].astype(jnp.int32)) + info[2:4].astype(jnp.int32)
    step = jnp.minimum(jnp.arange(n_steps, dtype=jnp.int32), n_valid - 1)
    e = jnp.sum(visit_end[None, :] <= step[:, None], axis=1).astype(jnp.int32)
    k = step - _lookup(visit_end - visits, e)
    tile = _lookup(first_tile, e) + k
    lo = jnp.clip(_lookup(off, e) - tile * TMX, 0, TMX)
    hi = jnp.clip(_lookup(end, e) - tile * TMX, 0, TMX)
    later = (experts[None, :] > experts[:, None]) & (cnt[None, :] > 0)
    nxt = jnp.min(jnp.where(later, experts[None, :], N_EXPERTS), axis=1)
    nxt = jnp.where(nxt == N_EXPERTS, -1, nxt)
    slot = (jnp.cumsum((cnt > 0).astype(jnp.int32)) - 1) % 2
    i32 = lambda v: v.astype(jnp.int32)
    return (dest.reshape(-1), i32(tile), i32(e), i32(n_valid.reshape(1)), i32(lo), i32(hi), i32(k == 0),
            i32(_lookup(nxt, e)), i32(_lookup(slot, e)))


def _dispatch_kernel(dest_ref, h_ref, xs_hbm, sem, *, n_tok):
    i = pl.program_id(0)

    def row_copy(r, d):
        return pltpu.make_async_copy(h_ref.at[pl.ds(r, 1)], xs_hbm.at[pl.ds(d, 1)], sem)

    def start(r, c):
        for k in range(2):
            row_copy(r, dest_ref[k * n_tok + i * TM + r]).start(priority=k)
        return c

    lax.fori_loop(0, TM, start, 0, unroll=8)

    def wait(r, c):
        for k in range(2):
            row_copy(r, 0).wait()
        return c

    lax.fori_loop(0, TM, wait, 0, unroll=8)


def _dispatch(dest, h):
    n_tok = h.shape[0]
    return pl.pallas_call(
        functools.partial(_dispatch_kernel, n_tok=n_tok),
        grid_spec=pltpu.PrefetchScalarGridSpec(
            num_scalar_prefetch=1,
            grid=(n_tok // TM,),
            in_specs=[pl.BlockSpec((TM, D_MODEL), lambda i, d: (i, 0))],
            out_specs=pl.BlockSpec(memory_space=pl.ANY),
            scratch_shapes=[pltpu.SemaphoreType.DMA(())]),
        out_shape=jax.ShapeDtypeStruct((2 * n_tok, D_MODEL), F32),
        compiler_params=_cparams(("arbitrary",)),
        name="moe_dispatch",
    )(dest, h)


def _moe_kernel(tile_ref, te_ref, nv_ref, lo_ref, hi_ref, first_ref, nxt_ref, slot_ref,
                xs_ref, wg_hbm, wu_hbm, wd_hbm, ys_ref, wg_f, wu_f, wd_f, wg_s, wu_s, wd_s, sem):
    s = pl.program_id(0)

    def fetch(e, slot):
        return [pltpu.make_async_copy(src.at[e], dst.at[slot], sem.at[slot])
                for src, dst in ((wg_hbm, wg_f), (wu_hbm, wu_f), (wd_hbm, wd_f))]

    @pl.when(s < nv_ref[0])
    def _():
        @pl.when(first_ref[s] == 1)
        def _():
            slot = slot_ref[s]

            @pl.when(s == 0)
            def _():
                for c in fetch(te_ref[0], slot):
                    c.start()

            for c in fetch(te_ref[s], slot):
                c.wait()

            @pl.when(nxt_ref[s] >= 0)
            def _():
                for c in fetch(nxt_ref[s], 1 - slot):
                    c.start()

            wg_s[...] = wg_f[slot].astype(BF16)
            wu_s[...] = wu_f[slot].astype(BF16)
            wd_s[...] = wd_f[slot].astype(BF16)

        lo = lo_ref[s]
        hi = hi_ref[s]

        @pl.when(lo == 0)
        def _():
            ys_ref[...] = jnp.zeros_like(ys_ref)

        def visit(r0, n):
            x = xs_ref[r0:r0 + n, :].astype(BF16)
            hg = _dot(x, wg_s[...])
            he = (hg * jax.nn.sigmoid(hg)) * _dot(x, wu_s[...])
            y = _dot(he.astype(BF16), wd_s[...])
            row = r0 + lax.broadcasted_iota(jnp.int32, (n, 1), 0)
            ys_ref[r0:r0 + n, :] = jnp.where((row >= lo) & (row < hi), y, ys_ref[r0:r0 + n, :])

        half = TMX // 2
        lower = hi <= half
        upper = lo >= half
        pl.when(lower)(lambda: visit(0, half))
        pl.when(upper)(lambda: visit(half, half))
        pl.when(jnp.logical_not(jnp.logical_or(lower, upper)))(lambda: visit(0, TMX))


def _moe(plan, xs, w_gate, w_up, w_down):
    n_steps = plan[0].shape[0]
    tile = lambda s, t, *_: (t[s], 0)
    any_spec = pl.BlockSpec(memory_space=pl.ANY)
    return pl.pallas_call(
        _moe_kernel,
        grid_spec=pltpu.PrefetchScalarGridSpec(
            num_scalar_prefetch=len(plan),
            grid=(n_steps,),
            in_specs=[pl.BlockSpec((TMX, D_MODEL), tile), any_spec, any_spec, any_spec],
            out_specs=pl.BlockSpec((TMX, D_MODEL), tile),
            scratch_shapes=[pltpu.VMEM((2, D_MODEL, EXPERT_FF), F32), pltpu.VMEM((2, D_MODEL, EXPERT_FF), F32),
                            pltpu.VMEM((2, EXPERT_FF, D_MODEL), F32),
                            pltpu.VMEM((D_MODEL, EXPERT_FF), BF16), pltpu.VMEM((D_MODEL, EXPERT_FF), BF16),
                            pltpu.VMEM((EXPERT_FF, D_MODEL), BF16), pltpu.SemaphoreType.DMA((2,))]),
        out_shape=jax.ShapeDtypeStruct(xs.shape, F32),
        compiler_params=_cparams(("arbitrary",)),
        name="moe_experts",
    )(*plan, xs, w_gate, w_up, w_down)


def _combine_kernel(dest_ref, ys_hbm, x_ref, gate_ref, g2_ref, n_ref, o_ref, y_buf, sem, *, n_tok):
    start, wait = _gather_rows(ys_hbm, y_buf, sem, dest_ref, pl.program_id(0) * TM, n_tok)
    start()
    wait()
    gate = gate_ref[...]
    x = x_ref[...] + g2_ref[0] * (gate[:, 4:5] * y_buf[0] + gate[:, 5:6] * y_buf[1])
    o_ref[...] = _rms(x, n_ref[...])


def _combine(dest, ys, x, gates, mod, norm, tile_fn, n_tiles):
    n_tok = n_tiles * TM
    row = lambda i, d: (i, 0)
    return pl.pallas_call(
        functools.partial(_combine_kernel, n_tok=n_tok),
        grid_spec=pltpu.PrefetchScalarGridSpec(
            num_scalar_prefetch=1,
            grid=(n_tiles,),
            in_specs=[pl.BlockSpec(memory_space=pl.ANY), pl.BlockSpec((TM, D_MODEL), row),
                      pl.BlockSpec((TM, SUBLANES), row), _mod_spec(tile_fn, 5),
                      pl.BlockSpec((1, D_MODEL), lambda i, d: (0, 0))],
            out_specs=pl.BlockSpec((TM, D_MODEL), row),
            scratch_shapes=[pltpu.VMEM((2, TM, D_MODEL), F32), pltpu.SemaphoreType.DMA(())]),
        out_shape=jax.ShapeDtypeStruct((n_tok, D_MODEL), F32),
        compiler_params=_cparams(("arbitrary",)),
        name="moe_combine",
    )(dest, ys, x, gates, mod, norm)


def _experts(h, info, counts, w_gate, w_up, w_down):
    dest, *plan = _dispatch_plan(info, counts, h.shape[0])
    return dest, _moe(plan, _dispatch(dest, h), w_gate, w_up, w_down)


def _rope_tables(dim):
    half = dim // 2
    inv_freq = ROPE_THETA ** (-jnp.arange(0, half, 2, dtype=F32) / half)
    ang_r = jnp.arange(SEQ // GRID_W, dtype=F32)[:, None] * inv_freq
    ang_c = jnp.arange(GRID_W, dtype=F32)[:, None] * inv_freq
    by_row = lambda v: jnp.repeat(v, GRID_W, axis=0)
    by_col = lambda v: jnp.tile(v, (SEQ // GRID_W, 1))
    cos_r, sin_r, cos_c, sin_c = by_row(jnp.cos(ang_r)), by_row(jnp.sin(ang_r)), by_col(jnp.cos(ang_c)), by_col(jnp.sin(ang_c))
    cos = jnp.concatenate([cos_r, cos_r, cos_c, cos_c], axis=-1)
    sin = jnp.concatenate([-sin_r, sin_r, -sin_c, sin_c], axis=-1)
    pad = LANES - dim
    cos = jnp.pad(cos, ((0, 0), (0, pad)))
    sin = jnp.pad(sin, ((0, 0), (0, pad)))
    ctx_cos = jnp.pad(jnp.ones((CTX_LEN, dim), F32), ((0, 0), (0, pad)))
    return jnp.concatenate([ctx_cos, cos], axis=0), jnp.concatenate([jnp.zeros((CTX_LEN, LANES), F32), sin], axis=0)


def kernel(x, c, ctx, c_ctx, router_w, router_b, final_norm, l0_mod_w, l0_mod_b, l0_norm_mix, l0_norm_ffn, l0_w_in, l0_q_norm, l0_w_uq, l0_kv_norm, l0_w_ukv, l0_conv_w, l0_conv_b, l0_gate_a_w, l0_gate_a_b, l0_gate_x_w, l0_gate_x_b, l0_lru_lambda, l0_w_out, l0_exp_gate, l0_exp_up, l0_exp_down, l1_mod_w, l1_mod_b, l1_norm_mix, l1_norm_ffn, l1_w_in, l1_sink, l1_rpb, l1_w_out, l1_exp_gate, l1_exp_up, l1_exp_down):
    row = lambda v: v.reshape(1, -1)
    x_in = (ctx.reshape(BATCH * CTX_LEN, D_MODEL), x.reshape(BATCH * SEQ, D_MODEL))

    cs = jnp.concatenate([c, c_ctx[None], jnp.zeros((SUBLANES - BATCH - 1, D_MODEL), F32)], axis=0)
    mod0 = _modulation(cs, l0_mod_w, l0_mod_b)
    mod1 = _modulation(cs, l1_mod_w, l1_mod_b)

    perm = jnp.arange(N_EXPERTS).reshape(N_GROUPS, EXPERTS_PER_GROUP).T.reshape(-1)
    rwt = router_w.T[perm]
    rwh = rwt.astype(BF16)
    rwl = (rwt - rwh.astype(F32)).astype(BF16)
    rb = router_b[perm].reshape(N_EXPERTS, 1).astype(F32)
    tri = jnp.triu(jnp.ones((OUT_SUB, OUT_SUB), F32), k=1).astype(BF16)
    router = (rwh, rwl, rb, tri)

    split = MLA_Q_RANK + MLA_KV_RANK + MLA_ROPE
    w_in0 = l0_w_in.astype(BF16)
    w_in0 = jnp.concatenate([w_in0[:, :split], jnp.zeros((D_MODEL, LANES - MLA_ROPE), BF16), w_in0[:, split:]], axis=1)
    cqkv, xr, gr = _in0(*x_in, row(l0_norm_mix), mod0, w_in0)
    wq = jnp.pad(l0_w_uq.reshape(MLA_Q_RANK, MLA_HEADS, MLA_NOPE + MLA_ROPE),
                 ((0, 0), (0, 0), (0, MLA_QK - MLA_NOPE - MLA_ROPE))).reshape(MLA_Q_RANK, MLA_HEADS * MLA_QK)
    wkv = l0_w_ukv.reshape(MLA_KV_RANK, MLA_HEADS, MLA_NOPE + MLA_V)
    wk = wkv[:, :, :MLA_NOPE].reshape(MLA_KV_RANK, MLA_HEADS * MLA_NOPE)
    wv = wkv[:, :, MLA_NOPE:].reshape(MLA_KV_RANK, MLA_HEADS * MLA_V)
    cos0, sin0 = _rope_tables(MLA_ROPE)
    q, k, v = _mla_proj(cqkv, row(l0_q_norm), row(l0_kv_norm), wq.astype(BF16), wk.astype(BF16), wv.astype(BF16),
                        cos0, sin0)
    att = _mla_attn(q, k, v)
    rnn = _rglru(xr, gr, l0_conv_w, l0_conv_b, l0_gate_a_w, l0_gate_a_b, l0_gate_x_w, l0_gate_x_b, l0_lru_lambda)
    n0 = TOK // TM
    xs, h, info, counts = _out_proj(att, rnn, x_in, l0_w_out.astype(BF16), mod0, row(l0_norm_ffn), *router,
                                    _comb_tile, n0)
    dest, ys = _experts(h, info, counts, l0_exp_gate, l0_exp_up, l0_exp_down)

    cos1, sin1 = _rope_tables(HEAD_DIM)
    xs, qw, kw, vw, qn, kn, vn = _in1(dest, ys, xs, info.T, mod0, row(l1_norm_mix), mod1, l1_w_in.astype(BF16),
                                      cos1, sin1)
    win = _win_attn(l1_sink.astype(F32), qw, kw, vw)
    na = _na_attn(qn, kn, vn, _na_bias_table(l1_rpb))
    n1 = BATCH * LAT_TPB
    xl, h, info, counts = _out_proj(win, na, xs, l1_w_out.astype(BF16), mod1, row(l1_norm_ffn), *router,
                                    _lat_tile, n1)
    dest, ys = _experts(h, info, counts, l1_exp_gate, l1_exp_up, l1_exp_down)
    out = _combine(dest, ys, xl, info.T, mod1, row(final_norm), lambda i: (i, i // LAT_TPB), n1)
    return out.reshape(BATCH, SEQ, D_MODEL)
```

```python
import functools

import jax
import jax.numpy as jnp
from jax import lax
from jax.experimental import pallas as pl
from jax.experimental.pallas import tpu as pltpu

F32 = jnp.float32
BF16 = jnp.bfloat16

D_MODEL = 2048
BATCH = 4
SEQ = 2048
GRID_W = 64
CTX_LEN = 256
EPS = 1e-6
NEG_INF = -1e30
ROPE_THETA = 10000.0
N_MOD = 6

MLA_HEADS = 8
MLA_Q_RANK = 512
MLA_KV_RANK = 256
MLA_NOPE = 128
MLA_ROPE = 64
MLA_V = 128

LRU_WIDTH = 1024
LRU_BLOCKS = 8
LRU_C = 8.0

HEAD_DIM = 128
WIN_HEADS = 8
WIN_KV_HEADS = 2
WINDOW = 128
NA_HEADS = 8
NA_ROWS = 8
NA_COLS = 16

N_EXPERTS = 32
N_GROUPS = 8
EXPERTS_PER_GROUP = 4
EXPERT_FF = 512

LANES = 128
SUBLANES = 8
VMEM_LIMIT = 56 * 1024 * 1024

NB = CTX_LEN + SEQ
TOK = BATCH * NB
TM = 256
TPB = NB // TM
LAT_TPB = SEQ // TM
TMX = 256
MLA_QK = 2 * LANES
LOG2E = 1.4426950408889634


def _cparams(sem):
    return pltpu.CompilerParams(dimension_semantics=sem, vmem_limit_bytes=VMEM_LIMIT)


def _resident(shape):
    nd = len(shape)
    return pl.BlockSpec(shape, lambda *_: (0,) * nd, pipeline_mode=pl.Buffered(1))


def _rms(x, g):
    return x * lax.rsqrt(jnp.mean(x * x, axis=-1, keepdims=True) + EPS) * g


def _sigmoid(x):
    return 0.5 * jnp.tanh(0.5 * x) + 0.5


def _dot(a, b):
    return jnp.dot(a, b, preferred_element_type=F32)


def _dot_nt(a, b):
    return lax.dot_general(a, b, (((1,), (1,)), ((), ())), preferred_element_type=F32)


def _swap_blocks(x, blk):
    lane = lax.broadcasted_iota(jnp.int32, x.shape, 1)
    nxt = pltpu.roll(x, LANES - blk, axis=1)
    prv = pltpu.roll(x, blk, axis=1)
    return jnp.where((lane % (2 * blk)) < blk, nxt, prv)


def _rope(x, cos, sin, blk):
    return x * cos + _swap_blocks(x, blk) * sin


def _comb_tile(i):
    b = i // TPB
    return i, jnp.where(i % TPB == 0, BATCH, b)


def _lat_tile(i):
    b = i // LAT_TPB
    return b * TPB + 1 + i % LAT_TPB, b


def _ctx_spec(tile_fn):
    return pl.BlockSpec((TM, D_MODEL), lambda i, *_: (tile_fn(i) // TPB, 0))


def _lat_spec(tile_fn):
    return pl.BlockSpec((TM, D_MODEL),
                        lambda i, *_: (tile_fn(i) // TPB * LAT_TPB + jnp.maximum(tile_fn(i) % TPB - 1, 0), 0))


def _pick_tile(tile, ctx_ref, lat_ref):
    return jnp.where(tile % TPB == 0, ctx_ref[...], lat_ref[...])


def _mod_spec(tile_fn, k):
    return pl.BlockSpec((1, 1, D_MODEL), lambda i, *_: (tile_fn(i)[1] * N_MOD + k, 0, 0))


def _mod_kernel(c_ref, w_ref, b_ref, o_ref):
    c = c_ref[...]
    a = (c * jax.nn.sigmoid(c)).astype(BF16)
    o_ref[...] = _dot(a, w_ref[...].astype(BF16)) + b_ref[...]


def _modulation(cs, w, b):
    n = N_MOD * D_MODEL
    tn = 1024
    out = pl.pallas_call(
        _mod_kernel,
        grid=(n // tn,),
        in_specs=[pl.BlockSpec((SUBLANES, D_MODEL), lambda j: (0, 0)),
                  pl.BlockSpec((D_MODEL, tn), lambda j: (0, j)),
                  pl.BlockSpec((1, tn), lambda j: (0, j))],
        out_specs=pl.BlockSpec((SUBLANES, tn), lambda j: (0, j)),
        out_shape=jax.ShapeDtypeStruct((SUBLANES, n), F32),
        compiler_params=_cparams(("arbitrary",)),
        name="modulation",
    )(cs, w, b.reshape(1, n))
    return out.reshape(SUBLANES * N_MOD, 1, D_MODEL)


L0_CQKV = MLA_Q_RANK + MLA_KV_RANK + LANES
L0_IN = MLA_Q_RANK + MLA_KV_RANK + MLA_ROPE + 2 * LRU_WIDTH


def _in0_kernel(xc_ref, xl_ref, g_ref, sh_ref, sc_ref, w_ref, cqkv_ref, xr_ref, gr_ref):
    x = _pick_tile(pl.program_id(0), xc_ref, xl_ref)
    h = _rms(x, g_ref[...]) * (1.0 + sc_ref[0]) + sh_ref[0]
    y = _dot(h.astype(BF16), w_ref[...])
    lat = MLA_Q_RANK + MLA_KV_RANK
    cqkv_ref[:, :lat] = y[:, :lat]
    tail = y[:, lat:L0_CQKV]
    lane = lax.broadcasted_iota(jnp.int32, tail.shape, 1)
    cqkv_ref[:, lat:] = jnp.where(lane < MLA_ROPE, tail, 0.0)
    xr_ref[...] = y[:, lat + MLA_ROPE:lat + MLA_ROPE + LRU_WIDTH]
    gr_ref[...] = y[:, lat + MLA_ROPE + LRU_WIDTH:]


def _in0(x_ctx, x_lat, g, mod, w):
    row = lambda i: (i, 0)
    tile = lambda i: i
    return pl.pallas_call(
        _in0_kernel,
        grid=(TOK // TM,),
        in_specs=[_ctx_spec(tile), _lat_spec(tile), _resident((1, D_MODEL)),
                  _mod_spec(_comb_tile, 0), _mod_spec(_comb_tile, 1), _resident((D_MODEL, L0_IN))],
        out_specs=[pl.BlockSpec((TM, L0_CQKV), row), pl.BlockSpec((TM, LRU_WIDTH), row),
                   pl.BlockSpec((TM, LRU_WIDTH), row)],
        out_shape=[jax.ShapeDtypeStruct((TOK, L0_CQKV), F32), jax.ShapeDtypeStruct((TOK, LRU_WIDTH), F32),
                   jax.ShapeDtypeStruct((TOK, LRU_WIDTH), F32)],
        compiler_params=_cparams(("parallel",)),
        name="l0_in_proj",
    )(x_ctx, x_lat, g, mod, mod, w)


def _mla_proj_kernel(c_ref, qn_ref, kvn_ref, wq_ref, wk_ref, wv_ref, cos_ref, sin_ref, q_ref, k_ref, v_ref):
    c = c_ref[...]
    cos = cos_ref[...]
    sin = sin_ref[...]
    nq = _rms(c[:, :MLA_Q_RANK], qn_ref[...]).astype(BF16)
    q = _dot(nq, wq_ref[...]) * (LOG2E * (MLA_NOPE + MLA_ROPE) ** -0.5)
    nkv = _rms(c[:, MLA_Q_RANK:MLA_Q_RANK + MLA_KV_RANK], kvn_ref[...]).astype(BF16)
    kn = _dot(nkv, wk_ref[...])
    v_ref[...] = _dot(nkv, wv_ref[...]).astype(BF16)
    kr = _rope(c[:, MLA_Q_RANK + MLA_KV_RANK:], cos, sin, MLA_ROPE // 4).astype(BF16)
    for h in range(MLA_HEADS):
        lo = h * MLA_QK
        q_ref[:, lo:lo + LANES] = q[:, lo:lo + LANES].astype(BF16)
        q_ref[:, lo + LANES:lo + MLA_QK] = _rope(q[:, lo + LANES:lo + MLA_QK], cos, sin, MLA_ROPE // 4).astype(BF16)
        k_ref[:, lo:lo + LANES] = kn[:, h * LANES:(h + 1) * LANES].astype(BF16)
        k_ref[:, lo + LANES:lo + MLA_QK] = kr


def _mla_proj(cqkv, qn, kvn, wq, wk, wv, cos, sin):
    row = lambda i: (i, 0)
    pos = lambda i: (i % TPB, 0)
    hq = MLA_HEADS * MLA_QK
    hv = MLA_HEADS * MLA_V
    return pl.pallas_call(
        _mla_proj_kernel,
        grid=(TOK // TM,),
        in_specs=[pl.BlockSpec((TM, L0_CQKV), row), _resident((1, MLA_Q_RANK)), _resident((1, MLA_KV_RANK)),
                  _resident((MLA_Q_RANK, hq)), _resident((MLA_KV_RANK, hv)), _resident((MLA_KV_RANK, hv)),
                  pl.BlockSpec((TM, LANES), pos), pl.BlockSpec((TM, LANES), pos)],
        out_specs=[pl.BlockSpec((TM, hq), row), pl.BlockSpec((TM, hq), row), pl.BlockSpec((TM, hv), row)],
        out_shape=[jax.ShapeDtypeStruct((TOK, hq), BF16), jax.ShapeDtypeStruct((TOK, hq), BF16),
                   jax.ShapeDtypeStruct((TOK, hv), BF16)],
        compiler_params=_cparams(("parallel",)),
        name="mla_proj",
    )(cqkv, qn, kvn, wq, wk, wv, cos, sin)


MLA_TQ = 256


def _softmax_pv(s, v):
    m = jnp.max(s, axis=-1, keepdims=True)
    p = jnp.exp2(s - m)
    l = jnp.sum(p, axis=-1, keepdims=True)
    return _dot(p.astype(BF16), v) / l


def _mla_attn_kernel(q_ref, k_ref, v_ref, o_ref):
    s = _dot_nt(q_ref[0:CTX_LEN, :], k_ref[0:CTX_LEN, :])
    o_ref[0:CTX_LEN, :] = _softmax_pv(s, v_ref[0:CTX_LEN, :]).astype(o_ref.dtype)
    for t in range(SEQ // MLA_TQ):
        r0 = CTX_LEN + t * MLA_TQ
        s = _dot_nt(q_ref[r0:r0 + MLA_TQ, :], k_ref[...])
        o_ref[r0:r0 + MLA_TQ, :] = _softmax_pv(s, v_ref[...]).astype(o_ref.dtype)


def _mla_attn(q, k, v):
    blk = lambda b, h: (b, h)
    return pl.pallas_call(
        _mla_attn_kernel,
        grid=(BATCH, MLA_HEADS),
        in_specs=[pl.BlockSpec((NB, MLA_QK), blk), pl.BlockSpec((NB, MLA_QK), blk), pl.BlockSpec((NB, MLA_V), blk)],
        out_specs=pl.BlockSpec((NB, MLA_V), blk),
        out_shape=jax.ShapeDtypeStruct((TOK, MLA_HEADS * MLA_V), BF16),
        compiler_params=_cparams(("parallel", "parallel")),
        name="mla_attn",
    )(q, k, v)


LRU_BW = LRU_WIDTH // LRU_BLOCKS
CTX_GROUPS = CTX_LEN // SUBLANES
LAT_GROUPS = SEQ // SUBLANES


def _scan_group(a, b, reverse):
    row = lax.broadcasted_iota(jnp.int32, a.shape, 0)
    for d in (1, 2, 4):
        shift = SUBLANES - d if reverse else d
        a_s = pltpu.roll(a, shift, axis=0)
        b_s = pltpu.roll(b, shift, axis=0)
        m = (row < SUBLANES - d) if reverse else (row >= d)
        b = jnp.where(m, a * b_s + b, b)
        a = jnp.where(m, a * a_s, a)
    return a, b


def _rglru_kernel(xr_ref, gr_ref, cw_ref, cb_ref, wa_ref, ba_ref, wx_ref, bx_ref, lam_ref, y_ref,
                  af_ref, bf_ref, ab_ref, bb_ref, hf_ref, hb_ref):
    row8 = lax.broadcasted_iota(jnp.int32, (SUBLANES, LRU_BW), 0)

    def taps(seg):
        n = seg.shape[0]

        def shifted(shift, keep, first):
            r = pltpu.roll(seg, shift % n, axis=0)
            if first:
                return jnp.concatenate([jnp.where(keep, r[:SUBLANES], 0.0), r[SUBLANES:]], axis=0)
            return jnp.concatenate([r[:-SUBLANES], jnp.where(keep, r[-SUBLANES:], 0.0)], axis=0)

        return shifted(2, row8 >= 2, True), shifted(1, row8 >= 1, True), shifted(-1, row8 < SUBLANES - 1, False)

    x = xr_ref[...]
    tc = taps(x[:CTX_LEN])
    tl = taps(x[CTX_LEN:])
    xm2, xm1, xp1 = [jnp.concatenate([a, b], axis=0) for a, b in zip(tc, tl)]
    u = cb_ref[...] + xm2 * cw_ref[0:1, :] + xm1 * cw_ref[1:2, :] + x * cw_ref[2:3, :] + xp1 * cw_ref[3:4, :]
    ub = u.astype(BF16)
    for d, (a_ref, b_ref) in enumerate(((af_ref, bf_ref), (ab_ref, bb_ref))):
        r = _sigmoid(_dot(ub, wa_ref[d, 0].astype(BF16)) + ba_ref[d:d + 1, :])
        ig = _sigmoid(_dot(ub, wx_ref[d, 0].astype(BF16)) + bx_ref[d:d + 1, :])
        z = -lam_ref[d:d + 1, :]
        softplus = jnp.maximum(z, 0.0) + jnp.log(1.0 + jnp.exp(-jnp.abs(z)))
        log_a = -LRU_C * r * softplus
        a = jnp.exp(log_a)
        a_ref[...] = a
        t = 1.0 - a * a
        b_ref[...] = jnp.where(t > 0.0, t * lax.rsqrt(t), 0.0) * (ig * u)

    def step(gf, gb, hf, hb):
        rf = pl.multiple_of(gf * SUBLANES, SUBLANES)
        a, b = _scan_group(af_ref[pl.ds(rf, SUBLANES), :], bf_ref[pl.ds(rf, SUBLANES), :], False)
        h = a * hf + b
        hf_ref[pl.ds(rf, SUBLANES), :] = h
        hf = jnp.broadcast_to(h[SUBLANES - 1:SUBLANES, :], h.shape)
        rb = pl.multiple_of(gb * SUBLANES, SUBLANES)
        a, b = _scan_group(ab_ref[pl.ds(rb, SUBLANES), :], bb_ref[pl.ds(rb, SUBLANES), :], True)
        h = a * hb + b
        hb_ref[pl.ds(rb, SUBLANES), :] = h
        hb = jnp.broadcast_to(h[0:1, :], h.shape)
        return hf, hb

    zero = jnp.zeros((SUBLANES, LRU_BW), F32)
    carry = lax.fori_loop(0, CTX_GROUPS, lambda i, c: step(i, CTX_GROUPS - 1 - i, *c), (zero, zero), unroll=4)
    lax.fori_loop(0, LAT_GROUPS, lambda i, c: step(CTX_GROUPS + i, CTX_GROUPS + LAT_GROUPS - 1 - i, *c), carry, unroll=4)
    y_ref[...] = ((hf_ref[...] + hb_ref[...]) * jax.nn.gelu(gr_ref[...])).astype(y_ref.dtype)


def _rglru(xr, gr, conv_w, conv_b, wa, ba, wx, bx, lam):
    blk = lambda b, n: (b, n)
    col = lambda b, n: (0, n)
    gate = lambda b, n: (0, n, 0, 0)
    seg = pltpu.VMEM((NB, LRU_BW), F32)
    return pl.pallas_call(
        _rglru_kernel,
        grid=(BATCH, LRU_BLOCKS),
        in_specs=[pl.BlockSpec((NB, LRU_BW), blk), pl.BlockSpec((NB, LRU_BW), blk),
                  pl.BlockSpec((4, LRU_BW), col), pl.BlockSpec((1, LRU_BW), col),
                  pl.BlockSpec((2, 1, LRU_BW, LRU_BW), gate), pl.BlockSpec((2, LRU_BW), col),
                  pl.BlockSpec((2, 1, LRU_BW, LRU_BW), gate), pl.BlockSpec((2, LRU_BW), col),
                  pl.BlockSpec((2, LRU_BW), col)],
        out_specs=pl.BlockSpec((NB, LRU_BW), blk),
        out_shape=jax.ShapeDtypeStruct((TOK, LRU_WIDTH), BF16),
        scratch_shapes=[seg, seg, seg, seg, seg, seg],
        compiler_params=_cparams(("parallel", "parallel")),
        name="rglru",
    )(xr, gr, conv_w, conv_b.reshape(1, LRU_WIDTH), wa, ba, wx, bx, lam)


def _gather_rows(ys_hbm, y_buf, sem, dest_ref, base, n_tok):
    def row_copy(k, r, d):
        return pltpu.make_async_copy(ys_hbm.at[pl.ds(d, 1)], y_buf.at[k, pl.ds(r, 1)], sem)

    def start(straight_line=False):
        def body(r, c):
            for k in range(2):
                row_copy(k, r, dest_ref[k * n_tok + base + r]).start(priority=k)
            return c

        if straight_line:
            for r in range(TM):
                body(r, 0)
        else:
            lax.fori_loop(0, TM, body, 0, unroll=8)

    def wait():
        def body(r, c):
            for k in range(2):
                row_copy(k, r, 0).wait()
            return c

        lax.fori_loop(0, TM, body, 0, unroll=8)

    return start, wait


L1_Q = WIN_HEADS * HEAD_DIM
L1_KV = WIN_KV_HEADS * HEAD_DIM
L1_NA = NA_HEADS * HEAD_DIM
L1_IN = L1_Q + 2 * L1_KV + 3 * L1_NA


def _in1_kernel(dest_ref, ys_hbm, x_ref, gate_ref, g2_ref, g_ref, sh_ref, sc_ref, w_ref, cos_ref, sin_ref,
                xo_ref, qw_ref, kw_ref, vw_ref, qn_ref, kn_ref, vn_ref, y_buf, sem):
    i = pl.program_id(0)
    last = pl.num_programs(0) - 1
    slot = i % 2
    rows = lambda tile, sl: _gather_rows(ys_hbm, y_buf.at[sl], sem.at[sl], dest_ref, tile * TM, TOK)

    @pl.when(i == 0)
    def _():
        rows(0, 0)[0]()

    rows(i, slot)[1]()
    rows(jnp.minimum(i + 1, last), 1 - slot)[0](straight_line=True)

    gate = gate_ref[...]
    x = x_ref[...] + g2_ref[0] * (gate[:, 4:5] * y_buf[slot, 0] + gate[:, 5:6] * y_buf[slot, 1])
    xo_ref[...] = x
    h = _rms(x, g_ref[...]) * (1.0 + sc_ref[0]) + sh_ref[0]
    y = _dot(h.astype(BF16), w_ref[...])
    cos = cos_ref[...]
    sin = sin_ref[...]
    scale = LOG2E * HEAD_DIM ** -0.5
    for hd in range(WIN_HEADS):
        lo = hd * HEAD_DIM
        qw_ref[:, lo:lo + HEAD_DIM] = _rope(y[:, lo:lo + HEAD_DIM] * scale, cos, sin, HEAD_DIM // 4).astype(BF16)
    for hd in range(WIN_KV_HEADS):
        lo = hd * HEAD_DIM
        kw_ref[:, lo:lo + HEAD_DIM] = _rope(y[:, L1_Q + lo:L1_Q + lo + HEAD_DIM], cos, sin, HEAD_DIM // 4).astype(BF16)
    o = L1_Q + L1_KV
    vw_ref[...] = y[:, o:o + L1_KV].astype(BF16)
    o += L1_KV
    qn_ref[...] = (y[:, o:o + L1_NA] * scale).astype(BF16)
    kn_ref[...] = y[:, o + L1_NA:o + 2 * L1_NA].astype(BF16)
    vn_ref[...] = y[:, o + 2 * L1_NA:].astype(BF16)

    @pl.when(i == last)
    def _():
        rows(last, 1 - slot)[1]()


def _in1(dest, ys, x, gates, mod_prev, g, mod, w, cos, sin):
    row = lambda i, d: (i, 0)
    pos = lambda i, d: (i % TPB, 0)
    widths = (L1_Q, L1_KV, L1_KV, L1_NA, L1_NA, L1_NA)
    return pl.pallas_call(
        _in1_kernel,
        grid_spec=pltpu.PrefetchScalarGridSpec(
            num_scalar_prefetch=1,
            grid=(TOK // TM,),
            in_specs=[pl.BlockSpec(memory_space=pl.ANY), pl.BlockSpec((TM, D_MODEL), row),
                      pl.BlockSpec((TM, SUBLANES), row), _mod_spec(_comb_tile, 5), _resident((1, D_MODEL)),
                      _mod_spec(_comb_tile, 0), _mod_spec(_comb_tile, 1), _resident((D_MODEL, L1_IN)),
                      pl.BlockSpec((TM, LANES), pos), pl.BlockSpec((TM, LANES), pos)],
            out_specs=[pl.BlockSpec((TM, D_MODEL), row)] + [pl.BlockSpec((TM, n), row) for n in widths],
            scratch_shapes=[pltpu.VMEM((2, 2, TM, D_MODEL), F32), pltpu.SemaphoreType.DMA((2,))]),
        out_shape=[jax.ShapeDtypeStruct((TOK, D_MODEL), F32)] + [jax.ShapeDtypeStruct((TOK, n), BF16) for n in widths],
        compiler_params=_cparams(("arbitrary",)),
        name="l1_in_proj",
    )(dest, ys, x, gates, mod_prev, g, mod, mod, w, cos, sin)


WIN_TQ = 128
WIN_SPAN = WIN_TQ + 2 * WINDOW
WIN_G = WIN_HEADS // WIN_KV_HEADS


def _win_key_start(n):
    return min(max((n - 1) * WIN_TQ, 0), SEQ - WIN_SPAN)


WIN_OFFSETS = sorted({n * WIN_TQ - _win_key_start(n) for n in range(SEQ // WIN_TQ)})


def _win_mask_bias():
    qoff = jnp.arange(WIN_G * WIN_TQ)[:, None] % WIN_TQ
    koff = jnp.arange(WIN_SPAN)[None, :]
    return jnp.stack([jnp.where(jnp.abs(qoff - koff + d) <= WINDOW, 0.0, NEG_INF) for d in WIN_OFFSETS]).astype(F32)


def _win_kernel(sink_ref, q_ref, k_ref, v_ref, mask_ref, o_ref):
    hk = pl.program_id(1)
    o_ref[0:CTX_LEN, :] = jnp.zeros((CTX_LEN, WIN_G * HEAD_DIM), o_ref.dtype)
    rows = WIN_G * WIN_TQ
    head = lax.broadcasted_iota(jnp.int32, (rows, 1), 0) // WIN_TQ
    sink = jnp.zeros((rows, 1), F32)
    for g in range(WIN_G):
        sink = jnp.where(head == g, sink_ref[hk * WIN_G + g] * LOG2E, sink)

    for n in range(SEQ // WIN_TQ):
        r0 = CTX_LEN + n * WIN_TQ
        start = _win_key_start(n)
        ks = CTX_LEN + start
        q4 = q_ref[r0:r0 + WIN_TQ, :]
        q = jnp.concatenate([q4[:, g * HEAD_DIM:(g + 1) * HEAD_DIM] for g in range(WIN_G)], axis=0)
        s_c = _dot_nt(q, k_ref[0:CTX_LEN, :])
        s_w = _dot_nt(q, k_ref[ks:ks + WIN_SPAN, :]) + mask_ref[WIN_OFFSETS.index(n * WIN_TQ - start)]
        m = jnp.maximum(jnp.maximum(jnp.max(s_c, axis=-1, keepdims=True), jnp.max(s_w, axis=-1, keepdims=True)), sink)
        p_c = jnp.exp2(s_c - m)
        p_w = jnp.exp2(s_w - m)
        l = jnp.sum(p_c, axis=-1, keepdims=True) + jnp.sum(p_w, axis=-1, keepdims=True) + jnp.exp2(sink - m)
        o = (_dot(p_w.astype(BF16), v_ref[ks:ks + WIN_SPAN, :]) + _dot(p_c.astype(BF16), v_ref[0:CTX_LEN, :])) / l
        for g in range(WIN_G):
            o_ref[r0:r0 + WIN_TQ, g * HEAD_DIM:(g + 1) * HEAD_DIM] = o[g * WIN_TQ:(g + 1) * WIN_TQ].astype(o_ref.dtype)


def _win_attn(sink, q, k, v):
    blk = lambda b, h, *_: (b, h)
    mask = _win_mask_bias()
    return pl.pallas_call(
        _win_kernel,
        grid_spec=pltpu.PrefetchScalarGridSpec(
            num_scalar_prefetch=1,
            grid=(BATCH, WIN_KV_HEADS),
            in_specs=[pl.BlockSpec((NB, WIN_G * HEAD_DIM), blk), pl.BlockSpec((NB, HEAD_DIM), blk),
                      pl.BlockSpec((NB, HEAD_DIM), blk), _resident(mask.shape)],
            out_specs=pl.BlockSpec((NB, WIN_G * HEAD_DIM), blk)),
        out_shape=jax.ShapeDtypeStruct((TOK, L1_Q), BF16),
        compiler_params=_cparams(("parallel", "parallel")),
        name="window_attn",
    )(sink, q, k, v, mask)


NA_GRID_ROWS = SEQ // GRID_W
NA_BAND = NA_ROWS * GRID_W
NA_RPI = 4


def _na_kernel(q_ref, k_ref, v_ref, bias_ref, o_ref):
    o_ref[0:CTX_LEN, :] = jnp.zeros((CTX_LEN, HEAD_DIM), o_ref.dtype)

    for i in range(NA_GRID_ROWS // NA_RPI):
        qs = CTX_LEN + i * NA_RPI * GRID_W
        q = q_ref[qs:qs + NA_RPI * GRID_W, :]
        s_c = _dot_nt(q, k_ref[0:CTX_LEN, :])
        starts = []
        s_w = []
        for j in range(NA_RPI):
            r = i * NA_RPI + j
            r0 = min(max(r - NA_ROWS // 2, 0), NA_GRID_ROWS - NA_ROWS)
            ks = CTX_LEN + r0 * GRID_W
            starts.append(ks)
            s_w.append(_dot_nt(q[j * GRID_W:(j + 1) * GRID_W], k_ref[ks:ks + NA_BAND, :])
                       + jnp.concatenate([bias_ref[0, r0 - r + NA_ROWS - 1 + 2 * p] for p in range(NA_ROWS // 2)],
                                         axis=1))
        s_w = jnp.concatenate(s_w, axis=0)
        m = jnp.maximum(jnp.max(s_c, axis=-1, keepdims=True), jnp.max(s_w, axis=-1, keepdims=True))
        p_c = jnp.exp2(s_c - m)
        p_w = jnp.exp2(s_w - m)
        l = jnp.sum(p_c, axis=-1, keepdims=True) + jnp.sum(p_w, axis=-1, keepdims=True)
        p_w = p_w.astype(BF16)
        o_w = jnp.concatenate([_dot(p_w[j * GRID_W:(j + 1) * GRID_W], v_ref[starts[j]:starts[j] + NA_BAND, :])
                               for j in range(NA_RPI)], axis=0)
        o = (o_w + _dot(p_c.astype(BF16), v_ref[0:CTX_LEN, :])) / l
        o_ref[qs:qs + NA_RPI * GRID_W, :] = o.astype(o_ref.dtype)


def _na_attn(q, k, v, bias):
    blk = lambda b, h: (b, h)
    return pl.pallas_call(
        _na_kernel,
        grid=(BATCH, NA_HEADS),
        in_specs=[pl.BlockSpec((NB, HEAD_DIM), blk), pl.BlockSpec((NB, HEAD_DIM), blk),
                  pl.BlockSpec((NB, HEAD_DIM), blk),
                  pl.BlockSpec((1, 2 * NA_ROWS - 2, GRID_W, 2 * GRID_W), lambda b, h: (h, 0, 0, 0))],
        out_specs=pl.BlockSpec((NB, HEAD_DIM), blk),
        out_shape=jax.ShapeDtypeStruct((TOK, L1_NA), BF16),
        compiler_params=_cparams(("parallel", "parallel")),
        name="na_attn",
    )(q, k, v, bias)


def _na_bias_table(rpb):
    cols = jnp.arange(GRID_W)
    c0 = jnp.clip(cols - NA_COLS // 2, 0, GRID_W - NA_COLS)
    kc = cols[None, :]
    valid = (kc >= c0[:, None]) & (kc < c0[:, None] + NA_COLS)
    pick = (jnp.arange(2 * NA_COLS - 1)[:, None, None] == (kc - cols[:, None] + NA_COLS - 1)[None]).astype(F32)
    tbl = jnp.einsum("hdo,ock->hdck", rpb.astype(F32) * LOG2E, pick, precision=lax.Precision.HIGHEST)
    tbl = jnp.where(valid[None, None], tbl, NEG_INF)
    return jnp.concatenate([tbl[:, :-1], tbl[:, 1:]], axis=-1)


def _router_logits(h, rwh_ref, rwl_ref):
    hh = h.astype(BF16)
    hl = (h - hh.astype(F32)).astype(BF16)
    return _dot_nt(rwh_ref[...], hh) + (_dot_nt(rwh_ref[...], hl) + _dot_nt(rwl_ref[...], hh))


def _router(logits, rb_ref, tri_ref, carry):
    tm = logits.shape[1]
    scores = jax.nn.sigmoid(logits)
    biased = scores + rb_ref[...]
    nj = EXPERTS_PER_GROUP
    s = [biased[j * N_GROUPS:(j + 1) * N_GROUPS] for j in range(nj)]
    u = [scores[j * N_GROUPS:(j + 1) * N_GROUPS] for j in range(nj)]
    gs = None
    for a in range(nj):
        for b in range(a + 1, nj):
            pair = s[a] + s[b]
            gs = pair if gs is None else jnp.maximum(gs, pair)
    giota = lax.broadcasted_iota(jnp.int32, (N_GROUPS, tm), 0).astype(F32)
    gmax = jnp.max(gs, axis=0, keepdims=True)
    gidx = jnp.min(jnp.where(gs == gmax, giota, float(N_GROUPS)), axis=0, keepdims=True)
    gm = giota == gidx
    v = [jnp.sum(jnp.where(gm, s[j], 0.0), axis=0, keepdims=True) for j in range(nj)]
    w = [jnp.sum(jnp.where(gm, u[j], 0.0), axis=0, keepdims=True) for j in range(nj)]
    sel = []
    for j in range(nj):
        beaten = jnp.zeros((1, tm), F32)
        for i in range(nj):
            if i != j:
                ahead = (v[i] >= v[j]) if i < j else (v[i] > v[j])
                beaten = beaten + jnp.where(ahead, 1.0, 0.0)
        sel.append(beaten < 2.0)
    wsum = sum(jnp.where(sel[j], w[j], 0.0) for j in range(nj))
    first = functools.reduce(jnp.minimum, [jnp.where(sel[j], float(j), float(nj)) for j in range(nj)])
    last = functools.reduce(jnp.maximum, [jnp.where(sel[j], float(j), -1.0) for j in range(nj)])
    gmf = jnp.where(gm, 1.0, 0.0)
    cnt = jnp.concatenate([jnp.where(sel[j], gmf, 0.0) for j in range(nj)], axis=0)
    pos = _dot(cnt.astype(BF16), tri_ref[...]) + carry
    rank = [jnp.sum(cnt[j * N_GROUPS:(j + 1) * N_GROUPS] * pos[j * N_GROUPS:(j + 1) * N_GROUPS], axis=0, keepdims=True)
            for j in range(nj)]
    pick = lambda which, vals: sum(jnp.where(which == float(j), vals[j], 0.0) for j in range(nj))
    gate = [w[j] / wsum for j in range(nj)]
    zero = jnp.zeros((1, tm), F32)
    info = jnp.concatenate(
        [gidx * nj + first, gidx * nj + last, pick(first, rank), pick(last, rank), pick(first, gate), pick(last, gate),
         zero, zero], axis=0)
    return info, carry + jnp.sum(cnt, axis=1, keepdims=True)


OUT_SUB = 128


def _out_kernel(ya_ref, yb_ref, *refs, n_tiles, split_x):
    i = pl.program_id(0)
    if split_x:
        xc_ref, xl_ref, *refs = refs
        x_in = lambda: _pick_tile(jnp.minimum(i, n_tiles - 1), xc_ref, xl_ref)
    else:
        x_ref, *refs = refs
        x_in = lambda: x_ref[...]
    (w_ref, g1_ref, n_ref, sh_ref, sc_ref, rwh_ref, rwl_ref, rb_ref, tri_ref,
     xo_ref, h_ref, info_ref, counts_ref, carry_ref, hbuf_ref) = refs

    @pl.when(i == 0)
    def _():
        carry_ref[...] = jnp.zeros_like(carry_ref)
        hbuf_ref[1] = jnp.zeros((TM, D_MODEL), F32)

    subs = range(0, TM, OUT_SUB)
    logits = [_router_logits(hbuf_ref[(i + 1) % 2, r0:r0 + OUT_SUB, :], rwh_ref, rwl_ref) for r0 in subs]
    half = w_ref.shape[0] // 2
    y = _dot(ya_ref[...], w_ref[0:half, :]) + _dot(yb_ref[...], w_ref[half:, :])
    old = carry_ref[:, 0:1]
    carry = old
    for r0, lg in zip(subs, logits):
        info_ref[:, r0:r0 + OUT_SUB], carry = _router(lg, rb_ref, tri_ref, carry)
    counts = jnp.broadcast_to(jnp.where(i > 0, carry, old), carry_ref.shape)
    carry_ref[...] = counts
    counts_ref[...] = counts

    x = x_in() + g1_ref[0] * y
    xo_ref[...] = x
    h = _rms(x, n_ref[...]) * (1.0 + sc_ref[0]) + sh_ref[0]
    h_ref[...] = h
    hbuf_ref[i % 2] = h


def _out_proj(ya, yb, x, w_out, mod, norm, rwh, rwl, rb, tri, tile_fn, n_tiles):
    split_x = isinstance(x, tuple)
    half = w_out.shape[0] // 2
    this = lambda i: jnp.minimum(i, n_tiles - 1)
    tile = lambda i: tile_fn(this(i))
    src = lambda i: (tile(i)[0], 0)
    dst = lambda i: (this(i), 0)
    n_tok = n_tiles * TM
    x_specs = [_ctx_spec(this), _lat_spec(this)] if split_x else [pl.BlockSpec((TM, D_MODEL), src)]
    return pl.pallas_call(
        functools.partial(_out_kernel, n_tiles=n_tiles, split_x=split_x),
        grid=(n_tiles + 1,),
        in_specs=[pl.BlockSpec((TM, half), src), pl.BlockSpec((TM, half), src), *x_specs,
                  _resident(w_out.shape),
                  _mod_spec(tile, 2), _resident((1, D_MODEL)), _mod_spec(tile, 3), _mod_spec(tile, 4),
                  _resident((N_EXPERTS, D_MODEL)), _resident((N_EXPERTS, D_MODEL)), _resident((N_EXPERTS, 1)),
                  _resident((OUT_SUB, OUT_SUB))],
        out_specs=[pl.BlockSpec((TM, D_MODEL), dst), pl.BlockSpec((TM, D_MODEL), dst),
                   pl.BlockSpec((SUBLANES, TM), lambda i: (0, jnp.maximum(i - 1, 0))),
                   pl.BlockSpec((N_EXPERTS, LANES), lambda i: (0, 0))],
        out_shape=[jax.ShapeDtypeStruct((n_tok, D_MODEL), F32), jax.ShapeDtypeStruct((n_tok, D_MODEL), F32),
                   jax.ShapeDtypeStruct((SUBLANES, n_tok), F32), jax.ShapeDtypeStruct((N_EXPERTS, LANES), F32)],
        scratch_shapes=[pltpu.VMEM((N_EXPERTS, LANES), F32), pltpu.VMEM((2, TM, D_MODEL), F32)],
        compiler_params=_cparams(("arbitrary",)),
        name="out_proj_router",
    )(ya, yb, *(x if split_x else (x,)), w_out, mod, norm, mod, mod, rwh, rwl, rb, tri)


def _lookup(table, idx):
    onehot = idx[..., None] == jnp.arange(table.shape[0], dtype=jnp.int32)
    return jnp.sum(jnp.where(onehot, table, 0), axis=-1)


def _dispatch_plan(info, counts, n_tok):
    n_steps = 2 * n_tok // TMX + N_EXPERTS - 1
    experts = jnp.arange(N_EXPERTS, dtype=jnp.int32)
    cnt = counts[:, 0].astype(jnp.int32).reshape(EXPERTS_PER_GROUP, N_GROUPS).T.reshape(N_EXPERTS)
    end = jnp.cumsum(cnt)
    off = end - cnt
    first_tile = off // TMX
    visits = jnp.where(cnt > 0, (end - 1) // TMX - first_tile + 1, 0)
    visit_end = jnp.cumsum(visits)
    n_valid = visit_end[-1]
    dest = _lookup(off, info[0:2].astype(jnp.int32)) + info[2:4].astype(jnp.int32)
    step = jnp.minimum(jnp.arange(n_steps, dtype=jnp.int32), n_valid - 1)
    e = jnp.sum(visit_end[None, :] <= step[:, None], axis=1).astype(jnp.int32)
    k = step - _lookup(visit_end - visits, e)
    tile = _lookup(first_tile, e) + k
    lo = jnp.clip(_lookup(off, e) - tile * TMX, 0, TMX)
    hi = jnp.clip(_lookup(end, e) - tile * TMX, 0, TMX)
    later = (experts[None, :] > experts[:, None]) & (cnt[None, :] > 0)
    nxt = jnp.min(jnp.where(later, experts[None, :], N_EXPERTS), axis=1)
    nxt = jnp.where(nxt == N_EXPERTS, -1, nxt)
    slot = (jnp.cumsum((cnt > 0).astype(jnp.int32)) - 1) % 2
    i32 = lambda v: v.astype(jnp.int32)
    return (dest.reshape(-1), i32(tile), i32(e), i32(n_valid.reshape(1)), i32(lo), i32(hi), i32(k == 0),
            i32(_lookup(nxt, e)), i32(_lookup(slot, e)))


def _dispatch_kernel(dest_ref, h_ref, xs_hbm, sem, *, n_tok):
    i = pl.program_id(0)

    def row_copy(r, d):
        return pltpu.make_async_copy(h_ref.at[pl.ds(r, 1)], xs_hbm.at[pl.ds(d, 1)], sem)

    def start(r, c):
        for k in range(2):
            row_copy(r, dest_ref[k * n_tok + i * TM + r]).start(priority=k)
        return c

    lax.fori_loop(0, TM, start, 0, unroll=8)

    def wait(r, c):
        for k in range(2):
            row_copy(r, 0).wait()
        return c

    lax.fori_loop(0, TM, wait, 0, unroll=8)


def _dispatch(dest, h):
    n_tok = h.shape[0]
    return pl.pallas_call(
        functools.partial(_dispatch_kernel, n_tok=n_tok),
        grid_spec=pltpu.PrefetchScalarGridSpec(
            num_scalar_prefetch=1,
            grid=(n_tok // TM,),
            in_specs=[pl.BlockSpec((TM, D_MODEL), lambda i, d: (i, 0))],
            out_specs=pl.BlockSpec(memory_space=pl.ANY),
            scratch_shapes=[pltpu.SemaphoreType.DMA(())]),
        out_shape=jax.ShapeDtypeStruct((2 * n_tok, D_MODEL), F32),
        compiler_params=_cparams(("arbitrary",)),
        name="moe_dispatch",
    )(dest, h)


def _moe_kernel(tile_ref, te_ref, nv_ref, lo_ref, hi_ref, first_ref, nxt_ref, slot_ref,
                xs_ref, wg_hbm, wu_hbm, wd_hbm, ys_ref, wg_f, wu_f, wd_f, wg_s, wu_s, wd_s, sem):
    s = pl.program_id(0)

    def fetch(e, slot):
        return [pltpu.make_async_copy(src.at[e], dst.at[slot], sem.at[slot])
                for src, dst in ((wg_hbm, wg_f), (wu_hbm, wu_f), (wd_hbm, wd_f))]

    @pl.when(s < nv_ref[0])
    def _():
        @pl.when(first_ref[s] == 1)
        def _():
            slot = slot_ref[s]

            @pl.when(s == 0)
            def _():
                for c in fetch(te_ref[0], slot):
                    c.start()

            for c in fetch(te_ref[s], slot):
                c.wait()

            @pl.when(nxt_ref[s] >= 0)
            def _():
                for c in fetch(nxt_ref[s], 1 - slot):
                    c.start()

            wg_s[...] = wg_f[slot].astype(BF16)
            wu_s[...] = wu_f[slot].astype(BF16)
            wd_s[...] = wd_f[slot].astype(BF16)

        lo = lo_ref[s]
        hi = hi_ref[s]

        @pl.when(lo == 0)
        def _():
            ys_ref[...] = jnp.zeros_like(ys_ref)

        def visit(r0, n):
            x = xs_ref[r0:r0 + n, :].astype(BF16)
            hg = _dot(x, wg_s[...])
            he = (hg * jax.nn.sigmoid(hg)) * _dot(x, wu_s[...])
            y = _dot(he.astype(BF16), wd_s[...])
            row = r0 + lax.broadcasted_iota(jnp.int32, (n, 1), 0)
            ys_ref[r0:r0 + n, :] = jnp.where((row >= lo) & (row < hi), y, ys_ref[r0:r0 + n, :])

        half = TMX // 2
        lower = hi <= half
        upper = lo >= half
        pl.when(lower)(lambda: visit(0, half))
        pl.when(upper)(lambda: visit(half, half))
        pl.when(jnp.logical_not(jnp.logical_or(lower, upper)))(lambda: visit(0, TMX))


def _moe(plan, xs, w_gate, w_up, w_down):
    n_steps = plan[0].shape[0]
    tile = lambda s, t, *_: (t[s], 0)
    any_spec = pl.BlockSpec(memory_space=pl.ANY)
    return pl.pallas_call(
        _moe_kernel,
        grid_spec=pltpu.PrefetchScalarGridSpec(
            num_scalar_prefetch=len(plan),
            grid=(n_steps,),
            in_specs=[pl.BlockSpec((TMX, D_MODEL), tile), any_spec, any_spec, any_spec],
            out_specs=pl.BlockSpec((TMX, D_MODEL), tile),
            scratch_shapes=[pltpu.VMEM((2, D_MODEL, EXPERT_FF), F32), pltpu.VMEM((2, D_MODEL, EXPERT_FF), F32),
                            pltpu.VMEM((2, EXPERT_FF, D_MODEL), F32),
                            pltpu.VMEM((D_MODEL, EXPERT_FF), BF16), pltpu.VMEM((D_MODEL, EXPERT_FF), BF16),
                            pltpu.VMEM((EXPERT_FF, D_MODEL), BF16), pltpu.SemaphoreType.DMA((2,))]),
        out_shape=jax.ShapeDtypeStruct(xs.shape, F32),
        compiler_params=_cparams(("arbitrary",)),
        name="moe_experts",
    )(*plan, xs, w_gate, w_up, w_down)


def _combine_kernel(dest_ref, ys_hbm, x_ref, gate_ref, g2_ref, n_ref, o_ref, y_buf, sem, *, n_tok):
    start, wait = _gather_rows(ys_hbm, y_buf, sem, dest_ref, pl.program_id(0) * TM, n_tok)
    start()
    wait()
    gate = gate_ref[...]
    x = x_ref[...] + g2_ref[0] * (gate[:, 4:5] * y_buf[0] + gate[:, 5:6] * y_buf[1])
    o_ref[...] = _rms(x, n_ref[...])


def _combine(dest, ys, x, gates, mod, norm, tile_fn, n_tiles):
    n_tok = n_tiles * TM
    row = lambda i, d: (i, 0)
    return pl.pallas_call(
        functools.partial(_combine_kernel, n_tok=n_tok),
        grid_spec=pltpu.PrefetchScalarGridSpec(
            num_scalar_prefetch=1,
            grid=(n_tiles,),
            in_specs=[pl.BlockSpec(memory_space=pl.ANY), pl.BlockSpec((TM, D_MODEL), row),
                      pl.BlockSpec((TM, SUBLANES), row), _mod_spec(tile_fn, 5),
                      pl.BlockSpec((1, D_MODEL), lambda i, d: (0, 0))],
            out_specs=pl.BlockSpec((TM, D_MODEL), row),
            scratch_shapes=[pltpu.VMEM((2, TM, D_MODEL), F32), pltpu.SemaphoreType.DMA(())]),
        out_shape=jax.ShapeDtypeStruct((n_tok, D_MODEL), F32),
        compiler_params=_cparams(("arbitrary",)),
        name="moe_combine",
    )(dest, ys, x, gates, mod, norm)


def _experts(h, info, counts, w_gate, w_up, w_down):
    dest, *plan = _dispatch_plan(info, counts, h.shape[0])
    return dest, _moe(plan, _dispatch(dest, h), w_gate, w_up, w_down)


def _rope_tables(dim):
    half = dim // 2
    inv_freq = ROPE_THETA ** (-jnp.arange(0, half, 2, dtype=F32) / half)
    ang_r = jnp.arange(SEQ // GRID_W, dtype=F32)[:, None] * inv_freq
    ang_c = jnp.arange(GRID_W, dtype=F32)[:, None] * inv_freq
    by_row = lambda v: jnp.repeat(v, GRID_W, axis=0)
    by_col = lambda v: jnp.tile(v, (SEQ // GRID_W, 1))
    cos_r, sin_r, cos_c, sin_c = by_row(jnp.cos(ang_r)), by_row(jnp.sin(ang_r)), by_col(jnp.cos(ang_c)), by_col(jnp.sin(ang_c))
    cos = jnp.concatenate([cos_r, cos_r, cos_c, cos_c], axis=-1)
    sin = jnp.concatenate([-sin_r, sin_r, -sin_c, sin_c], axis=-1)
    pad = LANES - dim
    cos = jnp.pad(cos, ((0, 0), (0, pad)))
    sin = jnp.pad(sin, ((0, 0), (0, pad)))
    ctx_cos = jnp.pad(jnp.ones((CTX_LEN, dim), F32), ((0, 0), (0, pad)))
    return jnp.concatenate([ctx_cos, cos], axis=0), jnp.concatenate([jnp.zeros((CTX_LEN, LANES), F32), sin], axis=0)


def kernel(x, c, ctx, c_ctx, router_w, router_b, final_norm, l0_mod_w, l0_mod_b, l0_norm_mix, l0_norm_ffn, l0_w_in, l0_q_norm, l0_w_uq, l0_kv_norm, l0_w_ukv, l0_conv_w, l0_conv_b, l0_gate_a_w, l0_gate_a_b, l0_gate_x_w, l0_gate_x_b, l0_lru_lambda, l0_w_out, l0_exp_gate, l0_exp_up, l0_exp_down, l1_mod_w, l1_mod_b, l1_norm_mix, l1_norm_ffn, l1_w_in, l1_sink, l1_rpb, l1_w_out, l1_exp_gate, l1_exp_up, l1_exp_down):
    row = lambda v: v.reshape(1, -1)
    x_in = (ctx.reshape(BATCH * CTX_LEN, D_MODEL), x.reshape(BATCH * SEQ, D_MODEL))

    cs = jnp.concatenate([c, c_ctx[None], jnp.zeros((SUBLANES - BATCH - 1, D_MODEL), F32)], axis=0)
    mod0 = _modulation(cs, l0_mod_w, l0_mod_b)
    mod1 = _modulation(cs, l1_mod_w, l1_mod_b)

    perm = jnp.arange(N_EXPERTS).reshape(N_GROUPS, EXPERTS_PER_GROUP).T.reshape(-1)
    rwt = router_w.T[perm]
    rwh = rwt.astype(BF16)
    rwl = (rwt - rwh.astype(F32)).astype(BF16)
    rb = router_b[perm].reshape(N_EXPERTS, 1).astype(F32)
    tri = jnp.triu(jnp.ones((OUT_SUB, OUT_SUB), F32), k=1).astype(BF16)
    router = (rwh, rwl, rb, tri)

    cqkv, xr, gr = _in0(*x_in, row(l0_norm_mix), mod0, l0_w_in.astype(BF16))
    wq = jnp.pad(l0_w_uq.reshape(MLA_Q_RANK, MLA_HEADS, MLA_NOPE + MLA_ROPE),
                 ((0, 0), (0, 0), (0, MLA_QK - MLA_NOPE - MLA_ROPE))).reshape(MLA_Q_RANK, MLA_HEADS * MLA_QK)
    wkv = l0_w_ukv.reshape(MLA_KV_RANK, MLA_HEADS, MLA_NOPE + MLA_V)
    wk = wkv[:, :, :MLA_NOPE].reshape(MLA_KV_RANK, MLA_HEADS * MLA_NOPE)
    wv = wkv[:, :, MLA_NOPE:].reshape(MLA_KV_RANK, MLA_HEADS * MLA_V)
    cos0, sin0 = _rope_tables(MLA_ROPE)
    q, k, v = _mla_proj(cqkv, row(l0_q_norm), row(l0_kv_norm), wq.astype(BF16), wk.astype(BF16), wv.astype(BF16),
                        cos0, sin0)
    att = _mla_attn(q, k, v)
    rnn = _rglru(xr, gr, l0_conv_w, l0_conv_b, l0_gate_a_w, l0_gate_a_b, l0_gate_x_w, l0_gate_x_b, l0_lru_lambda)
    n0 = TOK // TM
    xs, h, info, counts = _out_proj(att, rnn, x_in, l0_w_out.astype(BF16), mod0, row(l0_norm_ffn), *router,
                                    _comb_tile, n0)
    dest, ys = _experts(h, info, counts, l0_exp_gate, l0_exp_up, l0_exp_down)

    cos1, sin1 = _rope_tables(HEAD_DIM)
    xs, qw, kw, vw, qn, kn, vn = _in1(dest, ys, xs, info.T, mod0, row(l1_norm_mix), mod1, l1_w_in.astype(BF16),
                                      cos1, sin1)
    win = _win_attn(l1_sink.astype(F32), qw, kw, vw)
    na = _na_attn(qn, kn, vn, _na_bias_table(l1_rpb))
    n1 = BATCH * LAT_TPB
    xl, h, info, counts = _out_proj(win, na, xs, l1_w_out.astype(BF16), mod1, row(l1_norm_ffn), *router,
                                    _lat_tile, n1)
    dest, ys = _experts(h, info, counts, l1_exp_gate, l1_exp_up, l1_exp_down)
    out = _combine(dest, ys, xl, info.T, mod1, row(final_norm), lambda i: (i, i // LAT_TPB), n1)
    return out.reshape(BATCH, SEQ, D_MODEL)
```

```python
import functools

import jax
import jax.numpy as jnp
from jax import lax
from jax.experimental import pallas as pl
from jax.experimental.pallas import tpu as pltpu

F32 = jnp.float32
BF16 = jnp.bfloat16

D_MODEL = 2048
BATCH = 4
SEQ = 2048
GRID_W = 64
CTX_LEN = 256
EPS = 1e-6
NEG_INF = -1e30
ROPE_THETA = 10000.0
N_MOD = 6

MLA_HEADS = 8
MLA_Q_RANK = 512
MLA_KV_RANK = 256
MLA_NOPE = 128
MLA_ROPE = 64
MLA_V = 128

LRU_WIDTH = 1024
LRU_BLOCKS = 8
LRU_C = 8.0

HEAD_DIM = 128
WIN_HEADS = 8
WIN_KV_HEADS = 2
WINDOW = 128
NA_HEADS = 8
NA_ROWS = 8
NA_COLS = 16

N_EXPERTS = 32
N_GROUPS = 8
EXPERTS_PER_GROUP = 4
EXPERT_FF = 512

LANES = 128
SUBLANES = 8
VMEM_LIMIT = 56 * 1024 * 1024

NB = CTX_LEN + SEQ
TOK = BATCH * NB
TM = 256
TPB = NB // TM
LAT_TPB = SEQ // TM
TMX = 256
MLA_QK = 2 * LANES
LOG2E = 1.4426950408889634


def _cparams(sem):
    return pltpu.CompilerParams(dimension_semantics=sem, vmem_limit_bytes=VMEM_LIMIT)


def _resident(shape):
    nd = len(shape)
    return pl.BlockSpec(shape, lambda *_: (0,) * nd, pipeline_mode=pl.Buffered(1))


def _rms(x, g):
    return x * lax.rsqrt(jnp.mean(x * x, axis=-1, keepdims=True) + EPS) * g


def _sigmoid(x):
    return 0.5 * jnp.tanh(0.5 * x) + 0.5


def _dot(a, b):
    return jnp.dot(a, b, preferred_element_type=F32)


def _dot_nt(a, b):
    return lax.dot_general(a, b, (((1,), (1,)), ((), ())), preferred_element_type=F32)


def _swap_blocks(x, blk):
    lane = lax.broadcasted_iota(jnp.int32, x.shape, 1)
    nxt = pltpu.roll(x, LANES - blk, axis=1)
    prv = pltpu.roll(x, blk, axis=1)
    return jnp.where((lane % (2 * blk)) < blk, nxt, prv)


def _rope(x, cos, sin, blk):
    return x * cos + _swap_blocks(x, blk) * sin


def _comb_tile(i):
    b = i // TPB
    return i, jnp.where(i % TPB == 0, BATCH, b)


def _lat_tile(i):
    b = i // LAT_TPB
    return b * TPB + 1 + i % LAT_TPB, b


def _ctx_spec(tile_fn):
    return pl.BlockSpec((TM, D_MODEL), lambda i, *_: (tile_fn(i) // TPB, 0))


def _lat_spec(tile_fn):
    return pl.BlockSpec((TM, D_MODEL),
                        lambda i, *_: (tile_fn(i) // TPB * LAT_TPB + jnp.maximum(tile_fn(i) % TPB - 1, 0), 0))


def _pick_tile(tile, ctx_ref, lat_ref):
    return jnp.where(tile % TPB == 0, ctx_ref[...], lat_ref[...])


def _mod_spec(tile_fn, k):
    return pl.BlockSpec((1, 1, D_MODEL), lambda i, *_: (tile_fn(i)[1] * N_MOD + k, 0, 0))


def _mod_kernel(c_ref, w_ref, b_ref, o_ref):
    c = c_ref[...]
    a = (c * jax.nn.sigmoid(c)).astype(BF16)
    o_ref[...] = _dot(a, w_ref[...].astype(BF16)) + b_ref[...]


def _modulation(cs, w, b):
    n = N_MOD * D_MODEL
    tn = 1024
    out = pl.pallas_call(
        _mod_kernel,
        grid=(n // tn,),
        in_specs=[pl.BlockSpec((SUBLANES, D_MODEL), lambda j: (0, 0)),
                  pl.BlockSpec((D_MODEL, tn), lambda j: (0, j)),
                  pl.BlockSpec((1, tn), lambda j: (0, j))],
        out_specs=pl.BlockSpec((SUBLANES, tn), lambda j: (0, j)),
        out_shape=jax.ShapeDtypeStruct((SUBLANES, n), F32),
        compiler_params=_cparams(("arbitrary",)),
        name="modulation",
    )(cs, w, b.reshape(1, n))
    return out.reshape(SUBLANES * N_MOD, 1, D_MODEL)


L0_CQKV = MLA_Q_RANK + MLA_KV_RANK + LANES
L0_IN = MLA_Q_RANK + MLA_KV_RANK + MLA_ROPE + 2 * LRU_WIDTH


def _in0_kernel(xc_ref, xl_ref, g_ref, sh_ref, sc_ref, w_ref, cqkv_ref, xr_ref, gr_ref):
    x = _pick_tile(pl.program_id(0), xc_ref, xl_ref)
    h = _rms(x, g_ref[...]) * (1.0 + sc_ref[0]) + sh_ref[0]
    y = _dot(h.astype(BF16), w_ref[...])
    lat = MLA_Q_RANK + MLA_KV_RANK
    cqkv_ref[:, :lat] = y[:, :lat]
    tail = y[:, lat:L0_CQKV]
    lane = lax.broadcasted_iota(jnp.int32, tail.shape, 1)
    cqkv_ref[:, lat:] = jnp.where(lane < MLA_ROPE, tail, 0.0)
    xr_ref[...] = y[:, lat + MLA_ROPE:lat + MLA_ROPE + LRU_WIDTH]
    gr_ref[...] = y[:, lat + MLA_ROPE + LRU_WIDTH:]


def _in0(x_ctx, x_lat, g, mod, w):
    row = lambda i: (i, 0)
    tile = lambda i: i
    return pl.pallas_call(
        _in0_kernel,
        grid=(TOK // TM,),
        in_specs=[_ctx_spec(tile), _lat_spec(tile), _resident((1, D_MODEL)),
                  _mod_spec(_comb_tile, 0), _mod_spec(_comb_tile, 1), _resident((D_MODEL, L0_IN))],
        out_specs=[pl.BlockSpec((TM, L0_CQKV), row), pl.BlockSpec((TM, LRU_WIDTH), row),
                   pl.BlockSpec((TM, LRU_WIDTH), row)],
        out_shape=[jax.ShapeDtypeStruct((TOK, L0_CQKV), F32), jax.ShapeDtypeStruct((TOK, LRU_WIDTH), F32),
                   jax.ShapeDtypeStruct((TOK, LRU_WIDTH), F32)],
        compiler_params=_cparams(("parallel",)),
        name="l0_in_proj",
    )(x_ctx, x_lat, g, mod, mod, w)


def _mla_proj_kernel(c_ref, qn_ref, kvn_ref, wq_ref, wk_ref, wv_ref, cos_ref, sin_ref, q_ref, k_ref, v_ref):
    c = c_ref[...]
    cos = cos_ref[...]
    sin = sin_ref[...]
    nq = _rms(c[:, :MLA_Q_RANK], qn_ref[...]).astype(BF16)
    q = _dot(nq, wq_ref[...]) * (LOG2E * (MLA_NOPE + MLA_ROPE) ** -0.5)
    nkv = _rms(c[:, MLA_Q_RANK:MLA_Q_RANK + MLA_KV_RANK], kvn_ref[...]).astype(BF16)
    kn = _dot(nkv, wk_ref[...])
    v_ref[...] = _dot(nkv, wv_ref[...]).astype(BF16)
    kr = _rope(c[:, MLA_Q_RANK + MLA_KV_RANK:], cos, sin, MLA_ROPE // 4).astype(BF16)
    for h in range(MLA_HEADS):
        lo = h * MLA_QK
        q_ref[:, lo:lo + LANES] = q[:, lo:lo + LANES].astype(BF16)
        q_ref[:, lo + LANES:lo + MLA_QK] = _rope(q[:, lo + LANES:lo + MLA_QK], cos, sin, MLA_ROPE // 4).astype(BF16)
        k_ref[:, lo:lo + LANES] = kn[:, h * LANES:(h + 1) * LANES].astype(BF16)
        k_ref[:, lo + LANES:lo + MLA_QK] = kr


def _mla_proj(cqkv, qn, kvn, wq, wk, wv, cos, sin):
    row = lambda i: (i, 0)
    pos = lambda i: (i % TPB, 0)
    hq = MLA_HEADS * MLA_QK
    hv = MLA_HEADS * MLA_V
    return pl.pallas_call(
        _mla_proj_kernel,
        grid=(TOK // TM,),
        in_specs=[pl.BlockSpec((TM, L0_CQKV), row), _resident((1, MLA_Q_RANK)), _resident((1, MLA_KV_RANK)),
                  _resident((MLA_Q_RANK, hq)), _resident((MLA_KV_RANK, hv)), _resident((MLA_KV_RANK, hv)),
                  pl.BlockSpec((TM, LANES), pos), pl.BlockSpec((TM, LANES), pos)],
        out_specs=[pl.BlockSpec((TM, hq), row), pl.BlockSpec((TM, hq), row), pl.BlockSpec((TM, hv), row)],
        out_shape=[jax.ShapeDtypeStruct((TOK, hq), BF16), jax.ShapeDtypeStruct((TOK, hq), BF16),
                   jax.ShapeDtypeStruct((TOK, hv), BF16)],
        compiler_params=_cparams(("parallel",)),
        name="mla_proj",
    )(cqkv, qn, kvn, wq, wk, wv, cos, sin)


MLA_TQ = 256


def _softmax_pv(s, v):
    m = jnp.max(s, axis=-1, keepdims=True)
    p = jnp.exp2(s - m)
    l = jnp.sum(p, axis=-1, keepdims=True)
    return _dot(p.astype(BF16), v) / l


def _mla_attn_kernel(q_ref, k_ref, v_ref, o_ref):
    s = _dot_nt(q_ref[0:CTX_LEN, :], k_ref[0:CTX_LEN, :])
    o_ref[0:CTX_LEN, :] = _softmax_pv(s, v_ref[0:CTX_LEN, :]).astype(o_ref.dtype)
    for t in range(SEQ // MLA_TQ):
        r0 = CTX_LEN + t * MLA_TQ
        s = _dot_nt(q_ref[r0:r0 + MLA_TQ, :], k_ref[...])
        o_ref[r0:r0 + MLA_TQ, :] = _softmax_pv(s, v_ref[...]).astype(o_ref.dtype)


def _mla_attn(q, k, v):
    blk = lambda b, h: (b, h)
    return pl.pallas_call(
        _mla_attn_kernel,
        grid=(BATCH, MLA_HEADS),
        in_specs=[pl.BlockSpec((NB, MLA_QK), blk), pl.BlockSpec((NB, MLA_QK), blk), pl.BlockSpec((NB, MLA_V), blk)],
        out_specs=pl.BlockSpec((NB, MLA_V), blk),
        out_shape=jax.ShapeDtypeStruct((TOK, MLA_HEADS * MLA_V), BF16),
        compiler_params=_cparams(("parallel", "parallel")),
        name="mla_attn",
    )(q, k, v)


LRU_BW = LRU_WIDTH // LRU_BLOCKS
CTX_GROUPS = CTX_LEN // SUBLANES
LAT_GROUPS = SEQ // SUBLANES


def _scan_group(a, b, reverse):
    row = lax.broadcasted_iota(jnp.int32, a.shape, 0)
    for d in (1, 2, 4):
        shift = SUBLANES - d if reverse else d
        a_s = pltpu.roll(a, shift, axis=0)
        b_s = pltpu.roll(b, shift, axis=0)
        m = (row < SUBLANES - d) if reverse else (row >= d)
        b = jnp.where(m, a * b_s + b, b)
        a = jnp.where(m, a * a_s, a)
    return a, b


def _rglru_kernel(xr_ref, gr_ref, cw_ref, cb_ref, wa_ref, ba_ref, wx_ref, bx_ref, lam_ref, y_ref,
                  af_ref, bf_ref, ab_ref, bb_ref, hf_ref, hb_ref):
    row8 = lax.broadcasted_iota(jnp.int32, (SUBLANES, LRU_BW), 0)

    def taps(seg):
        n = seg.shape[0]

        def shifted(shift, keep, first):
            r = pltpu.roll(seg, shift % n, axis=0)
            if first:
                return jnp.concatenate([jnp.where(keep, r[:SUBLANES], 0.0), r[SUBLANES:]], axis=0)
            return jnp.concatenate([r[:-SUBLANES], jnp.where(keep, r[-SUBLANES:], 0.0)], axis=0)

        return shifted(2, row8 >= 2, True), shifted(1, row8 >= 1, True), shifted(-1, row8 < SUBLANES - 1, False)

    x = xr_ref[...]
    tc = taps(x[:CTX_LEN])
    tl = taps(x[CTX_LEN:])
    xm2, xm1, xp1 = [jnp.concatenate([a, b], axis=0) for a, b in zip(tc, tl)]
    u = cb_ref[...] + xm2 * cw_ref[0:1, :] + xm1 * cw_ref[1:2, :] + x * cw_ref[2:3, :] + xp1 * cw_ref[3:4, :]
    ub = u.astype(BF16)
    for d, (a_ref, b_ref) in enumerate(((af_ref, bf_ref), (ab_ref, bb_ref))):
        r = _sigmoid(_dot(ub, wa_ref[d, 0].astype(BF16)) + ba_ref[d:d + 1, :])
        ig = _sigmoid(_dot(ub, wx_ref[d, 0].astype(BF16)) + bx_ref[d:d + 1, :])
        z = -lam_ref[d:d + 1, :]
        softplus = jnp.maximum(z, 0.0) + jnp.log(1.0 + jnp.exp(-jnp.abs(z)))
        log_a = -LRU_C * r * softplus
        a = jnp.exp(log_a)
        a_ref[...] = a
        t = 1.0 - a * a
        b_ref[...] = jnp.where(t > 0.0, t * lax.rsqrt(t), 0.0) * (ig * u)

    def step(gf, gb, hf, hb):
        rf = pl.multiple_of(gf * SUBLANES, SUBLANES)
        a, b = _scan_group(af_ref[pl.ds(rf, SUBLANES), :], bf_ref[pl.ds(rf, SUBLANES), :], False)
        h = a * hf + b
        hf_ref[pl.ds(rf, SUBLANES), :] = h
        hf = jnp.broadcast_to(h[SUBLANES - 1:SUBLANES, :], h.shape)
        rb = pl.multiple_of(gb * SUBLANES, SUBLANES)
        a, b = _scan_group(ab_ref[pl.ds(rb, SUBLANES), :], bb_ref[pl.ds(rb, SUBLANES), :], True)
        h = a * hb + b
        hb_ref[pl.ds(rb, SUBLANES), :] = h
        hb = jnp.broadcast_to(h[0:1, :], h.shape)
        return hf, hb

    zero = jnp.zeros((SUBLANES, LRU_BW), F32)
    carry = lax.fori_loop(0, CTX_GROUPS, lambda i, c: step(i, CTX_GROUPS - 1 - i, *c), (zero, zero), unroll=4)
    lax.fori_loop(0, LAT_GROUPS, lambda i, c: step(CTX_GROUPS + i, CTX_GROUPS + LAT_GROUPS - 1 - i, *c), carry, unroll=4)
    y_ref[...] = ((hf_ref[...] + hb_ref[...]) * jax.nn.gelu(gr_ref[...])).astype(y_ref.dtype)


def _rglru(xr, gr, conv_w, conv_b, wa, ba, wx, bx, lam):
    blk = lambda b, n: (b, n)
    col = lambda b, n: (0, n)
    gate = lambda b, n: (0, n, 0, 0)
    seg = pltpu.VMEM((NB, LRU_BW), F32)
    return pl.pallas_call(
        _rglru_kernel,
        grid=(BATCH, LRU_BLOCKS),
        in_specs=[pl.BlockSpec((NB, LRU_BW), blk), pl.BlockSpec((NB, LRU_BW), blk),
                  pl.BlockSpec((4, LRU_BW), col), pl.BlockSpec((1, LRU_BW), col),
                  pl.BlockSpec((2, 1, LRU_BW, LRU_BW), gate), pl.BlockSpec((2, LRU_BW), col),
                  pl.BlockSpec((2, 1, LRU_BW, LRU_BW), gate), pl.BlockSpec((2, LRU_BW), col),
                  pl.BlockSpec((2, LRU_BW), col)],
        out_specs=pl.BlockSpec((NB, LRU_BW), blk),
        out_shape=jax.ShapeDtypeStruct((TOK, LRU_WIDTH), BF16),
        scratch_shapes=[seg, seg, seg, seg, seg, seg],
        compiler_params=_cparams(("parallel", "parallel")),
        name="rglru",
    )(xr, gr, conv_w, conv_b.reshape(1, LRU_WIDTH), wa, ba, wx, bx, lam)


def _gather_rows(ys_hbm, y_buf, sem, dest_ref, base, n_tok):
    def row_copy(k, r, d):
        return pltpu.make_async_copy(ys_hbm.at[pl.ds(d, 1)], y_buf.at[k, pl.ds(r, 1)], sem)

    def start(straight_line=False):
        def body(r, c):
            for k in range(2):
                row_copy(k, r, dest_ref[k * n_tok + base + r]).start(priority=k)
            return c

        if straight_line:
            for r in range(TM):
                body(r, 0)
        else:
            lax.fori_loop(0, TM, body, 0, unroll=8)

    def wait():
        def body(r, c):
            for k in range(2):
                row_copy(k, r, 0).wait()
            return c

        lax.fori_loop(0, TM, body, 0, unroll=8)

    return start, wait


L1_Q = WIN_HEADS * HEAD_DIM
L1_KV = WIN_KV_HEADS * HEAD_DIM
L1_NA = NA_HEADS * HEAD_DIM
L1_IN = L1_Q + 2 * L1_KV + 3 * L1_NA


def _in1_kernel(dest_ref, ys_hbm, x_ref, gate_ref, g2_ref, g_ref, sh_ref, sc_ref, w_ref, cos_ref, sin_ref,
                xo_ref, qw_ref, kw_ref, vw_ref, qn_ref, kn_ref, vn_ref, y_buf, sem):
    i = pl.program_id(0)
    last = pl.num_programs(0) - 1
    slot = i % 2
    rows = lambda tile, sl: _gather_rows(ys_hbm, y_buf.at[sl], sem.at[sl], dest_ref, tile * TM, TOK)

    @pl.when(i == 0)
    def _():
        rows(0, 0)[0]()

    rows(i, slot)[1]()
    rows(jnp.minimum(i + 1, last), 1 - slot)[0](straight_line=True)

    gate = gate_ref[...]
    x = x_ref[...] + g2_ref[0] * (gate[:, 4:5] * y_buf[slot, 0] + gate[:, 5:6] * y_buf[slot, 1])
    xo_ref[...] = x
    h = _rms(x, g_ref[...]) * (1.0 + sc_ref[0]) + sh_ref[0]
    y = _dot(h.astype(BF16), w_ref[...])
    cos = cos_ref[...]
    sin = sin_ref[...]
    scale = LOG2E * HEAD_DIM ** -0.5
    for hd in range(WIN_HEADS):
        lo = hd * HEAD_DIM
        qw_ref[:, lo:lo + HEAD_DIM] = _rope(y[:, lo:lo + HEAD_DIM] * scale, cos, sin, HEAD_DIM // 4).astype(BF16)
    for hd in range(WIN_KV_HEADS):
        lo = hd * HEAD_DIM
        kw_ref[:, lo:lo + HEAD_DIM] = _rope(y[:, L1_Q + lo:L1_Q + lo + HEAD_DIM], cos, sin, HEAD_DIM // 4).astype(BF16)
    o = L1_Q + L1_KV
    vw_ref[...] = y[:, o:o + L1_KV].astype(BF16)
    o += L1_KV
    qn_ref[...] = (y[:, o:o + L1_NA] * scale).astype(BF16)
    kn_ref[...] = y[:, o + L1_NA:o + 2 * L1_NA].astype(BF16)
    vn_ref[...] = y[:, o + 2 * L1_NA:].astype(BF16)

    @pl.when(i == last)
    def _():
        rows(last, 1 - slot)[1]()


def _in1(dest, ys, x, gates, mod_prev, g, mod, w, cos, sin):
    row = lambda i, d: (i, 0)
    pos = lambda i, d: (i % TPB, 0)
    widths = (L1_Q, L1_KV, L1_KV, L1_NA, L1_NA, L1_NA)
    return pl.pallas_call(
        _in1_kernel,
        grid_spec=pltpu.PrefetchScalarGridSpec(
            num_scalar_prefetch=1,
            grid=(TOK // TM,),
            in_specs=[pl.BlockSpec(memory_space=pl.ANY), pl.BlockSpec((TM, D_MODEL), row),
                      pl.BlockSpec((TM, SUBLANES), row), _mod_spec(_comb_tile, 5), _resident((1, D_MODEL)),
                      _mod_spec(_comb_tile, 0), _mod_spec(_comb_tile, 1), _resident((D_MODEL, L1_IN)),
                      pl.BlockSpec((TM, LANES), pos), pl.BlockSpec((TM, LANES), pos)],
            out_specs=[pl.BlockSpec((TM, D_MODEL), row)] + [pl.BlockSpec((TM, n), row) for n in widths],
            scratch_shapes=[pltpu.VMEM((2, 2, TM, D_MODEL), F32), pltpu.SemaphoreType.DMA((2,))]),
        out_shape=[jax.ShapeDtypeStruct((TOK, D_MODEL), F32)] + [jax.ShapeDtypeStruct((TOK, n), BF16) for n in widths],
        compiler_params=_cparams(("arbitrary",)),
        name="l1_in_proj",
    )(dest, ys, x, gates, mod_prev, g, mod, mod, w, cos, sin)


WIN_TQ = 128
WIN_SPAN = WIN_TQ + 2 * WINDOW
WIN_G = WIN_HEADS // WIN_KV_HEADS


def _win_key_start(n):
    return min(max((n - 1) * WIN_TQ, 0), SEQ - WIN_SPAN)


WIN_OFFSETS = sorted({n * WIN_TQ - _win_key_start(n) for n in range(SEQ // WIN_TQ)})


def _win_mask_bias():
    qoff = jnp.arange(WIN_G * WIN_TQ)[:, None] % WIN_TQ
    koff = jnp.arange(WIN_SPAN)[None, :]
    return jnp.stack([jnp.where(jnp.abs(qoff - koff + d) <= WINDOW, 0.0, NEG_INF) for d in WIN_OFFSETS]).astype(F32)


def _win_kernel(sink_ref, q_ref, k_ref, v_ref, mask_ref, o_ref):
    hk = pl.program_id(1)
    o_ref[0:CTX_LEN, :] = jnp.zeros((CTX_LEN, WIN_G * HEAD_DIM), o_ref.dtype)
    rows = WIN_G * WIN_TQ
    head = lax.broadcasted_iota(jnp.int32, (rows, 1), 0) // WIN_TQ
    sink = jnp.zeros((rows, 1), F32)
    for g in range(WIN_G):
        sink = jnp.where(head == g, sink_ref[hk * WIN_G + g] * LOG2E, sink)

    for n in range(SEQ // WIN_TQ):
        r0 = CTX_LEN + n * WIN_TQ
        start = _win_key_start(n)
        ks = CTX_LEN + start
        q4 = q_ref[r0:r0 + WIN_TQ, :]
        q = jnp.concatenate([q4[:, g * HEAD_DIM:(g + 1) * HEAD_DIM] for g in range(WIN_G)], axis=0)
        s_c = _dot_nt(q, k_ref[0:CTX_LEN, :])
        s_w = _dot_nt(q, k_ref[ks:ks + WIN_SPAN, :]) + mask_ref[WIN_OFFSETS.index(n * WIN_TQ - start)]
        m = jnp.maximum(jnp.maximum(jnp.max(s_c, axis=-1, keepdims=True), jnp.max(s_w, axis=-1, keepdims=True)), sink)
        p_c = jnp.exp2(s_c - m)
        p_w = jnp.exp2(s_w - m)
        l = jnp.sum(p_c, axis=-1, keepdims=True) + jnp.sum(p_w, axis=-1, keepdims=True) + jnp.exp2(sink - m)
        o = (_dot(p_w.astype(BF16), v_ref[ks:ks + WIN_SPAN, :]) + _dot(p_c.astype(BF16), v_ref[0:CTX_LEN, :])) / l
        for g in range(WIN_G):
            o_ref[r0:r0 + WIN_TQ, g * HEAD_DIM:(g + 1) * HEAD_DIM] = o[g * WIN_TQ:(g + 1) * WIN_TQ].astype(o_ref.dtype)


def _win_attn(sink, q, k, v):
    blk = lambda b, h, *_: (b, h)
    mask = _win_mask_bias()
    return pl.pallas_call(
        _win_kernel,
        grid_spec=pltpu.PrefetchScalarGridSpec(
            num_scalar_prefetch=1,
            grid=(BATCH, WIN_KV_HEADS),
            in_specs=[pl.BlockSpec((NB, WIN_G * HEAD_DIM), blk), pl.BlockSpec((NB, HEAD_DIM), blk),
                      pl.BlockSpec((NB, HEAD_DIM), blk), _resident(mask.shape)],
            out_specs=pl.BlockSpec((NB, WIN_G * HEAD_DIM), blk)),
        out_shape=jax.ShapeDtypeStruct((TOK, L1_Q), BF16),
        compiler_params=_cparams(("parallel", "parallel")),
        name="window_attn",
    )(sink, q, k, v, mask)


NA_GRID_ROWS = SEQ // GRID_W
NA_BAND = NA_ROWS * GRID_W
NA_RPI = 4


def _na_kernel(q_ref, k_ref, v_ref, bias_ref, o_ref):
    o_ref[0:CTX_LEN, :] = jnp.zeros((CTX_LEN, HEAD_DIM), o_ref.dtype)

    for i in range(NA_GRID_ROWS // NA_RPI):
        qs = CTX_LEN + i * NA_RPI * GRID_W
        q = q_ref[qs:qs + NA_RPI * GRID_W, :]
        s_c = _dot_nt(q, k_ref[0:CTX_LEN, :])
        starts = []
        s_w = []
        for j in range(NA_RPI):
            r = i * NA_RPI + j
            r0 = min(max(r - NA_ROWS // 2, 0), NA_GRID_ROWS - NA_ROWS)
            ks = CTX_LEN + r0 * GRID_W
            starts.append(ks)
            s_w.append(_dot_nt(q[j * GRID_W:(j + 1) * GRID_W], k_ref[ks:ks + NA_BAND, :])
                       + jnp.concatenate([bias_ref[0, r0 - r + NA_ROWS - 1 + 2 * p] for p in range(NA_ROWS // 2)],
                                         axis=1))
        s_w = jnp.concatenate(s_w, axis=0)
        m = jnp.maximum(jnp.max(s_c, axis=-1, keepdims=True), jnp.max(s_w, axis=-1, keepdims=True))
        p_c = jnp.exp2(s_c - m)
        p_w = jnp.exp2(s_w - m)
        l = jnp.sum(p_c, axis=-1, keepdims=True) + jnp.sum(p_w, axis=-1, keepdims=True)
        p_w = p_w.astype(BF16)
        o_w = jnp.concatenate([_dot(p_w[j * GRID_W:(j + 1) * GRID_W], v_ref[starts[j]:starts[j] + NA_BAND, :])
                               for j in range(NA_RPI)], axis=0)
        o = (o_w + _dot(p_c.astype(BF16), v_ref[0:CTX_LEN, :])) / l
        o_ref[qs:qs + NA_RPI * GRID_W, :] = o.astype(o_ref.dtype)


def _na_attn(q, k, v, bias):
    blk = lambda b, h: (b, h)
    return pl.pallas_call(
        _na_kernel,
        grid=(BATCH, NA_HEADS),
        in_specs=[pl.BlockSpec((NB, HEAD_DIM), blk), pl.BlockSpec((NB, HEAD_DIM), blk),
                  pl.BlockSpec((NB, HEAD_DIM), blk),
                  pl.BlockSpec((1, 2 * NA_ROWS - 2, GRID_W, 2 * GRID_W), lambda b, h: (h, 0, 0, 0))],
        out_specs=pl.BlockSpec((NB, HEAD_DIM), blk),
        out_shape=jax.ShapeDtypeStruct((TOK, L1_NA), BF16),
        compiler_params=_cparams(("parallel", "parallel")),
        name="na_attn",
    )(q, k, v, bias)


def _na_bias_table(rpb):
    cols = jnp.arange(GRID_W)
    c0 = jnp.clip(cols - NA_COLS // 2, 0, GRID_W - NA_COLS)
    kc = cols[None, :]
    valid = (kc >= c0[:, None]) & (kc < c0[:, None] + NA_COLS)
    pick = (jnp.arange(2 * NA_COLS - 1)[:, None, None] == (kc - cols[:, None] + NA_COLS - 1)[None]).astype(F32)
    tbl = jnp.einsum("hdo,ock->hdck", rpb.astype(F32) * LOG2E, pick, precision=lax.Precision.HIGHEST)
    tbl = jnp.where(valid[None, None], tbl, NEG_INF)
    return jnp.concatenate([tbl[:, :-1], tbl[:, 1:]], axis=-1)


def _router_logits(h, rwh_ref, rwl_ref):
    hh = h.astype(BF16)
    hl = (h - hh.astype(F32)).astype(BF16)
    return _dot_nt(rwh_ref[...], hh) + (_dot_nt(rwh_ref[...], hl) + _dot_nt(rwl_ref[...], hh))


def _router(logits, rb_ref, tri_ref, carry):
    tm = logits.shape[1]
    scores = jax.nn.sigmoid(logits)
    biased = scores + rb_ref[...]
    nj = EXPERTS_PER_GROUP
    s = [biased[j * N_GROUPS:(j + 1) * N_GROUPS] for j in range(nj)]
    u = [scores[j * N_GROUPS:(j + 1) * N_GROUPS] for j in range(nj)]
    gs = None
    for a in range(nj):
        for b in range(a + 1, nj):
            pair = s[a] + s[b]
            gs = pair if gs is None else jnp.maximum(gs, pair)
    giota = lax.broadcasted_iota(jnp.int32, (N_GROUPS, tm), 0).astype(F32)
    gmax = jnp.max(gs, axis=0, keepdims=True)
    gidx = jnp.min(jnp.where(gs == gmax, giota, float(N_GROUPS)), axis=0, keepdims=True)
    gm = giota == gidx
    v = [jnp.sum(jnp.where(gm, s[j], 0.0), axis=0, keepdims=True) for j in range(nj)]
    w = [jnp.sum(jnp.where(gm, u[j], 0.0), axis=0, keepdims=True) for j in range(nj)]
    sel = []
    for j in range(nj):
        beaten = jnp.zeros((1, tm), F32)
        for i in range(nj):
            if i != j:
                ahead = (v[i] >= v[j]) if i < j else (v[i] > v[j])
                beaten = beaten + jnp.where(ahead, 1.0, 0.0)
        sel.append(beaten < 2.0)
    wsum = sum(jnp.where(sel[j], w[j], 0.0) for j in range(nj))
    first = functools.reduce(jnp.minimum, [jnp.where(sel[j], float(j), float(nj)) for j in range(nj)])
    last = functools.reduce(jnp.maximum, [jnp.where(sel[j], float(j), -1.0) for j in range(nj)])
    gmf = jnp.where(gm, 1.0, 0.0)
    cnt = jnp.concatenate([jnp.where(sel[j], gmf, 0.0) for j in range(nj)], axis=0)
    pos = _dot(cnt.astype(BF16), tri_ref[...]) + carry
    rank = [jnp.sum(cnt[j * N_GROUPS:(j + 1) * N_GROUPS] * pos[j * N_GROUPS:(j + 1) * N_GROUPS], axis=0, keepdims=True)
            for j in range(nj)]
    pick = lambda which, vals: sum(jnp.where(which == float(j), vals[j], 0.0) for j in range(nj))
    gate = [w[j] / wsum for j in range(nj)]
    zero = jnp.zeros((1, tm), F32)
    info = jnp.concatenate(
        [gidx * nj + first, gidx * nj + last, pick(first, rank), pick(last, rank), pick(first, gate), pick(last, gate),
         zero, zero], axis=0)
    return info, carry + jnp.sum(cnt, axis=1, keepdims=True)


OUT_SUB = 128


def _out_kernel(ya_ref, yb_ref, *refs, n_tiles, split_x):
    i = pl.program_id(0)
    if split_x:
        xc_ref, xl_ref, *refs = refs
        x_in = lambda: _pick_tile(jnp.minimum(i, n_tiles - 1), xc_ref, xl_ref)
    else:
        x_ref, *refs = refs
        x_in = lambda: x_ref[...]
    (w_ref, g1_ref, n_ref, sh_ref, sc_ref, rwh_ref, rwl_ref, rb_ref, tri_ref,
     xo_ref, h_ref, info_ref, counts_ref, carry_ref, hbuf_ref) = refs

    @pl.when(i == 0)
    def _():
        carry_ref[...] = jnp.zeros_like(carry_ref)
        hbuf_ref[1] = jnp.zeros((TM, D_MODEL), F32)

    subs = range(0, TM, OUT_SUB)
    logits = [_router_logits(hbuf_ref[(i + 1) % 2, r0:r0 + OUT_SUB, :], rwh_ref, rwl_ref) for r0 in subs]
    half = w_ref.shape[0] // 2
    y = _dot(ya_ref[...], w_ref[0:half, :]) + _dot(yb_ref[...], w_ref[half:, :])
    old = carry_ref[:, 0:1]
    carry = old
    for r0, lg in zip(subs, logits):
        info_ref[:, r0:r0 + OUT_SUB], carry = _router(lg, rb_ref, tri_ref, carry)
    counts = jnp.broadcast_to(jnp.where(i > 0, carry, old), carry_ref.shape)
    carry_ref[...] = counts
    counts_ref[...] = counts

    x = x_in() + g1_ref[0] * y
    xo_ref[...] = x
    h = _rms(x, n_ref[...]) * (1.0 + sc_ref[0]) + sh_ref[0]
    h_ref[...] = h
    hbuf_ref[i % 2] = h


def _out_proj(ya, yb, x, w_out, mod, norm, rwh, rwl, rb, tri, tile_fn, n_tiles):
    split_x = isinstance(x, tuple)
    half = w_out.shape[0] // 2
    this = lambda i: jnp.minimum(i, n_tiles - 1)
    tile = lambda i: tile_fn(this(i))
    src = lambda i: (tile(i)[0], 0)
    dst = lambda i: (this(i), 0)
    n_tok = n_tiles * TM
    x_specs = [_ctx_spec(this), _lat_spec(this)] if split_x else [pl.BlockSpec((TM, D_MODEL), src)]
    return pl.pallas_call(
        functools.partial(_out_kernel, n_tiles=n_tiles, split_x=split_x),
        grid=(n_tiles + 1,),
        in_specs=[pl.BlockSpec((TM, half), src), pl.BlockSpec((TM, half), src), *x_specs,
                  _resident(w_out.shape),
                  _mod_spec(tile, 2), _resident((1, D_MODEL)), _mod_spec(tile, 3), _mod_spec(tile, 4),
                  _resident((N_EXPERTS, D_MODEL)), _resident((N_EXPERTS, D_MODEL)), _resident((N_EXPERTS, 1)),
                  _resident((OUT_SUB, OUT_SUB))],
        out_specs=[pl.BlockSpec((TM, D_MODEL), dst), pl.BlockSpec((TM, D_MODEL), dst),
                   pl.BlockSpec((SUBLANES, TM), lambda i: (0, jnp.maximum(i - 1, 0))),
                   pl.BlockSpec((N_EXPERTS, LANES), lambda i: (0, 0))],
        out_shape=[jax.ShapeDtypeStruct((n_tok, D_MODEL), F32), jax.ShapeDtypeStruct((n_tok, D_MODEL), F32),
                   jax.ShapeDtypeStruct((SUBLANES, n_tok), F32), jax.ShapeDtypeStruct((N_EXPERTS, LANES), F32)],
        scratch_shapes=[pltpu.VMEM((N_EXPERTS, LANES), F32), pltpu.VMEM((2, TM, D_MODEL), F32)],
        compiler_params=_cparams(("arbitrary",)),
        name="out_proj_router",
    )(ya, yb, *(x if split_x else (x,)), w_out, mod, norm, mod, mod, rwh, rwl, rb, tri)


def _lookup(table, idx):
    onehot = idx[..., None] == jnp.arange(table.shape[0], dtype=jnp.int32)
    return jnp.sum(jnp.where(onehot, table, 0), axis=-1)


def _dispatch_plan(info, counts, n_tok):
    n_steps = 2 * n_tok // TMX + N_EXPERTS - 1
    experts = jnp.arange(N_EXPERTS, dtype=jnp.int32)
    cnt = counts[:, 0].astype(jnp.int32).reshape(EXPERTS_PER_GROUP, N_GROUPS).T.reshape(N_EXPERTS)
    end = jnp.cumsum(cnt)
    off = end - cnt
    first_tile = off // TMX
    visits = jnp.where(cnt > 0, (end - 1) // TMX - first_tile + 1, 0)
    visit_end = jnp.cumsum(visits)
    n_valid = visit_end[-1]
    dest = _lookup(off, info[0:2].astype(jnp.int32)) + info[2:4].astype(jnp.int32)
    step = jnp.minimum(jnp.arange(n_steps, dtype=jnp.int32), n_valid - 1)
    e = jnp.sum(visit_end[None, :] <= step[:, None], axis=1).astype(jnp.int32)
    k = step - _lookup(visit_end - visits, e)
    tile = _lookup(first_tile, e) + k
    lo = jnp.clip(_lookup(off, e) - tile * TMX, 0, TMX)
    hi = jnp.clip(_lookup(end, e) - tile * TMX, 0, TMX)
    later = (experts[None, :] > experts[:, None]) & (cnt[None, :] > 0)
    nxt = jnp.min(jnp.where(later, experts[None, :], N_EXPERTS), axis=1)
    nxt = jnp.where(nxt == N_EXPERTS, -1, nxt)
    slot = (jnp.cumsum((cnt > 0).astype(jnp.int32)) - 1) % 2
    i32 = lambda v: v.astype(jnp.int32)
    return (dest.reshape(-1), i32(tile), i32(e), i32(n_valid.reshape(1)), i32(lo), i32(hi), i32(k == 0),
            i32(_lookup(nxt, e)), i32(_lookup(slot, e)))


def _dispatch_kernel(dest_ref, h_ref, xs_hbm, sem, *, n_tok):
    i = pl.program_id(0)

    def row_copy(r, d):
        return pltpu.make_async_copy(h_ref.at[pl.ds(r, 1)], xs_hbm.at[pl.ds(d, 1)], sem)

    def start(r, c):
        for k in range(2):
            row_copy(r, dest_ref[k * n_tok + i * TM + r]).start(priority=k)
        return c

    lax.fori_loop(0, TM, start, 0, unroll=8)

    def wait(r, c):
        for k in range(2):
            row_copy(r, 0).wait()
        return c

    lax.fori_loop(0, TM, wait, 0, unroll=8)


def _dispatch(dest, h):
    n_tok = h.shape[0]
    return pl.pallas_call(
        functools.partial(_dispatch_kernel, n_tok=n_tok),
        grid_spec=pltpu.PrefetchScalarGridSpec(
            num_scalar_prefetch=1,
            grid=(n_tok // TM,),
            in_specs=[pl.BlockSpec((TM, D_MODEL), lambda i, d: (i, 0))],
            out_specs=pl.BlockSpec(memory_space=pl.ANY),
            scratch_shapes=[pltpu.SemaphoreType.DMA(())]),
        out_shape=jax.ShapeDtypeStruct((2 * n_tok, D_MODEL), F32),
        compiler_params=_cparams(("arbitrary",)),
        name="moe_dispatch",
    )(dest, h)


X_SLOTS = 3


def _moe_kernel(tile_ref, te_ref, nv_ref, lo_ref, hi_ref, first_ref, nxt_ref, slot_ref,
                xs_hbm, wg_hbm, wu_hbm, wd_hbm, ys_ref, wg_f, wu_f, wd_f, wg_s, wu_s, wd_s, sem, x_buf, x_sem,
                *, n_tiles):
    s = pl.program_id(0)

    def tile_copy(t):
        r0 = pl.multiple_of(t * TMX, TMX)
        return pltpu.make_async_copy(xs_hbm.at[pl.ds(r0, TMX)], x_buf.at[t % X_SLOTS], x_sem.at[t % X_SLOTS])

    def fetch(e, slot):
        return [pltpu.make_async_copy(src.at[e], dst.at[slot], sem.at[slot])
                for src, dst in ((wg_hbm, wg_f), (wu_hbm, wu_f), (wd_hbm, wd_f))]

    @pl.when(s < nv_ref[0])
    def _():
        @pl.when(first_ref[s] == 1)
        def _():
            slot = slot_ref[s]

            @pl.when(s == 0)
            def _():
                for c in fetch(te_ref[0], slot):
                    c.start()

            for c in fetch(te_ref[s], slot):
                c.wait()

            @pl.when(nxt_ref[s] >= 0)
            def _():
                for c in fetch(nxt_ref[s], 1 - slot):
                    c.start()

            wg_s[...] = wg_f[slot].astype(BF16)
            wu_s[...] = wu_f[slot].astype(BF16)
            wd_s[...] = wd_f[slot].astype(BF16)

        lo = lo_ref[s]
        hi = hi_ref[s]
        t = tile_ref[s]

        @pl.when(lo == 0)
        def _():
            @pl.when(s == 0)
            def _():
                for t0 in range(min(X_SLOTS - 1, n_tiles)):
                    tile_copy(t0).start()

            tile_copy(t).wait()

            @pl.when(t + X_SLOTS - 1 < n_tiles)
            def _():
                tile_copy(t + X_SLOTS - 1).start()

            ys_ref[...] = jnp.zeros_like(ys_ref)

        def visit(r0, n):
            x = x_buf[t % X_SLOTS, r0:r0 + n, :].astype(BF16)
            hg = _dot(x, wg_s[...])
            he = (hg * jax.nn.sigmoid(hg)) * _dot(x, wu_s[...])
            y = _dot(he.astype(BF16), wd_s[...])
            row = r0 + lax.broadcasted_iota(jnp.int32, (n, 1), 0)
            ys_ref[r0:r0 + n, :] = jnp.where((row >= lo) & (row < hi), y, ys_ref[r0:r0 + n, :])

        half = TMX // 2
        lower = hi <= half
        upper = lo >= half
        pl.when(lower)(lambda: visit(0, half))
        pl.when(upper)(lambda: visit(half, half))
        pl.when(jnp.logical_not(jnp.logical_or(lower, upper)))(lambda: visit(0, TMX))


def _moe(plan, xs, w_gate, w_up, w_down):
    n_steps = plan[0].shape[0]
    tile = lambda s, t, *_: (t[s], 0)
    any_spec = pl.BlockSpec(memory_space=pl.ANY)
    return pl.pallas_call(
        functools.partial(_moe_kernel, n_tiles=xs.shape[0] // TMX),
        grid_spec=pltpu.PrefetchScalarGridSpec(
            num_scalar_prefetch=len(plan),
            grid=(n_steps,),
            in_specs=[any_spec, any_spec, any_spec, any_spec],
            out_specs=pl.BlockSpec((TMX, D_MODEL), tile),
            scratch_shapes=[pltpu.VMEM((2, D_MODEL, EXPERT_FF), F32), pltpu.VMEM((2, D_MODEL, EXPERT_FF), F32),
                            pltpu.VMEM((2, EXPERT_FF, D_MODEL), F32),
                            pltpu.VMEM((D_MODEL, EXPERT_FF), BF16), pltpu.VMEM((D_MODEL, EXPERT_FF), BF16),
                            pltpu.VMEM((EXPERT_FF, D_MODEL), BF16), pltpu.SemaphoreType.DMA((2,)),
                            pltpu.VMEM((X_SLOTS, TMX, D_MODEL), F32), pltpu.SemaphoreType.DMA((X_SLOTS,))]),
        out_shape=jax.ShapeDtypeStruct(xs.shape, F32),
        compiler_params=_cparams(("arbitrary",)),
        name="moe_experts",
    )(*plan, xs, w_gate, w_up, w_down)


def _combine_kernel(dest_ref, ys_hbm, x_ref, gate_ref, g2_ref, n_ref, o_ref, y_buf, sem, *, n_tok):
    start, wait = _gather_rows(ys_hbm, y_buf, sem, dest_ref, pl.program_id(0) * TM, n_tok)
    start()
    wait()
    gate = gate_ref[...]
    x = x_ref[...] + g2_ref[0] * (gate[:, 4:5] * y_buf[0] + gate[:, 5:6] * y_buf[1])
    o_ref[...] = _rms(x, n_ref[...])


def _combine(dest, ys, x, gates, mod, norm, tile_fn, n_tiles):
    n_tok = n_tiles * TM
    row = lambda i, d: (i, 0)
    return pl.pallas_call(
        functools.partial(_combine_kernel, n_tok=n_tok),
        grid_spec=pltpu.PrefetchScalarGridSpec(
            num_scalar_prefetch=1,
            grid=(n_tiles,),
            in_specs=[pl.BlockSpec(memory_space=pl.ANY), pl.BlockSpec((TM, D_MODEL), row),
                      pl.BlockSpec((TM, SUBLANES), row), _mod_spec(tile_fn, 5),
                      pl.BlockSpec((1, D_MODEL), lambda i, d: (0, 0))],
            out_specs=pl.BlockSpec((TM, D_MODEL), row),
            scratch_shapes=[pltpu.VMEM((2, TM, D_MODEL), F32), pltpu.SemaphoreType.DMA(())]),
        out_shape=jax.ShapeDtypeStruct((n_tok, D_MODEL), F32),
        compiler_params=_cparams(("arbitrary",)),
        name="moe_combine",
    )(dest, ys, x, gates, mod, norm)


def _experts(h, info, counts, w_gate, w_up, w_down):
    dest, *plan = _dispatch_plan(info, counts, h.shape[0])
    return dest, _moe(plan, _dispatch(dest, h), w_gate, w_up, w_down)


def _rope_tables(dim):
    half = dim // 2
    inv_freq = ROPE_THETA ** (-jnp.arange(0, half, 2, dtype=F32) / half)
    ang_r = jnp.arange(SEQ // GRID_W, dtype=F32)[:, None] * inv_freq
    ang_c = jnp.arange(GRID_W, dtype=F32)[:, None] * inv_freq
    by_row = lambda v: jnp.repeat(v, GRID_W, axis=0)
    by_col = lambda v: jnp.tile(v, (SEQ // GRID_W, 1))
    cos_r, sin_r, cos_c, sin_c = by_row(jnp.cos(ang_r)), by_row(jnp.sin(ang_r)), by_col(jnp.cos(ang_c)), by_col(jnp.sin(ang_c))
    cos = jnp.concatenate([cos_r, cos_r, cos_c, cos_c], axis=-1)
    sin = jnp.concatenate([-sin_r, sin_r, -sin_c, sin_c], axis=-1)
    pad = LANES - dim
    cos = jnp.pad(cos, ((0, 0), (0, pad)))
    sin = jnp.pad(sin, ((0, 0), (0, pad)))
    ctx_cos = jnp.pad(jnp.ones((CTX_LEN, dim), F32), ((0, 0), (0, pad)))
    return jnp.concatenate([ctx_cos, cos], axis=0), jnp.concatenate([jnp.zeros((CTX_LEN, LANES), F32), sin], axis=0)


def kernel(x, c, ctx, c_ctx, router_w, router_b, final_norm, l0_mod_w, l0_mod_b, l0_norm_mix, l0_norm_ffn, l0_w_in, l0_q_norm, l0_w_uq, l0_kv_norm, l0_w_ukv, l0_conv_w, l0_conv_b, l0_gate_a_w, l0_gate_a_b, l0_gate_x_w, l0_gate_x_b, l0_lru_lambda, l0_w_out, l0_exp_gate, l0_exp_up, l0_exp_down, l1_mod_w, l1_mod_b, l1_norm_mix, l1_norm_ffn, l1_w_in, l1_sink, l1_rpb, l1_w_out, l1_exp_gate, l1_exp_up, l1_exp_down):
    row = lambda v: v.reshape(1, -1)
    x_in = (ctx.reshape(BATCH * CTX_LEN, D_MODEL), x.reshape(BATCH * SEQ, D_MODEL))

    cs = jnp.concatenate([c, c_ctx[None], jnp.zeros((SUBLANES - BATCH - 1, D_MODEL), F32)], axis=0)
    mod0 = _modulation(cs, l0_mod_w, l0_mod_b)
    mod1 = _modulation(cs, l1_mod_w, l1_mod_b)

    perm = jnp.arange(N_EXPERTS).reshape(N_GROUPS, EXPERTS_PER_GROUP).T.reshape(-1)
    rwt = router_w.T[perm]
    rwh = rwt.astype(BF16)
    rwl = (rwt - rwh.astype(F32)).astype(BF16)
    rb = router_b[perm].reshape(N_EXPERTS, 1).astype(F32)
    tri = jnp.triu(jnp.ones((OUT_SUB, OUT_SUB), F32), k=1).astype(BF16)
    router = (rwh, rwl, rb, tri)

    cqkv, xr, gr = _in0(*x_in, row(l0_norm_mix), mod0, l0_w_in.astype(BF16))
    wq = jnp.pad(l0_w_uq.reshape(MLA_Q_RANK, MLA_HEADS, MLA_NOPE + MLA_ROPE),
                 ((0, 0), (0, 0), (0, MLA_QK - MLA_NOPE - MLA_ROPE))).reshape(MLA_Q_RANK, MLA_HEADS * MLA_QK)
    wkv = l0_w_ukv.reshape(MLA_KV_RANK, MLA_HEADS, MLA_NOPE + MLA_V)
    wk = wkv[:, :, :MLA_NOPE].reshape(MLA_KV_RANK, MLA_HEADS * MLA_NOPE)
    wv = wkv[:, :, MLA_NOPE:].reshape(MLA_KV_RANK, MLA_HEADS * MLA_V)
    cos0, sin0 = _rope_tables(MLA_ROPE)
    q, k, v = _mla_proj(cqkv, row(l0_q_norm), row(l0_kv_norm), wq.astype(BF16), wk.astype(BF16), wv.astype(BF16),
                        cos0, sin0)
    att = _mla_attn(q, k, v)
    rnn = _rglru(xr, gr, l0_conv_w, l0_conv_b, l0_gate_a_w, l0_gate_a_b, l0_gate_x_w, l0_gate_x_b, l0_lru_lambda)
    n0 = TOK // TM
    xs, h, info, counts = _out_proj(att, rnn, x_in, l0_w_out.astype(BF16), mod0, row(l0_norm_ffn), *router,
                                    _comb_tile, n0)
    dest, ys = _experts(h, info, counts, l0_exp_gate, l0_exp_up, l0_exp_down)

    cos1, sin1 = _rope_tables(HEAD_DIM)
    xs, qw, kw, vw, qn, kn, vn = _in1(dest, ys, xs, info.T, mod0, row(l1_norm_mix), mod1, l1_w_in.astype(BF16),
                                      cos1, sin1)
    win = _win_attn(l1_sink.astype(F32), qw, kw, vw)
    na = _na_attn(qn, kn, vn, _na_bias_table(l1_rpb))
    n1 = BATCH * LAT_TPB
    xl, h, info, counts = _out_proj(win, na, xs, l1_w_out.astype(BF16), mod1, row(l1_norm_ffn), *router,
                                    _lat_tile, n1)
    dest, ys = _experts(h, info, counts, l1_exp_gate, l1_exp_up, l1_exp_down)
    out = _combine(dest, ys, xl, info.T, mod1, row(final_norm), lambda i: (i, i // LAT_TPB), n1)
    return out.reshape(BATCH, SEQ, D_MODEL)
```
